```python
import math
import jax, jax.numpy as jnp
from jax import lax
import numpy as np

D_MODEL = 2048
BATCH = 16
SEQ = 256
DEPTH = 4
DEC_BATCH = 2
DEC_SEQ = 4096
PAST_LEN = 512

GRID_W = 64
ROPE_BASE = 10000.0
Q_BLOCK = 128
HG_HEADS = 8
HG_DK = 128
HG_DV = 128
HG_CHUNK = 32
MLA_HEADS = 8
MLA_Q_RANK = 512
MLA_KV_RANK = 256
MLA_NOPE = 128
MLA_ROPE = 64
MLA_DV = 128
DF_HEADS = 8
DF_DH = 64
BRANCH_W = 1024
PEER_HEADS = 8
PEER_NKEYS = 128
PEER_EXPERTS = PEER_NKEYS * PEER_NKEYS
PEER_DQ = 256
PEER_TOPK = 16
PEER_BLOCK = 128
N_MOD = 6
DEEP_ALPHA = (2 * DEPTH) ** 0.25
DEEP_BETA = (8 * DEPTH) ** -0.25
IN_SPLITS = (HG_HEADS * HG_DK, HG_HEADS * HG_DK, HG_HEADS * HG_DK, HG_HEADS * HG_DV, HG_HEADS * HG_DV,
             MLA_Q_RANK, MLA_KV_RANK, MLA_ROPE,
             DF_HEADS * 2 * DF_DH, DF_HEADS * 2 * DF_DH, DF_HEADS * 2 * DF_DH,
             D_MODEL, D_MODEL, D_MODEL)
IN_WIDTH = sum(IN_SPLITS)

kernel_name = 'hybrid_diffusion_prefix_trunk_step'


def layer_norm(x, g=None, b=None, eps=1e-5):
    xf = x.astype(jnp.float32)
    xc = xf - jnp.mean(xf, axis=-1, keepdims=True)
    y = xc * lax.rsqrt(jnp.mean(xc * xc, axis=-1, keepdims=True) + eps)
    if g is not None:
        y = y * g.astype(jnp.float32) + b.astype(jnp.float32)
    return y.astype(x.dtype)


def rms_norm(x, g, eps=1e-6):
    xf = x.astype(jnp.float32)
    y = xf * lax.rsqrt(jnp.mean(xf * xf, axis=-1, keepdims=True) + eps) * g.astype(jnp.float32)
    return y.astype(x.dtype)


def split_cols(a, sizes):
    out, start = [], 0
    for s in sizes:
        out.append(a[..., start:start + s])
        start += s
    return out


def axial_rope_tables(n_tok, dim):
    n_rows = n_tok // GRID_W
    row = jnp.repeat(jnp.arange(n_rows, dtype=jnp.float32), GRID_W)
    col = jnp.tile(jnp.arange(GRID_W, dtype=jnp.float32), n_rows)
    quarter = dim // 4
    inv_freq = ROPE_BASE ** (-jnp.arange(quarter, dtype=jnp.float32) / quarter)
    ang = jnp.stack([row[:, None] * inv_freq, col[:, None] * inv_freq], axis=1)
    ang = jnp.concatenate([ang, ang], axis=-1).reshape(n_tok, dim)
    return jnp.cos(ang), jnp.sin(ang)


def apply_rope(x, cos, sin):
    n_tok, dim = cos.shape
    shape = (1, n_tok) + (1,) * (x.ndim - 3) + (dim,)
    xs = x.reshape(x.shape[:-1] + (2, 2, dim // 4))
    rot = jnp.concatenate([-xs[..., 1:, :], xs[..., :1, :]], axis=-2).reshape(x.shape)
    return x * cos.reshape(shape).astype(x.dtype) + rot * sin.reshape(shape).astype(x.dtype)


def block_attention(q, k, v, scale):
    B, Tq, H, dq = q.shape
    nb = Tq // Q_BLOCK
    qb = q.reshape(B, nb, Q_BLOCK, H, dq).swapaxes(0, 1)

    def one(qi):
        s = jnp.einsum('bqhd,bkhd->bhqk', qi, k).astype(jnp.float32) * scale
        p = jax.nn.softmax(s, axis=-1).astype(v.dtype)
        return jnp.einsum('bhqk,bkhd->bqhd', p, v)

    o = lax.map(one, qb)
    return o.swapaxes(0, 1).reshape(B, Tq, H, v.shape[-1])


def block_diff_attention(q, k, v, lam, scale):
    B, Tq, H, _, dh = q.shape
    nb = Tq // Q_BLOCK
    qb = q.reshape(B, nb, Q_BLOCK, H, 2, dh).swapaxes(0, 1)

    def one(qi):
        s = jnp.einsum('bqhmd,bkhmd->bhmqk', qi, k).astype(jnp.float32) * scale
        p = jax.nn.softmax(s, axis=-1)
        w = (p[:, :, 0] - lam * p[:, :, 1]).astype(v.dtype)
        return jnp.einsum('bhqk,bkhd->bqhd', w, v)

    o = lax.map(one, qb)
    return o.swapaxes(0, 1).reshape(B, Tq, H, v.shape[-1])


def hgrn_forget(fz, lb):
    B, T, _ = fz.shape
    fz = fz.astype(jnp.float32)
    lb = lb.astype(jnp.float32)
    log_f = jnp.logaddexp(jnp.log(lb), jnp.log1p(-lb) + jax.nn.log_sigmoid(fz))
    k = (1.0 - lb) * jax.nn.sigmoid(-fz)
    return log_f.reshape(B, T, HG_HEADS, HG_DK), k.reshape(B, T, HG_HEADS, HG_DK)


def hgrn_chunk_scan(q, log_f, k, v, s0):
    B, T, H, dk = q.shape
    dv = v.shape[-1]
    n = T // HG_CHUNK
    chunks = lambda a: a.reshape(B, n, HG_CHUNK, H, a.shape[-1]).swapaxes(0, 1)
    causal = jnp.tril(jnp.ones((HG_CHUNK, HG_CHUNK), dtype=bool))[None, :, :, None, None]

    def step(S, inp):
        qc, lfc, kc, vc = inp
        b = jnp.cumsum(lfc, axis=1)
        o = jnp.einsum('bthk,bhkv->bthv', qc * jnp.exp(b), S)
        decay = jnp.exp(jnp.where(causal, b[:, :, None] - b[:, None, :], -jnp.inf))
        att = jnp.sum(qc[:, :, None] * kc[:, None, :] * decay, axis=-1)
        o = o + jnp.einsum('btsh,bshv->bthv', att, vc)
        b_last = b[:, -1]
        S = jnp.exp(b_last)[..., None] * S + jnp.einsum('bshk,bshv->bhkv', kc * jnp.exp(b_last[:, None] - b), vc)
        return S, o

    S, o = lax.scan(step, s0, (chunks(q), chunks(log_f), chunks(k), chunks(v)))
    return o.swapaxes(0, 1).reshape(B, T, H, dv), S


def peer_ffn(h, wq, sub_keys, u_tab, v_tab):
    B, T, D = h.shape
    n_tok = B * T
    xt = h.reshape(n_tok, D)
    q = (xt @ wq).reshape(n_tok, PEER_HEADS, 2, PEER_DQ // 2)
    s = jnp.einsum('nhpd,pkd->nhpk', q, sub_keys).astype(jnp.float32)
    v1, i1 = lax.top_k(s[:, :, 0], PEER_TOPK)
    v2, i2 = lax.top_k(s[:, :, 1], PEER_TOPK)
    cand = (v1[..., :, None] + v2[..., None, :]).reshape(n_tok, PEER_HEADS, PEER_TOPK * PEER_TOPK)
    cidx = (i1[..., :, None] * PEER_NKEYS + i2[..., None, :]).reshape(n_tok, PEER_HEADS, PEER_TOPK * PEER_TOPK)
    top, pos = lax.top_k(cand, PEER_TOPK)
    eidx = jnp.take_along_axis(cidx, pos, axis=-1).reshape(n_tok, PEER_HEADS * PEER_TOPK)
    gates = jax.nn.softmax(top, axis=-1).reshape(n_tok, PEER_HEADS * PEER_TOPK).astype(h.dtype)
    nb = n_tok // PEER_BLOCK
    xb = xt.reshape(nb, PEER_BLOCK, D)
    ib = eidx.reshape(nb, PEER_BLOCK, PEER_HEADS * PEER_TOPK)
    gb = gates.reshape(nb, PEER_BLOCK, PEER_HEADS * PEER_TOPK)

    def expert_block(args):
        x_blk, i_blk, g_blk = args
        u = jnp.take(u_tab, i_blk, axis=0)
        a = jax.nn.gelu(jnp.einsum('tkd,td->tk', u, x_blk), approximate=False)
        return jnp.einsum('tk,tkd->td', g_blk * a, jnp.take(v_tab, i_blk, axis=0))

    out = lax.map(expert_block, (xb, ib, gb))
    return out.reshape(B, T, D)


def trunk_layer(x, mod, p, lb, lam_init, ctx):
    f32 = jnp.float32
    B, T, _ = x.shape
    shift1, scale1, gate1, shift2, scale2, gate2 = jnp.split(mod[:, None, :], N_MOD, axis=-1)
    h = layer_norm(x) * (1 + scale1) + shift1
    (hq, hf_fwd, hf_bwd, hi, hg, cq, ckv, kpe, dq, dk, dv, ga, gb, gc) = split_cols(h @ p['w_in'], IN_SPLITS)

    q_h = jax.nn.silu(hq.astype(f32)).reshape(B, T, HG_HEADS, HG_DK)
    v_h = hi.astype(f32).reshape(B, T, HG_HEADS, HG_DV)
    lf_f, k_f = hgrn_forget(hf_fwd, lb[0])
    lf_b, k_b = hgrn_forget(hf_bwd, lb[1])
    if ctx is None:
        s0 = jnp.zeros((B, 2, HG_HEADS, HG_DK, HG_DV), f32)
    else:
        s0 = ctx[4].astype(f32)
    rev = lambda a: jnp.flip(a, axis=1)
    o_f, s_f = hgrn_chunk_scan(q_h, lf_f, k_f, v_h, s0[:, 0])
    o_b, s_b = hgrn_chunk_scan(rev(q_h), rev(lf_b), rev(k_b), rev(v_h), s0[:, 1])
    o_h = rms_norm(o_f + rev(o_b), p['hg_norm'].reshape(HG_HEADS, HG_DV))
    o_h = (o_h * jax.nn.silu(hg.astype(f32)).reshape(B, T, HG_HEADS, HG_DV)).astype(x.dtype).reshape(B, T, BRANCH_W)

    q_m = (rms_norm(cq, p['mla_q_norm']) @ p['w_uq']).reshape(B, T, MLA_HEADS, MLA_NOPE + MLA_ROPE)
    q_nope, q_pe = q_m[..., :MLA_NOPE], q_m[..., MLA_NOPE:]
    ckv_n = rms_norm(ckv, p['mla_kv_norm'])
    q_d = dq.reshape(B, T, DF_HEADS, 2, DF_DH)
    k_d = dk.reshape(B, T, DF_HEADS, 2, DF_DH)
    v_d = dv.reshape(B, T, DF_HEADS, 2 * DF_DH)
    if ctx is None:
        ckv_all, kpe_all, kd_all, vd_all = ckv_n, kpe, k_d, v_d
    else:
        L = ctx[0].shape[1]
        cos_m, sin_m = axial_rope_tables(T, MLA_ROPE)
        cos_d, sin_d = axial_rope_tables(T, DF_DH)
        q_pe = apply_rope(q_pe, cos_m, sin_m)
        q_d = apply_rope(q_d, cos_d, sin_d)
        ckv_all = jnp.concatenate([ctx[0], ckv_n], axis=1)
        kpe_all = jnp.concatenate([ctx[1], apply_rope(kpe, cos_m, sin_m)], axis=1)
        kd_all = jnp.concatenate([ctx[2].reshape(B, L, DF_HEADS, 2, DF_DH), apply_rope(k_d, cos_d, sin_d)], axis=1)
        vd_all = jnp.concatenate([ctx[3], v_d], axis=1)
    Tk = ckv_all.shape[1]
    kv = (ckv_all @ p['w_ukv']).reshape(B, Tk, MLA_HEADS, MLA_NOPE + MLA_DV)
    k_m = jnp.concatenate([kv[..., :MLA_NOPE],
                           jnp.broadcast_to(kpe_all[:, :, None, :], (B, Tk, MLA_HEADS, MLA_ROPE))], axis=-1)
    o_m = block_attention(jnp.concatenate([q_nope, q_pe], axis=-1), k_m, kv[..., MLA_NOPE:],
                          (MLA_NOPE + MLA_ROPE) ** -0.5).reshape(B, T, BRANCH_W)
    lq = p['df_lambda'].astype(f32)
    lam = jnp.exp(jnp.sum(lq[0] * lq[1])) - jnp.exp(jnp.sum(lq[2] * lq[3])) + lam_init
    o_d = block_diff_attention(q_d, kd_all, vd_all, lam, DF_DH ** -0.5)
    o_d = (rms_norm(o_d, p['df_norm']) * (1.0 - lam_init)).reshape(B, T, BRANCH_W)

    w_br = p['w_branch']
    merged = (jax.nn.sigmoid(ga) * (o_h @ w_br[0]) + jax.nn.sigmoid(gb) * (o_m @ w_br[1])
              + jax.nn.sigmoid(gc) * (o_d @ w_br[2]))
    x = layer_norm(DEEP_ALPHA * x + gate1 * (merged @ p['w_out']), p['ln1_g'], p['ln1_b'])

    h2 = layer_norm(x) * (1 + scale2) + shift2
    y2 = peer_ffn(h2, p['peer_wq'], p['peer_subkeys'], p['peer_u'], p['peer_v'])
    x = layer_norm(DEEP_ALPHA * x + gate2 * y2, p['ln2_g'], p['ln2_b'])
    if ctx is None:
        return x, (ckv_n, kpe, k_d.reshape(B, T, DF_HEADS, 2 * DF_DH), v_d, jnp.stack([s_f, s_b], axis=1))
    return x, None


def setup_inputs(seed: int = 0) -> dict:
    key = jax.random.key(seed)
    ks = jax.random.split(key, 40)
    f32 = jnp.float32
    nrm = lambda k, shape, std: jax.random.normal(k, shape, f32) * std
    D = D_MODEL
    return {
        'x_prompt': nrm(ks[0], (BATCH, SEQ, D), 1.0),
        'x_sample': nrm(ks[1], (DEC_BATCH, DEC_SEQ, D), 1.0),
        'cache_mla_ckv': nrm(ks[2], (DEC_BATCH, DEPTH, PAST_LEN, MLA_KV_RANK), 1.0),
        'cache_mla_kpe': nrm(ks[3], (DEC_BATCH, DEPTH, PAST_LEN, MLA_ROPE), 1.0),
        'cache_diff_k': nrm(ks[4], (DEC_BATCH, DEPTH, PAST_LEN, DF_HEADS, 2 * DF_DH), 1.0),
        'cache_diff_v': nrm(ks[5], (DEC_BATCH, DEPTH, PAST_LEN, DF_HEADS, 2 * DF_DH), 1.0),
        'state_hgrn': nrm(ks[6], (DEC_BATCH, DEPTH, 2, HG_HEADS, HG_DK, HG_DV), 1.0),
        'c': nrm(ks[7], (DEC_BATCH, D), 1.0),
        'c_ctx': nrm(ks[8], (D,), 1.0),
        'w_ada': nrm(ks[9], (DEPTH, D, N_MOD * D), 0.5 * D ** -0.5),
        'b_ada': nrm(ks[10], (DEPTH, N_MOD * D), 0.01),
        'w_in': nrm(ks[11], (DEPTH, D, IN_WIDTH), D ** -0.5),
        'hg_lb_logits': nrm(ks[12], (DEPTH, 2, HG_HEADS * HG_DK), 1.0),
        'hg_norm': 1.0 + nrm(ks[13], (DEPTH, HG_HEADS * HG_DV), 0.01),
        'mla_q_norm': 1.0 + nrm(ks[14], (DEPTH, MLA_Q_RANK), 0.01),
        'w_uq': nrm(ks[15], (DEPTH, MLA_Q_RANK, MLA_HEADS * (MLA_NOPE + MLA_ROPE)), MLA_Q_RANK ** -0.5),
        'mla_kv_norm': 1.0 + nrm(ks[16], (DEPTH, MLA_KV_RANK), 0.01),
        'w_ukv': nrm(ks[17], (DEPTH, MLA_KV_RANK, MLA_HEADS * (MLA_NOPE + MLA_DV)), MLA_KV_RANK ** -0.5),
        'df_lambda': nrm(ks[18], (DEPTH, 4, DF_DH), 0.1),
        'df_norm': 1.0 + nrm(ks[19], (DEPTH, 2 * DF_DH), 0.01),
        'w_branch': nrm(ks[20], (DEPTH, 3, BRANCH_W, D), DEEP_BETA * BRANCH_W ** -0.5),
        'w_out': nrm(ks[21], (DEPTH, D, D), DEEP_BETA * D ** -0.5),
        'ln1_g': 1.0 + nrm(ks[22], (DEPTH, D), 0.01),
        'ln1_b': nrm(ks[23], (DEPTH, D), 0.01),
        'ln2_g': 1.0 + nrm(ks[24], (DEPTH, D), 0.01),
        'ln2_b': nrm(ks[25], (DEPTH, D), 0.01),
        'peer_wq': nrm(ks[26], (DEPTH, D, PEER_HEADS * PEER_DQ), D ** -0.5),
        'peer_subkeys': nrm(ks[27], (DEPTH, 2, PEER_NKEYS, PEER_DQ // 2), (PEER_DQ // 2) ** -0.5),
        'peer_u': nrm(ks[28], (DEPTH, PEER_EXPERTS, D), D ** -0.5),
        'peer_v': nrm(ks[29], (DEPTH, PEER_EXPERTS, D), DEEP_BETA),
    }


def reference(x_prompt, x_sample, cache_mla_ckv, cache_mla_kpe, cache_diff_k, cache_diff_v, state_hgrn,
              c, c_ctx, w_ada, b_ada, w_in, hg_lb_logits, hg_norm, mla_q_norm, w_uq, mla_kv_norm, w_ukv,
              df_lambda, df_norm, w_branch, w_out, ln1_g, ln1_b, ln2_g, ln2_b,
              peer_wq, peer_subkeys, peer_u, peer_v):
    lb_all = jnp.cumsum(jax.nn.softmax(hg_lb_logits.astype(jnp.float32), axis=0), axis=0)
    lb_all = lb_all - lb_all[0:1]

    def layer_params(l):
        return {'w_in': w_in[l], 'hg_norm': hg_norm[l], 'mla_q_norm': mla_q_norm[l], 'w_uq': w_uq[l],
                'mla_kv_norm': mla_kv_norm[l], 'w_ukv': w_ukv[l], 'df_lambda': df_lambda[l], 'df_norm': df_norm[l],
                'w_branch': w_branch[l], 'w_out': w_out[l], 'ln1_g': ln1_g[l], 'ln1_b': ln1_b[l],
                'ln2_g': ln2_g[l], 'ln2_b': ln2_b[l], 'peer_wq': peer_wq[l], 'peer_subkeys': peer_subkeys[l],
                'peer_u': peer_u[l], 'peer_v': peer_v[l]}

    x = x_prompt
    ckv_l, kpe_l, dk_l, dv_l, st_l = [], [], [], [], []
    for l in range(DEPTH):
        lam_init = 0.8 - 0.6 * math.exp(-0.3 * l)
        mod = (jax.nn.silu(c_ctx) @ w_ada[l] + b_ada[l])[None, :]
        x, ctx_out = trunk_layer(x, mod, layer_params(l), lb_all[l], lam_init, None)
        ckv_l.append(ctx_out[0])
        kpe_l.append(ctx_out[1])
        dk_l.append(ctx_out[2])
        dv_l.append(ctx_out[3])
        st_l.append(ctx_out[4])
    y_prompt = x
    new_mla_ckv = jnp.stack(ckv_l, axis=1)
    new_mla_kpe = jnp.stack(kpe_l, axis=1)
    new_diff_k = jnp.stack(dk_l, axis=1)
    new_diff_v = jnp.stack(dv_l, axis=1)
    new_state_hgrn = jnp.stack(st_l, axis=1)

    x = x_sample
    for l in range(DEPTH):
        lam_init = 0.8 - 0.6 * math.exp(-0.3 * l)
        mod = jax.nn.silu(c) @ w_ada[l] + b_ada[l]
        ctx = (cache_mla_ckv[:, l], cache_mla_kpe[:, l], cache_diff_k[:, l], cache_diff_v[:, l], state_hgrn[:, l])
        x, _ = trunk_layer(x, mod, layer_params(l), lb_all[l], lam_init, ctx)
    y_sample = x

    return (y_prompt, y_sample, new_mla_ckv, new_mla_kpe, new_diff_k, new_diff_v, new_state_hgrn)
```

```python
import functools
import math

import jax
import jax.numpy as jnp
import numpy as np
from jax import lax
from jax.experimental import pallas as pl
from jax.experimental.pallas import tpu as pltpu

F32 = jnp.float32
BF16 = jnp.bfloat16

GRID_W = 64
ROPE_BASE = 10000.0
HEADS = 8
HEAD_W = 128
HG_CHUNK = 32
MLA_Q_RANK = 512
MLA_KV_RANK = 256
MLA_NOPE = 128
MLA_ROPE = 64
MLA_QK_W = 256
DF_DH = 64
BRANCH_W = HEADS * HEAD_W
PEER_NKEYS = 128
PEER_TOPK = 16
N_MOD = 6
LANES = 128
VMEM_LIMIT = 52 * 1024 * 1024

COL_HQ, COL_FF, COL_FB, COL_HI, COL_HG = 0, 1024, 2048, 3072, 4096
COL_DQ, COL_DK, COL_DV = 5120, 6144, 7168
COL_GA = 8192
COL_CQ, COL_CKV, COL_KPE = 14336, 14848, 15104
IN_W_PAD = 15360

NT_DIMS = (((1,), (1,)), ((), ()))
TN_DIMS = (((0,), (0,)), ((), ()))


def _params(sem, vmem=VMEM_LIMIT):
    return pltpu.CompilerParams(dimension_semantics=sem, vmem_limit_bytes=vmem)


def _layer_norm(x, eps=1e-5):
    xc = x - jnp.mean(x, axis=-1, keepdims=True)
    return xc * lax.rsqrt(jnp.mean(xc * xc, axis=-1, keepdims=True) + eps)


def _rms_norm(x, g, eps=1e-6):
    return x * lax.rsqrt(jnp.mean(x * x, axis=-1, keepdims=True) + eps) * g


def _silu(x):
    return x * jax.nn.sigmoid(x)


def _gelu(x):
    return 0.5 * x * (1.0 + lax.erf(x * (2.0 ** -0.5)))


def _softmax_rows(s):
    e = jnp.exp(s - jnp.max(s, axis=-1, keepdims=True))
    return e / jnp.sum(e, axis=-1, keepdims=True)


def _ada_kernel(c_ref, w_ref, b_ref, o_ref):
    s = _silu(c_ref[...]).astype(BF16)
    o_ref[0] = jnp.dot(s, w_ref[0].astype(BF16), preferred_element_type=F32) + b_ref[0]


def _ada_call(cvec, w_ada, b_ada):
    depth, d, w = w_ada.shape
    tn = 1024
    return pl.pallas_call(
        _ada_kernel,
        grid=(depth, w // tn),
        in_specs=[pl.BlockSpec((8, d), lambda l, j: (0, 0)),
                  pl.BlockSpec((1, d, tn), lambda l, j: (l, 0, j)),
                  pl.BlockSpec((1, 1, tn), lambda l, j: (l, 0, j))],
        out_specs=pl.BlockSpec((1, 8, tn), lambda l, j: (l, 0, j)),
        out_shape=jax.ShapeDtypeStruct((depth, 8, w), F32),
        compiler_params=_params(("parallel", "parallel")),
        name="ada",
    )(cvec, w_ada, b_ada.reshape(depth, 1, w))


def _inproj_kernel(x_ref, mod_ref, w_ref, o_ref, h_scr):
    @pl.when(pl.program_id(1) == 0)
    def _():
        shift, scale = mod_ref[0, 0:1, :], mod_ref[0, 1:2, :]
        h_scr[...] = (_layer_norm(x_ref[...]) * (1.0 + scale) + shift).astype(BF16)

    o_ref[...] = jnp.dot(h_scr[...], w_ref[...], preferred_element_type=F32)


def _inproj_call(x, mod, w, mod_idx, tm):
    n, d = x.shape
    wp = w.shape[1]
    tn = 1024
    return pl.pallas_call(
        _inproj_kernel,
        grid=(n // tm, wp // tn),
        in_specs=[pl.BlockSpec((tm, d), lambda i, j: (i, 0)),
                  pl.BlockSpec((1, N_MOD, d), lambda i, j: (mod_idx(i), 0, 0)),
                  pl.BlockSpec((d, tn), lambda i, j: (0, j))],
        out_specs=pl.BlockSpec((tm, tn), lambda i, j: (i, j)),
        out_shape=jax.ShapeDtypeStruct((n, wp), F32),
        scratch_shapes=[pltpu.VMEM((tm, d), BF16)],
        compiler_params=_params(("parallel", "arbitrary")),
        name="inproj",
    )(x, mod, w)


def _hgrn_chunk(direction, q, z, v, lbp, tri, emat, st):
    c = HG_CHUNK
    loglb, log1mlb, omlb = lbp[0:1, :], lbp[1:2, :], lbp[2:3, :]
    log_f = jnp.logaddexp(loglb, log1mlb + jax.nn.log_sigmoid(z))
    k = omlb * jax.nn.sigmoid(-z)
    q = _silu(q)

    hi = log_f.astype(BF16)
    r1 = log_f - hi.astype(F32)
    mid = r1.astype(BF16)
    lo = (r1 - mid.astype(F32)).astype(BF16)
    cs = jnp.dot(tri, jnp.concatenate([hi, mid, lo], axis=1), preferred_element_type=F32)
    b = cs[:, 0:LANES] + cs[:, LANES:2 * LANES] + cs[:, 2 * LANES:3 * LANES]
    total = b[c - 1:c, :] if direction == 0 else b[0:1, :]

    qe = (q * jnp.exp(b)).astype(BF16)
    kd = (k * jnp.exp(total - b)).astype(BF16)
    vb = v.astype(BF16)
    o = lax.dot_general(qe, st.astype(BF16), NT_DIMS, preferred_element_type=F32)

    row = lax.broadcasted_iota(jnp.int32, (8, LANES), 0)
    cols = []
    for s in range(c):
        pieces = []
        for j in range(c // 8):
            lo_t, hi_t = 8 * j, 8 * j + 7
            if direction == 0:
                dead, full = hi_t < s, lo_t >= s
            else:
                dead, full = lo_t > s, hi_t <= s
            if dead:
                pieces.append(jnp.zeros((8, LANES), F32))
                continue
            val = q[lo_t:lo_t + 8, :] * jnp.exp(b[lo_t:lo_t + 8, :] - b[s:s + 1, :]) * k[s:s + 1, :]
            if not full:
                keep = (row + lo_t >= s) if direction == 0 else (row + lo_t <= s)
                val = jnp.where(keep, val, 0.0)
            pieces.append(val)
        cols.append(jnp.concatenate(pieces, axis=0))
    pcat = jnp.concatenate(cols, axis=1).astype(BF16)
    att = jnp.dot(pcat, emat, preferred_element_type=F32)
    o = o + jnp.dot(att.astype(BF16), vb, preferred_element_type=F32)

    st = st * jnp.exp(total) + lax.dot_general(vb, kd, TN_DIMS, preferred_element_type=F32)
    return o, st


def _hgrn_kernel(q_ref, ff_ref, fb_ref, v_ref, g_ref, lbp_ref, gn_ref, s0_ref, tri_ref, e_ref,
                 o_ref, sout_ref, of_scr, ob_scr, st_scr, *, seq):
    c = HG_CHUNK
    n = seq // c
    st_scr[0] = s0_ref[0, 0, 0].T
    st_scr[1] = s0_ref[0, 1, 0].T

    def body(i, carry):
        for direction, f_ref, acc in ((0, ff_ref, of_scr), (1, fb_ref, ob_scr)):
            r0 = pl.multiple_of((i if direction == 0 else n - 1 - i) * c, c)
            rows = pl.ds(r0, c)
            o, st = _hgrn_chunk(direction, q_ref[rows, :], f_ref[rows, :], v_ref[rows, :],
                                lbp_ref[direction], tri_ref[direction], e_ref[...], st_scr[direction])
            acc[rows, :] = o
            st_scr[direction] = st
        return carry

    lax.fori_loop(0, n, body, 0)
    sout_ref[0, 0, 0] = st_scr[0].T
    sout_ref[0, 1, 0] = st_scr[1].T

    rt = min(seq, 256)

    def fin(i, carry):
        rows = pl.ds(pl.multiple_of(i * rt, rt), rt)
        o = _rms_norm(of_scr[rows, :] + ob_scr[rows, :], gn_ref[...])
        o_ref[rows, :] = (o * _silu(g_ref[rows, :])).astype(BF16)
        return carry

    lax.fori_loop(0, seq // rt, fin, 0)


def _hgrn_call(y, lbp, gnorm, s0, tri, emat, seq, nseq, row_block0):
    def col(c0):
        return pl.BlockSpec((seq, HEAD_W), lambda b, h, c0=c0: (row_block0 + b, c0 // HEAD_W + h))

    state_spec = pl.BlockSpec((1, 2, 1, HEAD_W, HEAD_W), lambda b, h: (b, 0, h, 0, 0))
    return pl.pallas_call(
        functools.partial(_hgrn_kernel, seq=seq),
        grid=(nseq, HEADS),
        in_specs=[col(COL_HQ), col(COL_FF), col(COL_FB), col(COL_HI), col(COL_HG),
                  pl.BlockSpec((2, 3, HEAD_W), lambda b, h: (0, 0, h)),
                  pl.BlockSpec((1, HEAD_W), lambda b, h: (0, h)),
                  state_spec,
                  pl.BlockSpec((2, HG_CHUNK, HG_CHUNK), lambda b, h: (0, 0, 0)),
                  pl.BlockSpec((HG_CHUNK * HEAD_W, HG_CHUNK), lambda b, h: (0, 0))],
        out_specs=[pl.BlockSpec((seq, HEAD_W), lambda b, h: (b, h)), state_spec],
        out_shape=[jax.ShapeDtypeStruct((nseq * seq, BRANCH_W), BF16),
                   jax.ShapeDtypeStruct((nseq, 2, HEADS, HEAD_W, HEAD_W), F32)],
        scratch_shapes=[pltpu.VMEM((seq, HEAD_W), F32), pltpu.VMEM((seq, HEAD_W), F32),
                        pltpu.VMEM((2, HEAD_W, HEAD_W), F32)],
        compiler_params=_params(("parallel", "parallel")),
        name="hgrn",
    )(y, y, y, y, y, lbp, gnorm, s0, tri, emat)


def _rope(x, cos, sin_signed, lo_half):
    rot = jnp.where(lo_half, pltpu.roll(x, LANES - 16, 1), pltpu.roll(x, 16, 1))
    return x * cos + rot * sin_signed


def _prep_kernel(cq_ref, ckv_ref, kpe_ref, dq_ref, dk_ref, qn_ref, kvn_ref, wuq_ref,
                 cosm_ref, sinm_ref, cosd_ref, sind_ref,
                 qcat_ref, ckvn_ref, kper_ref, qd_ref, kd_ref):
    lo_half = (lax.broadcasted_iota(jnp.int32, (1, LANES), 1) % 32) < 16
    cq = _rms_norm(cq_ref[...], qn_ref[...]).astype(BF16)
    qm = jnp.dot(cq, wuq_ref[...], preferred_element_type=F32)
    cos_pe, sin_pe = cosm_ref[...], sinm_ref[...]
    for h in range(HEADS):
        c0 = h * MLA_QK_W
        qcat_ref[:, c0:c0 + LANES] = qm[:, c0:c0 + LANES].astype(BF16)
        pe = _rope(qm[:, c0 + LANES:c0 + 2 * LANES], cos_pe, sin_pe, lo_half)
        qcat_ref[:, c0 + LANES:c0 + 2 * LANES] = pe.astype(BF16)
    ckvn_ref[...] = _rms_norm(ckv_ref[...], kvn_ref[...])
    kper_ref[...] = _rope(kpe_ref[...], cos_pe, sin_pe, lo_half)
    cos_d, sin_d = cosd_ref[...], sind_ref[...]
    for h in range(HEADS):
        cs = slice(h * LANES, (h + 1) * LANES)
        qd_ref[:, cs] = _rope(dq_ref[:, cs], cos_d, sin_d, lo_half).astype(BF16)
        kd_ref[:, cs] = _rope(dk_ref[:, cs], cos_d, sin_d, lo_half)


def _prep_call(y, qn, kvn, wuq, cosm, sinm, cosd, sind, tm):
    n = y.shape[0]
    row = lambda w, c0: pl.BlockSpec((tm, w), lambda i, c0=c0, w=w: (i, c0 // w))
    full = lambda a: pl.BlockSpec(a.shape, lambda i: (0, 0))
    tab = pl.BlockSpec((tm, LANES), lambda i: (i, 0))
    out = lambda w: pl.BlockSpec((tm, w), lambda i: (i, 0))
    return pl.pallas_call(
        _prep_kernel,
        grid=(n // tm,),
        in_specs=[row(MLA_Q_RANK, COL_CQ), row(MLA_KV_RANK, COL_CKV), row(LANES, COL_KPE),
                  row(BRANCH_W, COL_DQ), row(BRANCH_W, COL_DK), full(qn), full(kvn), full(wuq),
                  tab, tab, tab, tab],
        out_specs=[out(HEADS * MLA_QK_W), out(MLA_KV_RANK), out(LANES), out(BRANCH_W), out(BRANCH_W)],
        out_shape=[jax.ShapeDtypeStruct((n, HEADS * MLA_QK_W), BF16),
                   jax.ShapeDtypeStruct((n, MLA_KV_RANK), F32),
                   jax.ShapeDtypeStruct((n, LANES), F32),
                   jax.ShapeDtypeStruct((n, BRANCH_W), BF16),
                   jax.ShapeDtypeStruct((n, BRANCH_W), F32)],
        compiler_params=_params(("parallel",)),
        name="prep",
    )(y, y, y, y, y, qn, kvn, wuq, cosm, sinm, cosd, sind)


def _kvup_kernel(ckv_ref, kpe_ref, w_ref, kcat_ref, v_ref):
    kv = jnp.dot(ckv_ref[...], w_ref[...], preferred_element_type=F32)
    kpe = kpe_ref[...]
    for h in range(HEADS):
        c0 = h * MLA_QK_W
        kcat_ref[:, c0:c0 + LANES] = kv[:, h * LANES:(h + 1) * LANES].astype(BF16)
        kcat_ref[:, c0 + LANES:c0 + 2 * LANES] = kpe
    v_ref[...] = kv[:, BRANCH_W:].astype(BF16)


def _kvup_call(ckv_all, kpe_all, w, tk):
    r = ckv_all.shape[0]
    return pl.pallas_call(
        _kvup_kernel,
        grid=(r // tk,),
        in_specs=[pl.BlockSpec((tk, MLA_KV_RANK), lambda i: (i, 0)),
                  pl.BlockSpec((tk, LANES), lambda i: (i, 0)),
                  pl.BlockSpec(w.shape, lambda i: (0, 0))],
        out_specs=[pl.BlockSpec((tk, HEADS * MLA_QK_W), lambda i: (i, 0)),
                   pl.BlockSpec((tk, BRANCH_W), lambda i: (i, 0))],
        out_shape=[jax.ShapeDtypeStruct((r, HEADS * MLA_QK_W), BF16),
                   jax.ShapeDtypeStruct((r, BRANCH_W), BF16)],
        compiler_params=_params(("parallel",)),
        name="kvup",
    )(ckv_all, kpe_all, w)


def _mla_head(q, k, v):
    s = lax.dot_general(q, k, NT_DIMS, preferred_element_type=F32) * ((MLA_NOPE + MLA_ROPE) ** -0.5)
    return jnp.dot(_softmax_rows(s).astype(BF16), v, preferred_element_type=F32)


def _diff_head(q, k, v, lam, gn, out_scale):
    lane = lax.broadcasted_iota(jnp.int32, (1, LANES), 1)
    zero = jnp.zeros_like(q)
    q1 = jnp.where(lane < DF_DH, q, zero)
    q2 = jnp.where(lane >= DF_DH, q, zero)
    s1 = lax.dot_general(q1, k, NT_DIMS, preferred_element_type=F32) * (DF_DH ** -0.5)
    s2 = lax.dot_general(q2, k, NT_DIMS, preferred_element_type=F32) * (DF_DH ** -0.5)
    w = (_softmax_rows(s1) - lam * _softmax_rows(s2)).astype(BF16)
    o = jnp.dot(w, v, preferred_element_type=F32)
    return _rms_norm(o, gn) * out_scale


def _attn_kernel(qm_ref, km_ref, vm_ref, qd_ref, kd_ref, vd_ref, lam_ref, gn_ref, om_ref, od_ref,
                 *, heads, out_scale):
    lam = lam_ref[:, 0:1]
    gn = gn_ref[...]
    for h in range(heads):
        qk = slice(h * MLA_QK_W, (h + 1) * MLA_QK_W)
        hw = slice(h * HEAD_W, (h + 1) * HEAD_W)
        om_ref[:, hw] = _mla_head(qm_ref[:, qk], km_ref[:, qk], vm_ref[:, hw]).astype(BF16)
        od_ref[:, hw] = _diff_head(qd_ref[:, hw], kd_ref[:, hw], vd_ref[:, hw], lam, gn,
                                   out_scale).astype(BF16)


def _attn_call(qcat, qd, kcat, vm, kd, vd, lam, gn, out_scale, *, nseq, tq, q_tiles, q_block0,
               tk, k_block0, heads_per_step):
    hps = heads_per_step
    hg = HEADS // hps
    qmap = lambda b, h, i: (q_block0 + b * q_tiles + i, h)
    kmap = lambda b, h, i: (k_block0 + b, h)
    return pl.pallas_call(
        functools.partial(_attn_kernel, heads=hps, out_scale=out_scale),
        grid=(nseq, hg, q_tiles),
        in_specs=[pl.BlockSpec((tq, hps * MLA_QK_W), qmap),
                  pl.BlockSpec((tk, hps * MLA_QK_W), kmap),
                  pl.BlockSpec((tk, hps * HEAD_W), kmap),
                  pl.BlockSpec((tq, hps * HEAD_W), qmap),
                  pl.BlockSpec((tk, hps * HEAD_W), kmap),
                  pl.BlockSpec((tk, hps * HEAD_W), kmap),
                  pl.BlockSpec((1, LANES), lambda b, h, i: (0, 0)),
                  pl.BlockSpec((1, LANES), lambda b, h, i: (0, 0))],
        out_specs=[pl.BlockSpec((tq, hps * HEAD_W), lambda b, h, i: (b * q_tiles + i, h)),
                   pl.BlockSpec((tq, hps * HEAD_W), lambda b, h, i: (b * q_tiles + i, h))],
        out_shape=[jax.ShapeDtypeStruct((nseq * q_tiles * tq, BRANCH_W), BF16),
                   jax.ShapeDtypeStruct((nseq * q_tiles * tq, BRANCH_W), BF16)],
        compiler_params=_params(("parallel", "parallel", "arbitrary")),
        name="attn",
    )(qcat, kcat, vm, qd, kd, vd, lam, gn)


def _merge_kernel(oh_ref, om_ref, od_ref, ga_ref, gb_ref, gc_ref, w_ref, o_ref):
    acc = jax.nn.sigmoid(ga_ref[...]) * jnp.dot(oh_ref[...], w_ref[0], preferred_element_type=F32)
    acc += jax.nn.sigmoid(gb_ref[...]) * jnp.dot(om_ref[...], w_ref[1], preferred_element_type=F32)
    acc += jax.nn.sigmoid(gc_ref[...]) * jnp.dot(od_ref[...], w_ref[2], preferred_element_type=F32)
    o_ref[...] = acc.astype(BF16)


def _merge_call(oh, om, od, y, wbr, tm):
    n = oh.shape[0]
    d = wbr.shape[2]
    tn = 1024
    br = pl.BlockSpec((tm, BRANCH_W), lambda j, i: (i, 0))
    gate = lambda g: pl.BlockSpec((tm, tn), lambda j, i, g=g: (i, (COL_GA + g * d) // tn + j))
    return pl.pallas_call(
        _merge_kernel,
        grid=(d // tn, n // tm),
        in_specs=[br, br, br, gate(0), gate(1), gate(2),
                  pl.BlockSpec((3, BRANCH_W, tn), lambda j, i: (0, 0, j))],
        out_specs=pl.BlockSpec((tm, tn), lambda j, i: (i, j)),
        out_shape=jax.ShapeDtypeStruct((n, d), BF16),
        compiler_params=_params(("parallel", "parallel")),
        name="merge",
    )(oh, om, od, y, y, y, wbr)


def _outproj_kernel(m_ref, x_ref, mod_ref, w_ref, g_ref, b_ref, x1_ref, h2_ref, *, alpha):
    gate1 = mod_ref[0, 2:3, :]
    shift2, scale2 = mod_ref[0, 3:4, :], mod_ref[0, 4:5, :]
    y = jnp.dot(m_ref[...], w_ref[...], preferred_element_type=F32)
    x1 = _layer_norm(alpha * x_ref[...] + gate1 * y) * g_ref[...] + b_ref[...]
    x1_ref[...] = x1
    h2_ref[...] = (_layer_norm(x1) * (1.0 + scale2) + shift2).astype(BF16)


def _outproj_call(merged, x, mod, w, g, b, mod_idx, alpha, tm):
    n, d = x.shape
    row = pl.BlockSpec((tm, d), lambda i: (i, 0))
    vec = pl.BlockSpec((1, d), lambda i: (0, 0))
    return pl.pallas_call(
        functools.partial(_outproj_kernel, alpha=alpha),
        grid=(n // tm,),
        in_specs=[row, row, pl.BlockSpec((1, N_MOD, d), lambda i: (mod_idx(i), 0, 0)),
                  pl.BlockSpec((d, d), lambda i: (0, 0)), vec, vec],
        out_specs=[row, row],
        out_shape=[jax.ShapeDtypeStruct((n, d), F32), jax.ShapeDtypeStruct((n, d), BF16)],
        compiler_params=_params(("parallel",)),
        name="outproj",
    )(merged, x, mod, w, g, b)


def _peerq_kernel(h_ref, w_ref, sk_ref, st_ref):
    q = jnp.dot(h_ref[...], w_ref[...], preferred_element_type=F32).astype(BF16)
    for hp in range(2 * HEADS):
        st_ref[hp] = lax.dot_general(sk_ref[hp % 2], q[:, hp * LANES:(hp + 1) * LANES], NT_DIMS,
                                     preferred_element_type=F32)


def _peerq_call(h2, wq, subkeys, tm):
    n, d = h2.shape
    return pl.pallas_call(
        _peerq_kernel,
        grid=(n // tm,),
        in_specs=[pl.BlockSpec((tm, d), lambda i: (i, 0)),
                  pl.BlockSpec(wq.shape, lambda i: (0, 0)),
                  pl.BlockSpec(subkeys.shape, lambda i: (0, 0, 0))],
        out_specs=pl.BlockSpec((2 * HEADS, PEER_NKEYS, tm), lambda i: (0, 0, i)),
        out_shape=jax.ShapeDtypeStruct((2 * HEADS, PEER_NKEYS, n), F32),
        compiler_params=_params(("parallel",)),
        name="peerq",
    )(h2, wq, subkeys)


def _top_values(x, count):
    rows = lax.broadcasted_iota(jnp.int32, x.shape, 0).astype(F32)
    vals = []
    for r in range(count):
        m = jnp.max(x, axis=0, keepdims=True)
        vals.append(m)
        if r + 1 < count:
            first = jnp.min(jnp.where(x == m, rows, float(x.shape[0])), axis=0, keepdims=True)
            x = jnp.where(rows == first, -jnp.inf, x)
    return vals


def _topk_kernel(st_ref, stat_ref, v_scr, cand_scr):
    row8 = lax.broadcasted_iota(jnp.int32, (8, st_ref.shape[2]), 0)
    for h in range(HEADS):
        v1 = _top_values(st_ref[2 * h], PEER_TOPK)
        v2 = _top_values(st_ref[2 * h + 1], PEER_TOPK)
        for r in range(PEER_TOPK):
            v_scr[0, r:r + 1, :] = v1[r]
            v_scr[1, r:r + 1, :] = v2[r]
        cand_scr[0:16, :] = v1[0] + v_scr[1]
        for a in range(1, 8):
            pair = v1[a] + v_scr[1, 0:8, :]
            cand_scr[8 + 8 * a:16 + 8 * a, :] = jnp.where(row8 < PEER_TOPK // (a + 1), pair, -jnp.inf)
        cand_scr[72:80, :] = v_scr[0, 8:16, :] + v2[0]
        top = _top_values(cand_scr[...], PEER_TOPK)
        z = jnp.exp(top[0] - top[0])
        for r in range(1, PEER_TOPK):
            z = z + jnp.exp(top[r] - top[0])
        stat_ref[h, 0:1, :] = top[PEER_TOPK - 1]
        stat_ref[h, 1:2, :] = v1[0]
        stat_ref[h, 2:3, :] = v2[0]
        stat_ref[h, 3:4, :] = 1.0 / z
        stat_ref[h, 4:8, :] = jnp.zeros((4, st_ref.shape[2]), F32)


def _topk_call(st, tm):
    n = st.shape[2]
    return pl.pallas_call(
        _topk_kernel,
        grid=(n // tm,),
        in_specs=[pl.BlockSpec((2 * HEADS, PEER_NKEYS, tm), lambda i: (0, 0, i))],
        out_specs=pl.BlockSpec((HEADS, 8, tm), lambda i: (0, 0, i)),
        out_shape=jax.ShapeDtypeStruct((HEADS, 8, n), F32),
        scratch_shapes=[pltpu.VMEM((2, PEER_TOPK, tm), F32), pltpu.VMEM((80, tm), F32)],
        compiler_params=_params(("parallel",)),
        name="topk",
    )(st)


def _peer_kernel(h_ref, st_ref, stat_ref, u_ref, v_ref, o_ref, e_scr, *, rows_per_step):
    j = pl.program_id(1)

    @pl.when(j == 0)
    def _():
        o_ref[...] = jnp.zeros_like(o_ref)
        for h in range(HEADS):
            stat = stat_ref[h]
            e_scr[2 * h] = jnp.exp(st_ref[2 * h] - stat[1:2, :]) * stat[3:4, :]
            e_scr[2 * h + 1] = jnp.exp(st_ref[2 * h + 1] - stat[2:3, :])

    a_t = lax.dot_general(u_ref[...], h_ref[...], NT_DIMS, preferred_element_type=F32)
    parts = []
    for ii in range(rows_per_step):
        i = j * rows_per_step + ii
        g = jnp.zeros((PEER_NKEYS, h_ref.shape[0]), F32)
        for h in range(HEADS):
            s = st_ref[2 * h + 1] + st_ref[2 * h, pl.ds(i, 1), :]
            w = e_scr[2 * h + 1] * e_scr[2 * h, pl.ds(i, 1), :]
            g = g + jnp.where(s >= stat_ref[h, 0:1, :], w, 0.0)
        act = _gelu(a_t[ii * PEER_NKEYS:(ii + 1) * PEER_NKEYS, :])
        parts.append((g * act).astype(BF16))
    ga = jnp.concatenate(parts, axis=0)
    o_ref[...] += lax.dot_general(ga, v_ref[...], TN_DIMS, preferred_element_type=F32)


def _peer_call(h2, st, stats, u, v, tm, rows_per_step):
    n, d = h2.shape
    te = rows_per_step * PEER_NKEYS
    return pl.pallas_call(
        functools.partial(_peer_kernel, rows_per_step=rows_per_step),
        grid=(n // tm, u.shape[0] // te),
        in_specs=[pl.BlockSpec((tm, d), lambda i, j: (i, 0)),
                  pl.BlockSpec((2 * HEADS, PEER_NKEYS, tm), lambda i, j: (0, 0, i)),
                  pl.BlockSpec((HEADS, 8, tm), lambda i, j: (0, 0, i)),
                  pl.BlockSpec((te, d), lambda i, j: (j, 0)),
                  pl.BlockSpec((te, d), lambda i, j: (j, 0))],
        out_specs=pl.BlockSpec((tm, d), lambda i, j: (i, 0)),
        out_shape=jax.ShapeDtypeStruct((n, d), F32),
        scratch_shapes=[pltpu.VMEM((2 * HEADS, PEER_NKEYS, tm), F32)],
        compiler_params=_params(("parallel", "arbitrary")),
        name="peer",
    )(h2, st, stats, u, v)


def _ln2_kernel(x_ref, y_ref, mod_ref, g_ref, b_ref, o_ref, *, alpha):
    gate2 = mod_ref[0, 5:6, :]
    o_ref[...] = _layer_norm(alpha * x_ref[...] + gate2 * y_ref[...]) * g_ref[...] + b_ref[...]


def _ln2_call(x1, y2, mod, g, b, mod_idx, alpha, tm):
    n, d = x1.shape
    row = pl.BlockSpec((tm, d), lambda i: (i, 0))
    vec = pl.BlockSpec((1, d), lambda i: (0, 0))
    return pl.pallas_call(
        functools.partial(_ln2_kernel, alpha=alpha),
        grid=(n // tm,),
        in_specs=[row, row, pl.BlockSpec((1, N_MOD, d), lambda i: (mod_idx(i), 0, 0)), vec, vec],
        out_specs=row,
        out_shape=jax.ShapeDtypeStruct((n, d), F32),
        compiler_params=_params(("parallel",)),
        name="ln2",
    )(x1, y2, mod, g, b)


def _rope_tables(n_ctx, n_lat_seq, lat_seq):
    quarter = 16
    t = jnp.arange(lat_seq)
    rowp = (t // GRID_W).astype(F32)
    colp = (t % GRID_W).astype(F32)
    inv_freq = ROPE_BASE ** (-jnp.arange(quarter, dtype=F32) / quarter)
    ang = jnp.stack([rowp[:, None] * inv_freq, colp[:, None] * inv_freq], axis=1)
    ang = jnp.concatenate([ang, ang], axis=-1).reshape(lat_seq, 64)
    sign = jnp.where((jnp.arange(64) % 32) < 16, -1.0, 1.0).astype(F32)
    cos, sin = jnp.cos(ang), jnp.sin(ang) * sign
    ones, zeros = jnp.ones((lat_seq, 64), F32), jnp.zeros((lat_seq, 64), F32)

    def full(tab, ident):
        lat = jnp.tile(tab, (n_lat_seq, 1))
        return jnp.concatenate([jnp.full((n_ctx, LANES), ident, F32), lat], axis=0)

    cosd = full(jnp.concatenate([cos, cos], axis=1), 1.0)
    sind = full(jnp.concatenate([sin, sin], axis=1), 0.0)
    cosm = full(jnp.concatenate([cos, ones], axis=1), 1.0)
    sinm = full(jnp.concatenate([sin, zeros], axis=1), 0.0)
    return cosd, sind, cosm, sinm


def kernel(x_prompt, x_sample, cache_mla_ckv, cache_mla_kpe, cache_diff_k, cache_diff_v, state_hgrn,
           c, c_ctx, w_ada, b_ada, w_in, hg_lb_logits, hg_norm, mla_q_norm, w_uq, mla_kv_norm, w_ukv,
           df_lambda, df_norm, w_branch, w_out, ln1_g, ln1_b, ln2_g, ln2_b,
           peer_wq, peer_subkeys, peer_u, peer_v):
    nb, seq, d = x_prompt.shape
    db, dseq, _ = x_sample.shape
    depth = w_in.shape[0]
    past = cache_mla_ckv.shape[2]
    n_ctx, n_lat = nb * seq, db * dseq
    n = n_ctx + n_lat
    tk_lat = past + dseq
    tm = 512
    assert n_ctx % tm == 0 and dseq % tm == 0 and (db * tk_lat) % seq == 0 and n_ctx % dseq == 0
    alpha = (2 * depth) ** 0.25

    def mod_idx(i, tile=tm):
        return jnp.where(i < n_ctx // tile, 0, 1 + (i - n_ctx // tile) // (dseq // tile))

    lb = jnp.cumsum(jax.nn.softmax(hg_lb_logits.astype(F32), axis=0), axis=0)
    lb = lb - lb[0:1]
    lbp = jnp.stack([jnp.log(lb), jnp.log1p(-lb), 1.0 - lb], axis=2)
    lq = df_lambda.astype(F32)
    lam_init = np.array([0.8 - 0.6 * math.exp(-0.3 * l) for l in range(depth)], np.float32)
    lam = jnp.exp(jnp.sum(lq[:, 0] * lq[:, 1], axis=-1)) - jnp.exp(jnp.sum(lq[:, 2] * lq[:, 3], axis=-1)) + lam_init

    kpe0 = 5 * 1024 + MLA_Q_RANK + MLA_KV_RANK
    w_in_p = jnp.concatenate(
        [w_in[:, :, :5 * 1024], w_in[:, :, kpe0 + MLA_ROPE:], w_in[:, :, 5 * 1024:kpe0],
         w_in[:, :, kpe0:kpe0 + MLA_ROPE],
         jnp.zeros((depth, d, IN_W_PAD - w_in.shape[2]), w_in.dtype)], axis=2).astype(BF16)
    wq4 = w_uq.reshape(depth, MLA_Q_RANK, HEADS, MLA_NOPE + MLA_ROPE)
    w_uq_p = jnp.concatenate(
        [wq4, jnp.zeros((depth, MLA_Q_RANK, HEADS, MLA_QK_W - MLA_NOPE - MLA_ROPE), w_uq.dtype)],
        axis=3).reshape(depth, MLA_Q_RANK, HEADS * MLA_QK_W).astype(BF16)
    wkv4 = w_ukv.reshape(depth, MLA_KV_RANK, HEADS, 2 * HEAD_W)
    w_ukv_p = jnp.concatenate([wkv4[..., :HEAD_W].reshape(depth, MLA_KV_RANK, BRANCH_W),
                               wkv4[..., HEAD_W:].reshape(depth, MLA_KV_RANK, BRANCH_W)], axis=2).astype(BF16)
    w_branch_b, w_out_b, peer_wq_b = w_branch.astype(BF16), w_out.astype(BF16), peer_wq.astype(BF16)
    subkeys_b, peer_u_b, peer_v_b = peer_subkeys.astype(BF16), peer_u.astype(BF16), peer_v.astype(BF16)

    tri = jnp.stack([jnp.tril(jnp.ones((HG_CHUNK, HG_CHUNK), F32)),
                     jnp.triu(jnp.ones((HG_CHUNK, HG_CHUNK), F32))]).astype(BF16)
    emat = jnp.repeat(jnp.eye(HG_CHUNK, dtype=F32), HEAD_W, axis=0).astype(BF16)
    cosd, sind, cosm, sinm = _rope_tables(n_ctx, db, dseq)

    cvec = jnp.concatenate([c_ctx[None, :], c, jnp.zeros((8 - 1 - db, d), F32)], axis=0)
    mods = _ada_call(cvec, w_ada, b_ada)[:, :1 + db].reshape(depth, 1 + db, N_MOD, d)

    x = jnp.concatenate([x_prompt.reshape(n_ctx, d), x_sample.reshape(n_lat, d)], axis=0)
    zero_state = jnp.zeros((nb, 2, HEADS, HEAD_W, HEAD_W), F32)
    new_ckv, new_kpe, new_dk, new_dv, new_st = [], [], [], [], []
    for l in range(depth):
        mod = mods[l]
        y = _inproj_call(x, mod, w_in_p[l], mod_idx, tm)

        hg_args = (lbp[l], hg_norm[l][None, :])
        oh_ctx, st_ctx = _hgrn_call(y, *hg_args, zero_state, tri, emat, seq, nb, 0)
        oh_lat, _ = _hgrn_call(y, *hg_args, state_hgrn[:, l], tri, emat, dseq, db, n_ctx // dseq)
        oh = jnp.concatenate([oh_ctx, oh_lat], axis=0)

        qcat, ckvn, kper, qd, kdr = _prep_call(y, mla_q_norm[l][None, :], mla_kv_norm[l][None, :], w_uq_p[l],
                                               cosm, sinm, cosd, sind, tm)
        vd = y[:, COL_DV:COL_DV + BRANCH_W]
        lat3 = lambda a: a[n_ctx:].reshape(db, dseq, a.shape[1])
        keys = lambda cache, a: jnp.concatenate(
            [jnp.concatenate([cache, lat3(a)], axis=1).reshape(db * tk_lat, a.shape[1]), a[:n_ctx]],
            axis=0).astype(BF16)
        kpe_cache = jnp.pad(cache_mla_kpe[:, l], ((0, 0), (0, 0), (0, LANES - MLA_ROPE)))
        kcat, vm = _kvup_call(keys(cache_mla_ckv[:, l], ckvn), keys(kpe_cache, kper), w_ukv_p[l], seq)
        kd_all = keys(cache_diff_k[:, l].reshape(db, past, BRANCH_W), kdr)
        vd_all = keys(cache_diff_v[:, l].reshape(db, past, BRANCH_W), vd)
        lam_row = jnp.full((1, LANES), lam[l], F32)
        attn_args = (qcat, qd, kcat, vm, kd_all, vd_all, lam_row, df_norm[l][None, :], float(1.0 - lam_init[l]))
        om_ctx, od_ctx = _attn_call(*attn_args, nseq=nb, tq=seq, q_tiles=1, q_block0=0, tk=seq,
                                    k_block0=db * tk_lat // seq, heads_per_step=HEADS)
        tq = 256
        om_lat, od_lat = _attn_call(*attn_args, nseq=db, tq=tq, q_tiles=dseq // tq, q_block0=n_ctx // tq,
                                    tk=tk_lat, k_block0=0, heads_per_step=1)
        om = jnp.concatenate([om_ctx, om_lat], axis=0)
        od = jnp.concatenate([od_ctx, od_lat], axis=0)

        merged = _merge_call(oh, om, od, y, w_branch_b[l], tm)
        x1, h2 = _outproj_call(merged, x, mod, w_out_b[l], ln1_g[l][None, :], ln1_b[l][None, :],
                               functools.partial(mod_idx, tile=256), alpha, 256)

        st = _peerq_call(h2, peer_wq_b[l], subkeys_b[l], tm)
        stats = _topk_call(st, tm)
        y2 = _peer_call(h2, st, stats, peer_u_b[l], peer_v_b[l], tm, 4)
        x = _ln2_call(x1, y2, mod, ln2_g[l][None, :], ln2_b[l][None, :], mod_idx, alpha, tm)

        new_ckv.append(ckvn[:n_ctx].reshape(nb, seq, MLA_KV_RANK))
        new_kpe.append(kper[:n_ctx, :MLA_ROPE].reshape(nb, seq, MLA_ROPE))
        new_dk.append(y[:n_ctx, COL_DK:COL_DK + BRANCH_W].reshape(nb, seq, HEADS, HEAD_W))
        new_dv.append(vd[:n_ctx].reshape(nb, seq, HEADS, HEAD_W))
        new_st.append(st_ctx)

    return (x[:n_ctx].reshape(nb, seq, d), x[n_ctx:].reshape(db, dseq, d),
            jnp.stack(new_ckv, axis=1), jnp.stack(new_kpe, axis=1), jnp.stack(new_dk, axis=1),
            jnp.stack(new_dv, axis=1), jnp.stack(new_st, axis=1))
```

```python
import functools
import math

import jax
import jax.numpy as jnp
import numpy as np
from jax import lax
from jax.experimental import pallas as pl
from jax.experimental.pallas import tpu as pltpu

F32 = jnp.float32
BF16 = jnp.bfloat16

GRID_W = 64
ROPE_BASE = 10000.0
HEADS = 8
HEAD_W = 128
HG_CHUNK = 32
MLA_Q_RANK = 512
MLA_KV_RANK = 256
MLA_NOPE = 128
MLA_ROPE = 64
MLA_QK_W = 256
DF_DH = 64
BRANCH_W = HEADS * HEAD_W
PEER_NKEYS = 128
PEER_TOPK = 16
N_MOD = 6
LANES = 128
VMEM_LIMIT = 52 * 1024 * 1024

COL_HQ, COL_FF, COL_FB, COL_HI, COL_HG = 0, 1024, 2048, 3072, 4096
COL_DQ, COL_DK, COL_DV = 5120, 6144, 7168
COL_GA = 8192
COL_CQ, COL_CKV, COL_KPE = 14336, 14848, 15104
IN_W_PAD = 15360

NT_DIMS = (((1,), (1,)), ((), ()))
TN_DIMS = (((0,), (0,)), ((), ()))


def _params(sem, vmem=VMEM_LIMIT):
    return pltpu.CompilerParams(dimension_semantics=sem, vmem_limit_bytes=vmem)


def _layer_norm(x, eps=1e-5):
    xc = x - jnp.mean(x, axis=-1, keepdims=True)
    return xc * lax.rsqrt(jnp.mean(xc * xc, axis=-1, keepdims=True) + eps)


def _rms_norm(x, g, eps=1e-6):
    return x * lax.rsqrt(jnp.mean(x * x, axis=-1, keepdims=True) + eps) * g


def _silu(x):
    return x * jax.nn.sigmoid(x)


def _gelu(x):
    return 0.5 * x * (1.0 + lax.erf(x * (2.0 ** -0.5)))


def _ada_kernel(c_ref, w_ref, b_ref, o_ref):
    s = _silu(c_ref[...]).astype(BF16)
    o_ref[0] = jnp.dot(s, w_ref[0].astype(BF16), preferred_element_type=F32) + b_ref[0]


def _ada_call(cvec, w_ada, b_ada):
    depth, d, w = w_ada.shape
    tn = 1024
    return pl.pallas_call(
        _ada_kernel,
        grid=(depth, w // tn),
        in_specs=[pl.BlockSpec((8, d), lambda l, j: (0, 0)),
                  pl.BlockSpec((1, d, tn), lambda l, j: (l, 0, j)),
                  pl.BlockSpec((1, 1, tn), lambda l, j: (l, 0, j))],
        out_specs=pl.BlockSpec((1, 8, tn), lambda l, j: (l, 0, j)),
        out_shape=jax.ShapeDtypeStruct((depth, 8, w), F32),
        compiler_params=_params(("parallel", "parallel")),
        name="ada",
    )(cvec, w_ada, b_ada.reshape(depth, 1, w))


def _inproj_kernel(x_ref, mod_ref, w_ref, o_ref, h_scr):
    @pl.when(pl.program_id(1) == 0)
    def _():
        shift, scale = mod_ref[0, 0:1, :], mod_ref[0, 1:2, :]
        h_scr[...] = (_layer_norm(x_ref[...]) * (1.0 + scale) + shift).astype(BF16)

    o_ref[...] = jnp.dot(h_scr[...], w_ref[...], preferred_element_type=F32)


def _inproj_call(x, mod, w, mod_idx, tm):
    n, d = x.shape
    wp = w.shape[1]
    tn = 1024
    return pl.pallas_call(
        _inproj_kernel,
        grid=(n // tm, wp // tn),
        in_specs=[pl.BlockSpec((tm, d), lambda i, j: (i, 0)),
                  pl.BlockSpec((1, N_MOD, d), lambda i, j: (mod_idx(i), 0, 0)),
                  pl.BlockSpec((d, tn), lambda i, j: (0, j))],
        out_specs=pl.BlockSpec((tm, tn), lambda i, j: (i, j)),
        out_shape=jax.ShapeDtypeStruct((n, wp), F32),
        scratch_shapes=[pltpu.VMEM((tm, d), BF16)],
        compiler_params=_params(("parallel", "arbitrary")),
        name="inproj",
    )(x, mod, w)


def _hgrn_chunk_pre(direction, q, z, v, lbp, tri):
    c = HG_CHUNK
    loglb, log1mlb, omlb = lbp[0:1, :], lbp[1:2, :], lbp[2:3, :]
    log_f = jnp.logaddexp(loglb, log1mlb + jax.nn.log_sigmoid(z))
    k = omlb * jax.nn.sigmoid(-z)
    q = _silu(q)

    hi = log_f.astype(BF16)
    r1 = log_f - hi.astype(F32)
    mid = r1.astype(BF16)
    lo = (r1 - mid.astype(F32)).astype(BF16)
    cs = jnp.dot(tri, jnp.concatenate([hi, mid, lo], axis=1), preferred_element_type=F32)
    b = cs[:, 0:LANES] + cs[:, LANES:2 * LANES] + cs[:, 2 * LANES:3 * LANES]
    total = b[c - 1:c, :] if direction == 0 else b[0:1, :]

    qe = (q * jnp.exp(b)).astype(BF16)
    kd = (k * jnp.exp(total - b)).astype(BF16)
    vb = v.astype(BF16)

    row = lax.broadcasted_iota(jnp.int32, (8, LANES), 0)
    cols = []
    for s in range(c):
        pieces = []
        for j in range(c // 8):
            lo_t, hi_t = 8 * j, 8 * j + 7
            if direction == 0:
                dead, full = hi_t < s, lo_t >= s
            else:
                dead, full = lo_t > s, hi_t <= s
            if dead:
                pieces.append(jnp.zeros((8, LANES), F32))
                continue
            val = q[lo_t:lo_t + 8, :] * jnp.exp(b[lo_t:lo_t + 8, :] - b[s:s + 1, :]) * k[s:s + 1, :]
            if not full:
                keep = (row + lo_t >= s) if direction == 0 else (row + lo_t <= s)
                val = jnp.where(keep, val, 0.0)
            pieces.append(val)
        cols.append(jnp.concatenate(pieces, axis=0))
    pcat = jnp.concatenate(cols, axis=1).astype(BF16)
    return qe, kd, vb, total, pcat


def _hgrn_chunk_post(qe, kd, vb, total, att, st):
    o = lax.dot_general(qe, st.astype(BF16), NT_DIMS, preferred_element_type=F32)
    o = o + jnp.dot(att.astype(BF16), vb, preferred_element_type=F32)
    st = st * jnp.exp(total) + lax.dot_general(vb, kd, TN_DIMS, preferred_element_type=F32)
    return o, st


def _hgrn_kernel(qf_ref, ff_ref, vf_ref, qb_ref, fb_ref, vb_ref, lbp_ref, s0_ref, tri_ref, e_ref, *rest,
                 tile, heads, aliased):
    of_ref, ob_ref, sout_ref, st_scr = rest[2:] if aliased else rest
    i = pl.program_id(2)
    c = HG_CHUNK
    n = tile // c

    @pl.when(i == 0)
    def _():
        for hh in range(heads):
            for direction in range(2):
                st_scr[hh, direction] = s0_ref[0, direction, hh].T

    def body(j, carry):
        chains = []
        for hh in range(heads):
            cs = slice(hh * HEAD_W, (hh + 1) * HEAD_W)
            for direction, q_ref, f_ref, v_ref, o_ref in ((0, qf_ref, ff_ref, vf_ref, of_ref),
                                                          (1, qb_ref, fb_ref, vb_ref, ob_ref)):
                r0 = pl.multiple_of((j if direction == 0 else n - 1 - j) * c, c)
                rows = pl.ds(r0, c)
                pre = _hgrn_chunk_pre(direction, q_ref[rows, cs], f_ref[rows, cs], v_ref[rows, cs],
                                      lbp_ref[direction, :, cs], tri_ref[direction])
                chains.append((hh, direction, o_ref, rows, cs, pre))
        att = jnp.dot(jnp.concatenate([ch[5][4] for ch in chains], axis=0), e_ref[...],
                      preferred_element_type=F32)
        for idx, (hh, direction, o_ref, rows, cs, pre) in enumerate(chains):
            o, st = _hgrn_chunk_post(*pre[:4], att[idx * c:(idx + 1) * c, :], st_scr[hh, direction])
            o_ref[rows, cs] = o
            st_scr[hh, direction] = st
        return carry

    lax.fori_loop(0, n, body, 0)

    @pl.when(i == pl.num_programs(2) - 1)
    def _():
        for hh in range(heads):
            for direction in range(2):
                sout_ref[0, direction, hh] = st_scr[hh, direction].T


def _hgrn_call(y, lbp, s0, tri, emat, *, seq, nseq, row0, tile, heads, prev=None):
    n = y.shape[0]
    nt = seq // tile
    w = heads * HEAD_W
    blk0 = row0 // tile

    def col(c0, rev):
        def index(b, g, i):
            t = nt - 1 - i if rev else i
            return (blk0 + b * nt + t, c0 // w + g)
        return pl.BlockSpec((tile, w), index)

    state_spec = pl.BlockSpec((1, 2, heads, HEAD_W, HEAD_W), lambda b, g, i: (b, 0, g, 0, 0))
    out_f = pl.BlockSpec((tile, w), lambda b, g, i: (blk0 + b * nt + i, g))
    out_b = pl.BlockSpec((tile, w), lambda b, g, i: (blk0 + b * nt + nt - 1 - i, g))
    in_specs = [col(COL_HQ, False), col(COL_FF, False), col(COL_HI, False),
                col(COL_HQ, True), col(COL_FB, True), col(COL_HI, True),
                pl.BlockSpec((2, 3, w), lambda b, g, i: (0, 0, g)),
                state_spec,
                pl.BlockSpec((2, HG_CHUNK, HG_CHUNK), lambda b, g, i: (0, 0, 0)),
                pl.BlockSpec((HG_CHUNK * HEAD_W, HG_CHUNK), lambda b, g, i: (0, 0))]
    args = [y, y, y, y, y, y, lbp, s0, tri, emat]
    aliases = {}
    if prev is not None:
        in_specs += [pl.BlockSpec(memory_space=pl.ANY), pl.BlockSpec(memory_space=pl.ANY)]
        aliases = {len(args): 0, len(args) + 1: 1}
        args += list(prev)
    return pl.pallas_call(
        functools.partial(_hgrn_kernel, tile=tile, heads=heads, aliased=prev is not None),
        grid=(nseq, HEADS // heads, nt),
        in_specs=in_specs,
        out_specs=[out_f, out_b, state_spec],
        out_shape=[jax.ShapeDtypeStruct((n, BRANCH_W), F32), jax.ShapeDtypeStruct((n, BRANCH_W), F32),
                   jax.ShapeDtypeStruct((nseq, 2, HEADS, HEAD_W, HEAD_W), F32)],
        scratch_shapes=[pltpu.VMEM((heads, 2, HEAD_W, HEAD_W), F32)],
        input_output_aliases=aliases,
        compiler_params=_params(("parallel", "parallel", "arbitrary")),
        name="hgrn",
    )(*args)


def _hfin_kernel(of_ref, ob_ref, g_ref, gn_ref, o_ref):
    for h in range(HEADS):
        cs = slice(h * HEAD_W, (h + 1) * HEAD_W)
        o = _rms_norm(of_ref[:, cs] + ob_ref[:, cs], gn_ref[:, cs])
        o_ref[:, cs] = (o * _silu(g_ref[:, cs])).astype(BF16)


def _hfin_call(of, ob, y, gnorm, tm):
    n = of.shape[0]
    row = pl.BlockSpec((tm, BRANCH_W), lambda i: (i, 0))
    return pl.pallas_call(
        _hfin_kernel,
        grid=(n // tm,),
        in_specs=[row, row, pl.BlockSpec((tm, BRANCH_W), lambda i: (i, COL_HG // BRANCH_W)),
                  pl.BlockSpec((1, BRANCH_W), lambda i: (0, 0))],
        out_specs=row,
        out_shape=jax.ShapeDtypeStruct((n, BRANCH_W), BF16),
        compiler_params=_params(("parallel",)),
        name="hfin",
    )(of, ob, y, gnorm)


def _rope(x, cos, sin_signed, lo_half):
    rot = jnp.where(lo_half, pltpu.roll(x, LANES - 16, 1), pltpu.roll(x, 16, 1))
    return x * cos + rot * sin_signed


def _prep_kernel(cq_ref, ckv_ref, kpe_ref, dq_ref, dk_ref, qn_ref, kvn_ref, wuq_ref,
                 cosm_ref, sinm_ref, cosd_ref, sind_ref,
                 qcat_ref, ckvn_ref, kper_ref, qd_ref, kd_ref):
    lo_half = (lax.broadcasted_iota(jnp.int32, (1, LANES), 1) % 32) < 16
    cq = _rms_norm(cq_ref[...], qn_ref[...]).astype(BF16)
    qm = jnp.dot(cq, wuq_ref[...], preferred_element_type=F32)
    cos_pe, sin_pe = cosm_ref[...], sinm_ref[...]
    for h in range(HEADS):
        c0 = h * MLA_QK_W
        qcat_ref[:, c0:c0 + LANES] = qm[:, c0:c0 + LANES].astype(BF16)
        pe = _rope(qm[:, c0 + LANES:c0 + 2 * LANES], cos_pe, sin_pe, lo_half)
        qcat_ref[:, c0 + LANES:c0 + 2 * LANES] = pe.astype(BF16)
    ckvn_ref[...] = _rms_norm(ckv_ref[...], kvn_ref[...])
    kper_ref[...] = _rope(kpe_ref[...], cos_pe, sin_pe, lo_half)
    cos_d, sin_d = cosd_ref[...], sind_ref[...]
    for h in range(HEADS):
        cs = slice(h * LANES, (h + 1) * LANES)
        qd_ref[:, cs] = _rope(dq_ref[:, cs], cos_d, sin_d, lo_half).astype(BF16)
        kd_ref[:, cs] = _rope(dk_ref[:, cs], cos_d, sin_d, lo_half)


def _prep_call(y, qn, kvn, wuq, cosm, sinm, cosd, sind, tm):
    n = y.shape[0]
    row = lambda w, c0: pl.BlockSpec((tm, w), lambda i, c0=c0, w=w: (i, c0 // w))
    full = lambda a: pl.BlockSpec(a.shape, lambda i: (0, 0))
    tab = pl.BlockSpec((tm, LANES), lambda i: (i, 0))
    out = lambda w: pl.BlockSpec((tm, w), lambda i: (i, 0))
    return pl.pallas_call(
        _prep_kernel,
        grid=(n // tm,),
        in_specs=[row(MLA_Q_RANK, COL_CQ), row(MLA_KV_RANK, COL_CKV), row(LANES, COL_KPE),
                  row(BRANCH_W, COL_DQ), row(BRANCH_W, COL_DK), full(qn), full(kvn), full(wuq),
                  tab, tab, tab, tab],
        out_specs=[out(HEADS * MLA_QK_W), out(MLA_KV_RANK), out(LANES), out(BRANCH_W), out(BRANCH_W)],
        out_shape=[jax.ShapeDtypeStruct((n, HEADS * MLA_QK_W), BF16),
                   jax.ShapeDtypeStruct((n, MLA_KV_RANK), F32),
                   jax.ShapeDtypeStruct((n, LANES), F32),
                   jax.ShapeDtypeStruct((n, BRANCH_W), BF16),
                   jax.ShapeDtypeStruct((n, BRANCH_W), F32)],
        compiler_params=_params(("parallel",)),
        name="prep",
    )(y, y, y, y, y, qn, kvn, wuq, cosm, sinm, cosd, sind)


def _kvup_kernel(ckv_ref, kpe_ref, wk_ref, wvt_ref, kcat_ref, vt_ref):
    ckv = ckv_ref[...]
    kn = jnp.dot(ckv, wk_ref[...], preferred_element_type=F32)
    kpe = kpe_ref[...]
    for h in range(HEADS):
        c0 = h * MLA_QK_W
        kcat_ref[:, c0:c0 + LANES] = kn[:, h * LANES:(h + 1) * LANES].astype(BF16)
        kcat_ref[:, c0 + LANES:c0 + 2 * LANES] = kpe
    vt_ref[...] = lax.dot_general(wvt_ref[...], ckv, NT_DIMS, preferred_element_type=F32).astype(BF16)


def _kvup_call(ckv_all, kpe_all, wk, wvt, tk):
    r = ckv_all.shape[0]
    return pl.pallas_call(
        _kvup_kernel,
        grid=(r // tk,),
        in_specs=[pl.BlockSpec((tk, MLA_KV_RANK), lambda i: (i, 0)),
                  pl.BlockSpec((tk, LANES), lambda i: (i, 0)),
                  pl.BlockSpec(wk.shape, lambda i: (0, 0)),
                  pl.BlockSpec(wvt.shape, lambda i: (0, 0))],
        out_specs=[pl.BlockSpec((tk, HEADS * MLA_QK_W), lambda i: (i, 0)),
                   pl.BlockSpec((BRANCH_W, tk), lambda i: (0, i))],
        out_shape=[jax.ShapeDtypeStruct((r, HEADS * MLA_QK_W), BF16),
                   jax.ShapeDtypeStruct((BRANCH_W, r), BF16)],
        compiler_params=_params(("parallel",)),
        name="kvup",
    )(ckv_all, kpe_all, wk, wvt)


def _exp_cols(s, scale):
    e = jnp.exp((s - jnp.max(s, axis=0, keepdims=True)) * scale)
    return e, 1.0 / jnp.sum(e, axis=0, keepdims=True)


def _attn_kernel(qm_ref, km_ref, vmt_ref, qd_ref, kd_ref, vdt_ref, lam_ref, gn_ref, *rest,
                 heads, out_scale, aliased):
    om_ref, od_ref = rest[2:] if aliased else rest
    lam = lam_ref[:, 0:1]
    gn = gn_ref[...]
    lane = lax.broadcasted_iota(jnp.int32, (1, LANES), 1)
    for h in range(heads):
        qk = slice(h * MLA_QK_W, (h + 1) * MLA_QK_W)
        hw = slice(h * HEAD_W, (h + 1) * HEAD_W)
        s = lax.dot_general(km_ref[:, qk], qm_ref[:, qk], NT_DIMS, preferred_element_type=F32)
        e, r = _exp_cols(s, (MLA_NOPE + MLA_ROPE) ** -0.5)
        o_t = jnp.dot(vmt_ref[hw, :], (e * r).astype(BF16), preferred_element_type=F32)
        om_ref[:, hw] = o_t.T.astype(BF16)

        q = qd_ref[:, hw]
        zero = jnp.zeros_like(q)
        k = kd_ref[:, hw]
        s1 = lax.dot_general(k, jnp.where(lane < DF_DH, q, zero), NT_DIMS, preferred_element_type=F32)
        s2 = lax.dot_general(k, jnp.where(lane >= DF_DH, q, zero), NT_DIMS, preferred_element_type=F32)
        e1, r1 = _exp_cols(s1, DF_DH ** -0.5)
        e2, r2 = _exp_cols(s2, DF_DH ** -0.5)
        w = (e1 * r1 - e2 * (lam * r2)).astype(BF16)
        o_t = jnp.dot(vdt_ref[hw, :], w, preferred_element_type=F32)
        od_ref[:, hw] = (_rms_norm(o_t.T, gn) * out_scale).astype(BF16)


def _attn_call(qcat, qd, kcat, vmt, kd, vdt, lam, gn, out_scale, *, nseq, tq, q_tiles, q_row0, tk,
               k_block0, heads, prev=None):
    n = qcat.shape[0]
    qb0 = q_row0 // tq
    qmap = lambda b, g, i: (qb0 + b * q_tiles + i, g)
    kmap = lambda b, g, i: (k_block0 + b, g)
    vmap = lambda b, g, i: (g, k_block0 + b)
    in_specs = [pl.BlockSpec((tq, heads * MLA_QK_W), qmap),
                pl.BlockSpec((tk, heads * MLA_QK_W), kmap),
                pl.BlockSpec((heads * HEAD_W, tk), vmap),
                pl.BlockSpec((tq, heads * HEAD_W), qmap),
                pl.BlockSpec((tk, heads * HEAD_W), kmap),
                pl.BlockSpec((heads * HEAD_W, tk), vmap),
                pl.BlockSpec((1, LANES), lambda b, g, i: (0, 0)),
                pl.BlockSpec((1, LANES), lambda b, g, i: (0, 0))]
    args = [qcat, kcat, vmt, qd, kd, vdt, lam, gn]
    aliases = {}
    if prev is not None:
        in_specs += [pl.BlockSpec(memory_space=pl.ANY), pl.BlockSpec(memory_space=pl.ANY)]
        aliases = {len(args): 0, len(args) + 1: 1}
        args += list(prev)
    out_spec = pl.BlockSpec((tq, heads * HEAD_W), qmap)
    return pl.pallas_call(
        functools.partial(_attn_kernel, heads=heads, out_scale=out_scale, aliased=prev is not None),
        grid=(nseq, HEADS // heads, q_tiles),
        in_specs=in_specs,
        out_specs=[out_spec, out_spec],
        out_shape=[jax.ShapeDtypeStruct((n, BRANCH_W), BF16), jax.ShapeDtypeStruct((n, BRANCH_W), BF16)],
        input_output_aliases=aliases,
        compiler_params=_params(("parallel", "parallel", "arbitrary")),
        name="attn",
    )(*args)


def _merge_kernel(oh_ref, om_ref, od_ref, ga_ref, gb_ref, gc_ref, w_ref, o_ref):
    acc = jax.nn.sigmoid(ga_ref[...]) * jnp.dot(oh_ref[...], w_ref[0], preferred_element_type=F32)
    acc += jax.nn.sigmoid(gb_ref[...]) * jnp.dot(om_ref[...], w_ref[1], preferred_element_type=F32)
    acc += jax.nn.sigmoid(gc_ref[...]) * jnp.dot(od_ref[...], w_ref[2], preferred_element_type=F32)
    o_ref[...] = acc.astype(BF16)


def _merge_call(oh, om, od, y, wbr, tm):
    n = oh.shape[0]
    d = wbr.shape[2]
    tn = 1024
    br = pl.BlockSpec((tm, BRANCH_W), lambda j, i: (i, 0))
    gate = lambda g: pl.BlockSpec((tm, tn), lambda j, i, g=g: (i, (COL_GA + g * d) // tn + j))
    return pl.pallas_call(
        _merge_kernel,
        grid=(d // tn, n // tm),
        in_specs=[br, br, br, gate(0), gate(1), gate(2),
                  pl.BlockSpec((3, BRANCH_W, tn), lambda j, i: (0, 0, j))],
        out_specs=pl.BlockSpec((tm, tn), lambda j, i: (i, j)),
        out_shape=jax.ShapeDtypeStruct((n, d), BF16),
        compiler_params=_params(("parallel", "parallel")),
        name="merge",
    )(oh, om, od, y, y, y, wbr)


def _outproj_kernel(m_ref, x_ref, mod_ref, w_ref, g_ref, b_ref, x1_ref, h2_ref, *, alpha):
    gate1 = mod_ref[0, 2:3, :]
    shift2, scale2 = mod_ref[0, 3:4, :], mod_ref[0, 4:5, :]
    y = jnp.dot(m_ref[...], w_ref[...], preferred_element_type=F32)
    x1 = _layer_norm(alpha * x_ref[...] + gate1 * y) * g_ref[...] + b_ref[...]
    x1_ref[...] = x1
    h2_ref[...] = (_layer_norm(x1) * (1.0 + scale2) + shift2).astype(BF16)


def _outproj_call(merged, x, mod, w, g, b, mod_idx, alpha, tm):
    n, d = x.shape
    row = pl.BlockSpec((tm, d), lambda i: (i, 0))
    vec = pl.BlockSpec((1, d), lambda i: (0, 0))
    return pl.pallas_call(
        functools.partial(_outproj_kernel, alpha=alpha),
        grid=(n // tm,),
        in_specs=[row, row, pl.BlockSpec((1, N_MOD, d), lambda i: (mod_idx(i), 0, 0)),
                  pl.BlockSpec((d, d), lambda i: (0, 0)), vec, vec],
        out_specs=[row, row],
        out_shape=[jax.ShapeDtypeStruct((n, d), F32), jax.ShapeDtypeStruct((n, d), BF16)],
        compiler_params=_params(("parallel",)),
        name="outproj",
    )(merged, x, mod, w, g, b)


def _peerq_kernel(h_ref, w_ref, sk_ref, st_ref):
    q = jnp.dot(h_ref[...], w_ref[...], preferred_element_type=F32).astype(BF16)
    for hp in range(2 * HEADS):
        st_ref[hp // 2, hp % 2] = lax.dot_general(sk_ref[hp % 2], q[:, hp * LANES:(hp + 1) * LANES], NT_DIMS,
                                                  preferred_element_type=F32)


def _peerq_call(h2, wq, subkeys, tm):
    n, d = h2.shape
    return pl.pallas_call(
        _peerq_kernel,
        grid=(n // tm,),
        in_specs=[pl.BlockSpec((tm, d), lambda i: (i, 0)),
                  pl.BlockSpec(wq.shape, lambda i: (0, 0)),
                  pl.BlockSpec(subkeys.shape, lambda i: (0, 0, 0))],
        out_specs=pl.BlockSpec((HEADS, 2, PEER_NKEYS, tm), lambda i: (0, 0, 0, i)),
        out_shape=jax.ShapeDtypeStruct((HEADS, 2, PEER_NKEYS, n), F32),
        compiler_params=_params(("parallel",)),
        name="peerq",
    )(h2, wq, subkeys)


def _top_values(x, count):
    rows = lax.broadcasted_iota(jnp.int32, x.shape, 0).astype(F32)
    vals = []
    for r in range(count):
        m = jnp.max(x, axis=0, keepdims=True)
        vals.append(m)
        if r + 1 < count:
            first = jnp.min(jnp.where(x == m, rows, float(x.shape[0])), axis=0, keepdims=True)
            x = jnp.where(rows == first, -jnp.inf, x)
    return vals


def _topk_kernel(st_ref, e1_ref, e2_ref, th_ref, v_scr, cand_scr):
    row8 = lax.broadcasted_iota(jnp.int32, (8, st_ref.shape[3]), 0)
    for h in range(HEADS):
        s1, s2 = st_ref[h, 0], st_ref[h, 1]
        v1 = _top_values(s1, PEER_TOPK)
        v2 = _top_values(s2, PEER_TOPK)
        for r in range(PEER_TOPK):
            v_scr[0, r:r + 1, :] = v1[r]
            v_scr[1, r:r + 1, :] = v2[r]
        cand_scr[0:16, :] = v1[0] + v_scr[1]
        for a in range(1, 8):
            pair = v1[a] + v_scr[1, 0:8, :]
            cand_scr[8 + 8 * a:16 + 8 * a, :] = jnp.where(row8 < PEER_TOPK // (a + 1), pair, -jnp.inf)
        cand_scr[72:80, :] = v_scr[0, 8:16, :] + v2[0]
        top = _top_values(cand_scr[...], PEER_TOPK)
        z = jnp.exp(top[0] - top[0])
        for r in range(1, PEER_TOPK):
            z = z + jnp.exp(top[r] - top[0])
        e1_ref[h] = jnp.exp(s1 - v1[0]) * (1.0 / z)
        e2_ref[h] = jnp.exp(s2 - v2[0])
        th = jnp.full(s1.shape, jnp.inf, F32)
        for b in range(PEER_TOPK):
            th = jnp.where(s1 + v2[b] >= top[PEER_TOPK - 1], v2[b], th)
        th_ref[h] = th


def _topk_call(st, tm):
    n = st.shape[3]
    out = pl.BlockSpec((HEADS, PEER_NKEYS, tm), lambda i: (0, 0, i))
    shape = jax.ShapeDtypeStruct((HEADS, PEER_NKEYS, n), F32)
    return pl.pallas_call(
        _topk_kernel,
        grid=(n // tm,),
        in_specs=[pl.BlockSpec((HEADS, 2, PEER_NKEYS, tm), lambda i: (0, 0, 0, i))],
        out_specs=[out, out, out],
        out_shape=[shape, shape, shape],
        scratch_shapes=[pltpu.VMEM((2, PEER_TOPK, tm), F32), pltpu.VMEM((80, tm), F32)],
        compiler_params=_params(("parallel",)),
        name="topk",
    )(st)


def _peer_kernel(ht_ref, s2_ref, e1_ref, e2_ref, th_ref, u_ref, vt_ref, ot_ref, ga_scr, *, rows_per_step):
    j = pl.program_id(1)
    nj = pl.num_programs(1) - 1
    slot = j % 2
    tm = ht_ref.shape[1]

    @pl.when(j == 0)
    def _():
        ot_ref[...] = jnp.zeros_like(ot_ref)
        ga_scr[1] = jnp.zeros(ga_scr.shape[1:], BF16)

    ot_ref[...] += jnp.dot(vt_ref[...], ga_scr[1 - slot], preferred_element_type=F32)

    a_t = jnp.dot(u_ref[...], ht_ref[...], preferred_element_type=F32)
    i0 = jnp.minimum(j, nj - 1) * rows_per_step
    for r in range(rows_per_step):
        i = i0 + r
        g = jnp.zeros((PEER_NKEYS, tm), F32)
        for h in range(HEADS):
            keep = s2_ref[h] >= th_ref[h, pl.ds(i, 1), :]
            g = g + jnp.where(keep, e2_ref[h], 0.0) * e1_ref[h, pl.ds(i, 1), :]
        rows = slice(r * PEER_NKEYS, (r + 1) * PEER_NKEYS)
        ga_scr[slot, rows, :] = (g * _gelu(a_t[rows, :])).astype(BF16)


def _peer_call(h2t, st, e1, e2, th, u, vt, tm, rows_per_step):
    d, n = h2t.shape
    te = rows_per_step * PEER_NKEYS
    nj = u.shape[0] // te
    once = pl.Buffered(1)
    tok = pl.BlockSpec((HEADS, PEER_NKEYS, tm), lambda i, j: (0, 0, i), pipeline_mode=once)
    return pl.pallas_call(
        functools.partial(_peer_kernel, rows_per_step=rows_per_step),
        grid=(n // tm, nj + 1),
        in_specs=[pl.BlockSpec((d, tm), lambda i, j: (0, i), pipeline_mode=once),
                  pl.BlockSpec((HEADS, None, PEER_NKEYS, tm), lambda i, j: (0, 1, 0, i), pipeline_mode=once),
                  tok, tok, tok,
                  pl.BlockSpec((te, d), lambda i, j: (jnp.minimum(j, nj - 1), 0)),
                  pl.BlockSpec((d, te), lambda i, j: (0, jnp.maximum(j - 1, 0)))],
        out_specs=pl.BlockSpec((d, tm), lambda i, j: (0, i)),
        out_shape=jax.ShapeDtypeStruct((d, n), F32),
        scratch_shapes=[pltpu.VMEM((2, te, tm), BF16)],
        compiler_params=_params(("parallel", "arbitrary")),
        name="peer",
    )(h2t, st, e1, e2, th, u, vt)


def _ln2_kernel(x_ref, yt_ref, mod_ref, g_ref, b_ref, o_ref, *, alpha):
    gate2 = mod_ref[0, 5:6, :]
    o_ref[...] = _layer_norm(alpha * x_ref[...] + gate2 * yt_ref[...].T) * g_ref[...] + b_ref[...]


def _ln2_call(x1, y2t, mod, g, b, mod_idx, alpha, tm):
    n, d = x1.shape
    row = pl.BlockSpec((tm, d), lambda i: (i, 0))
    vec = pl.BlockSpec((1, d), lambda i: (0, 0))
    return pl.pallas_call(
        functools.partial(_ln2_kernel, alpha=alpha),
        grid=(n // tm,),
        in_specs=[row, pl.BlockSpec((d, tm), lambda i: (0, i)),
                  pl.BlockSpec((1, N_MOD, d), lambda i: (mod_idx(i), 0, 0)), vec, vec],
        out_specs=row,
        out_shape=jax.ShapeDtypeStruct((n, d), F32),
        compiler_params=_params(("parallel",)),
        name="ln2",
    )(x1, y2t, mod, g, b)


def _rope_tables(n_ctx, n_lat_seq, lat_seq):
    quarter = 16
    t = jnp.arange(lat_seq)
    rowp = (t // GRID_W).astype(F32)
    colp = (t % GRID_W).astype(F32)
    inv_freq = ROPE_BASE ** (-jnp.arange(quarter, dtype=F32) / quarter)
    ang = jnp.stack([rowp[:, None] * inv_freq, colp[:, None] * inv_freq], axis=1)
    ang = jnp.concatenate([ang, ang], axis=-1).reshape(lat_seq, 64)
    sign = jnp.where((jnp.arange(64) % 32) < 16, -1.0, 1.0).astype(F32)
    cos, sin = jnp.cos(ang), jnp.sin(ang) * sign
    ones, zeros = jnp.ones((lat_seq, 64), F32), jnp.zeros((lat_seq, 64), F32)

    def full(tab, ident):
        lat = jnp.tile(tab, (n_lat_seq, 1))
        return jnp.concatenate([jnp.full((n_ctx, LANES), ident, F32), lat], axis=0)

    cosd = full(jnp.concatenate([cos, cos], axis=1), 1.0)
    sind = full(jnp.concatenate([sin, sin], axis=1), 0.0)
    cosm = full(jnp.concatenate([cos, ones], axis=1), 1.0)
    sinm = full(jnp.concatenate([sin, zeros], axis=1), 0.0)
    return cosd, sind, cosm, sinm


def kernel(x_prompt, x_sample, cache_mla_ckv, cache_mla_kpe, cache_diff_k, cache_diff_v, state_hgrn,
           c, c_ctx, w_ada, b_ada, w_in, hg_lb_logits, hg_norm, mla_q_norm, w_uq, mla_kv_norm, w_ukv,
           df_lambda, df_norm, w_branch, w_out, ln1_g, ln1_b, ln2_g, ln2_b,
           peer_wq, peer_subkeys, peer_u, peer_v):
    nb, seq, d = x_prompt.shape
    db, dseq, _ = x_sample.shape
    depth = w_in.shape[0]
    past = cache_mla_ckv.shape[2]
    n_ctx, n_lat = nb * seq, db * dseq
    n = n_ctx + n_lat
    tk_lat = past + dseq
    tm = 512
    tq = 256
    hg_tile = min(dseq, 512)
    assert n_ctx % tm == 0 and dseq % tm == 0 and (db * tk_lat) % seq == 0 and seq % tq == 0
    alpha = (2 * depth) ** 0.25

    def mod_idx(i, tile=tm):
        return jnp.where(i < n_ctx // tile, 0, 1 + (i - n_ctx // tile) // (dseq // tile))

    lb = jnp.cumsum(jax.nn.softmax(hg_lb_logits.astype(F32), axis=0), axis=0)
    lb = lb - lb[0:1]
    lbp = jnp.stack([jnp.log(lb), jnp.log1p(-lb), 1.0 - lb], axis=2)
    lq = df_lambda.astype(F32)
    lam_init = np.array([0.8 - 0.6 * math.exp(-0.3 * l) for l in range(depth)], np.float32)
    lam = jnp.exp(jnp.sum(lq[:, 0] * lq[:, 1], axis=-1)) - jnp.exp(jnp.sum(lq[:, 2] * lq[:, 3], axis=-1)) + lam_init

    kpe0 = 5 * 1024 + MLA_Q_RANK + MLA_KV_RANK
    w_in_p = jnp.concatenate(
        [w_in[:, :, :5 * 1024], w_in[:, :, kpe0 + MLA_ROPE:], w_in[:, :, 5 * 1024:kpe0],
         w_in[:, :, kpe0:kpe0 + MLA_ROPE],
         jnp.zeros((depth, d, IN_W_PAD - w_in.shape[2]), w_in.dtype)], axis=2).astype(BF16)
    wq4 = w_uq.reshape(depth, MLA_Q_RANK, HEADS, MLA_NOPE + MLA_ROPE)
    w_uq_p = jnp.concatenate(
        [wq4, jnp.zeros((depth, MLA_Q_RANK, HEADS, MLA_QK_W - MLA_NOPE - MLA_ROPE), w_uq.dtype)],
        axis=3).reshape(depth, MLA_Q_RANK, HEADS * MLA_QK_W).astype(BF16)
    wkv4 = w_ukv.reshape(depth, MLA_KV_RANK, HEADS, 2 * HEAD_W)
    w_uk = wkv4[..., :HEAD_W].reshape(depth, MLA_KV_RANK, BRANCH_W).astype(BF16)
    w_uvt = jnp.swapaxes(wkv4[..., HEAD_W:].reshape(depth, MLA_KV_RANK, BRANCH_W), 1, 2).astype(BF16)
    w_branch_b, w_out_b, peer_wq_b = w_branch.astype(BF16), w_out.astype(BF16), peer_wq.astype(BF16)
    subkeys_b, peer_u_b = peer_subkeys.astype(BF16), peer_u.astype(BF16)
    peer_vt_b = jnp.swapaxes(peer_v.astype(BF16), 1, 2)

    tri = jnp.stack([jnp.tril(jnp.ones((HG_CHUNK, HG_CHUNK), F32)),
                     jnp.triu(jnp.ones((HG_CHUNK, HG_CHUNK), F32))]).astype(BF16)
    emat = jnp.repeat(jnp.eye(HG_CHUNK, dtype=F32), HEAD_W, axis=0).astype(BF16)
    cosd, sind, cosm, sinm = _rope_tables(n_ctx, db, dseq)

    cvec = jnp.concatenate([c_ctx[None, :], c, jnp.zeros((8 - 1 - db, d), F32)], axis=0)
    mods = _ada_call(cvec, w_ada, b_ada)[:, :1 + db].reshape(depth, 1 + db, N_MOD, d)

    x = jnp.concatenate([x_prompt.reshape(n_ctx, d), x_sample.reshape(n_lat, d)], axis=0)
    zero_state = jnp.zeros((nb, 2, HEADS, HEAD_W, HEAD_W), F32)
    new_ckv, new_kpe, new_dk, new_dv, new_st = [], [], [], [], []
    for l in range(depth):
        mod = mods[l]
        y = _inproj_call(x, mod, w_in_p[l], mod_idx, tm)

        of, ob, st_ctx = _hgrn_call(y, lbp[l], zero_state, tri, emat, seq=seq, nseq=nb, row0=0,
                                    tile=seq, heads=4)
        of, ob, _ = _hgrn_call(y, lbp[l], state_hgrn[:, l], tri, emat, seq=dseq, nseq=db, row0=n_ctx,
                               tile=hg_tile, heads=4, prev=(of, ob))
        oh = _hfin_call(of, ob, y, hg_norm[l][None, :], tm)

        qcat, ckvn, kper, qd, kdr = _prep_call(y, mla_q_norm[l][None, :], mla_kv_norm[l][None, :], w_uq_p[l],
                                               cosm, sinm, cosd, sind, tm)
        vd = y[:, COL_DV:COL_DV + BRANCH_W]
        lat3 = lambda a: a[n_ctx:].reshape(db, dseq, a.shape[1])
        keys = lambda cache, a: jnp.concatenate(
            [jnp.concatenate([cache, lat3(a)], axis=1).reshape(db * tk_lat, a.shape[1]), a[:n_ctx]],
            axis=0).astype(BF16)
        kpe_cache = jnp.pad(cache_mla_kpe[:, l], ((0, 0), (0, 0), (0, LANES - MLA_ROPE)))
        kcat, vmt = _kvup_call(keys(cache_mla_ckv[:, l], ckvn), keys(kpe_cache, kper), w_uk[l], w_uvt[l], seq)
        kd_all = keys(cache_diff_k[:, l].reshape(db, past, BRANCH_W), kdr)
        vdt = keys(cache_diff_v[:, l].reshape(db, past, BRANCH_W), vd).T
        lam_row = jnp.full((1, LANES), lam[l], F32)
        attn_args = (qcat, qd, kcat, vmt, kd_all, vdt, lam_row, df_norm[l][None, :], float(1.0 - lam_init[l]))
        om, od = _attn_call(*attn_args, nseq=nb, tq=seq, q_tiles=1, q_row0=0, tk=seq,
                            k_block0=db * tk_lat // seq, heads=HEADS)
        om, od = _attn_call(*attn_args, nseq=db, tq=tq, q_tiles=dseq // tq, q_row0=n_ctx,
                            tk=tk_lat, k_block0=0, heads=1, prev=(om, od))

        merged = _merge_call(oh, om, od, y, w_branch_b[l], tm)
        x1, h2 = _outproj_call(merged, x, mod, w_out_b[l], ln1_g[l][None, :], ln1_b[l][None, :],
                               functools.partial(mod_idx, tile=256), alpha, 256)

        st = _peerq_call(h2, peer_wq_b[l], subkeys_b[l], tm)
        e1, e2, th = _topk_call(st, tm)
        y2t = _peer_call(h2.T, st, e1, e2, th, peer_u_b[l], peer_vt_b[l], tm, 8)
        x = _ln2_call(x1, y2t, mod, ln2_g[l][None, :], ln2_b[l][None, :], mod_idx, alpha, tm)

        new_ckv.append(ckvn[:n_ctx].reshape(nb, seq, MLA_KV_RANK))
        new_kpe.append(kper[:n_ctx, :MLA_ROPE].reshape(nb, seq, MLA_ROPE))
        new_dk.append(y[:n_ctx, COL_DK:COL_DK + BRANCH_W].reshape(nb, seq, HEADS, HEAD_W))
        new_dv.append(vd[:n_ctx].reshape(nb, seq, HEADS, HEAD_W))
        new_st.append(st_ctx)

    return (x[:n_ctx].reshape(nb, seq, d), x[n_ctx:].reshape(db, dseq, d),
            jnp.stack(new_ckv, axis=1), jnp.stack(new_kpe, axis=1), jnp.stack(new_dk, axis=1),
            jnp.stack(new_dv, axis=1), jnp.stack(new_st, axis=1))
```

```python
import functools
import math

import jax
import jax.numpy as jnp
import numpy as np
from jax import lax
from jax.experimental import pallas as pl
from jax.experimental.pallas import tpu as pltpu

F32 = jnp.float32
BF16 = jnp.bfloat16

GRID_W = 64
ROPE_BASE = 10000.0
HEADS = 8
HEAD_W = 128
HG_CHUNK = 32
MLA_Q_RANK = 512
MLA_KV_RANK = 256
MLA_NOPE = 128
MLA_ROPE = 64
MLA_QK_W = 256
DF_DH = 64
BRANCH_W = HEADS * HEAD_W
PEER_NKEYS = 128
PEER_TOPK = 16
N_MOD = 6
LANES = 128
VMEM_LIMIT = 52 * 1024 * 1024

COL_HQ, COL_FF, COL_FB, COL_HI, COL_HG = 0, 1024, 2048, 3072, 4096
COL_DQ, COL_DK, COL_DV = 5120, 6144, 7168
COL_GA = 8192
COL_CQ, COL_CKV, COL_KPE = 14336, 14848, 15104
IN_W_PAD = 15360

NT_DIMS = (((1,), (1,)), ((), ()))
TN_DIMS = (((0,), (0,)), ((), ()))


def _params(sem, vmem=VMEM_LIMIT):
    return pltpu.CompilerParams(dimension_semantics=sem, vmem_limit_bytes=vmem)


def _layer_norm(x, eps=1e-5):
    xc = x - jnp.mean(x, axis=-1, keepdims=True)
    return xc * lax.rsqrt(jnp.mean(xc * xc, axis=-1, keepdims=True) + eps)


def _rms_norm(x, g, eps=1e-6):
    return x * lax.rsqrt(jnp.mean(x * x, axis=-1, keepdims=True) + eps) * g


def _silu(x):
    return x * jax.nn.sigmoid(x)


def _gelu(x):
    return 0.5 * x * (1.0 + lax.erf(x * (2.0 ** -0.5)))


def _ada_kernel(c_ref, w_ref, b_ref, o_ref):
    s = _silu(c_ref[...]).astype(BF16)
    o_ref[0] = jnp.dot(s, w_ref[0].astype(BF16), preferred_element_type=F32) + b_ref[0]


def _ada_call(cvec, w_ada, b_ada):
    depth, d, w = w_ada.shape
    tn = 1024
    return pl.pallas_call(
        _ada_kernel,
        grid=(depth, w // tn),
        in_specs=[pl.BlockSpec((8, d), lambda l, j: (0, 0)),
                  pl.BlockSpec((1, d, tn), lambda l, j: (l, 0, j)),
                  pl.BlockSpec((1, 1, tn), lambda l, j: (l, 0, j))],
        out_specs=pl.BlockSpec((1, 8, tn), lambda l, j: (l, 0, j)),
        out_shape=jax.ShapeDtypeStruct((depth, 8, w), F32),
        compiler_params=_params(("parallel", "parallel")),
        name="ada",
    )(cvec, w_ada, b_ada.reshape(depth, 1, w))


def _inproj_kernel(x_ref, mod_ref, w_ref, o_ref, h_scr):
    @pl.when(pl.program_id(1) == 0)
    def _():
        shift, scale = mod_ref[0, 0:1, :], mod_ref[0, 1:2, :]
        h_scr[...] = (_layer_norm(x_ref[...]) * (1.0 + scale) + shift).astype(BF16)

    o_ref[...] = jnp.dot(h_scr[...], w_ref[...], preferred_element_type=F32)


def _inproj_call(x, mod, w, mod_idx, tm):
    n, d = x.shape
    wp = w.shape[1]
    tn = 1024
    return pl.pallas_call(
        _inproj_kernel,
        grid=(n // tm, wp // tn),
        in_specs=[pl.BlockSpec((tm, d), lambda i, j: (i, 0)),
                  pl.BlockSpec((1, N_MOD, d), lambda i, j: (mod_idx(i), 0, 0)),
                  pl.BlockSpec((d, tn), lambda i, j: (0, j))],
        out_specs=pl.BlockSpec((tm, tn), lambda i, j: (i, j)),
        out_shape=jax.ShapeDtypeStruct((n, wp), F32),
        scratch_shapes=[pltpu.VMEM((tm, d), BF16)],
        compiler_params=_params(("parallel", "arbitrary")),
        name="inproj",
    )(x, mod, w)


def _hgrn_chunk_pre(direction, q, z, v, lbp, tri):
    c = HG_CHUNK
    loglb, log1mlb, omlb = lbp[0:1, :], lbp[1:2, :], lbp[2:3, :]
    log_f = jnp.logaddexp(loglb, log1mlb + jax.nn.log_sigmoid(z))
    k = omlb * jax.nn.sigmoid(-z)
    q = _silu(q)

    hi = log_f.astype(BF16)
    r1 = log_f - hi.astype(F32)
    mid = r1.astype(BF16)
    lo = (r1 - mid.astype(F32)).astype(BF16)
    cs = jnp.dot(tri, jnp.concatenate([hi, mid, lo], axis=1), preferred_element_type=F32)
    b = cs[:, 0:LANES] + cs[:, LANES:2 * LANES] + cs[:, 2 * LANES:3 * LANES]
    total = b[c - 1:c, :] if direction == 0 else b[0:1, :]

    qe = (q * jnp.exp(b)).astype(BF16)
    kd = (k * jnp.exp(total - b)).astype(BF16)
    vb = v.astype(BF16)

    row = lax.broadcasted_iota(jnp.int32, (8, LANES), 0)
    cols = []
    for s in range(c):
        pieces = []
        for j in range(c // 8):
            lo_t, hi_t = 8 * j, 8 * j + 7
            if direction == 0:
                dead, full = hi_t < s, lo_t >= s
            else:
                dead, full = lo_t > s, hi_t <= s
            if dead:
                pieces.append(jnp.zeros((8, LANES), F32))
                continue
            val = q[lo_t:lo_t + 8, :] * jnp.exp(b[lo_t:lo_t + 8, :] - b[s:s + 1, :]) * k[s:s + 1, :]
            if not full:
                keep = (row + lo_t >= s) if direction == 0 else (row + lo_t <= s)
                val = jnp.where(keep, val, 0.0)
            pieces.append(val)
        cols.append(jnp.concatenate(pieces, axis=0))
    pcat = jnp.concatenate(cols, axis=1).astype(BF16)
    return qe, kd, vb, total, pcat


def _hgrn_chunk_post(qe, kd, vb, total, att, st):
    o = lax.dot_general(qe, st.astype(BF16), NT_DIMS, preferred_element_type=F32)
    o = o + jnp.dot(att.astype(BF16), vb, preferred_element_type=F32)
    st = st * jnp.exp(total) + lax.dot_general(vb, kd, TN_DIMS, preferred_element_type=F32)
    return o, st


def _hgrn_kernel(qf_ref, ff_ref, vf_ref, qb_ref, fb_ref, vb_ref, lbp_ref, s0_ref, tri_ref, e_ref, *rest,
                 tile, heads, aliased):
    of_ref, ob_ref, sout_ref, st_scr = rest[2:] if aliased else rest
    i = pl.program_id(2)
    c = HG_CHUNK
    n = tile // c

    @pl.when(i == 0)
    def _():
        for hh in range(heads):
            for direction in range(2):
                st_scr[hh, direction] = s0_ref[0, direction, hh].T

    def body(j, carry):
        chains = []
        for hh in range(heads):
            cs = slice(hh * HEAD_W, (hh + 1) * HEAD_W)
            for direction, q_ref, f_ref, v_ref, o_ref in ((0, qf_ref, ff_ref, vf_ref, of_ref),
                                                          (1, qb_ref, fb_ref, vb_ref, ob_ref)):
                r0 = pl.multiple_of((j if direction == 0 else n - 1 - j) * c, c)
                rows = pl.ds(r0, c)
                pre = _hgrn_chunk_pre(direction, q_ref[rows, cs], f_ref[rows, cs], v_ref[rows, cs],
                                      lbp_ref[direction, :, cs], tri_ref[direction])
                chains.append((hh, direction, o_ref, rows, cs, pre))
        att = jnp.dot(jnp.concatenate([ch[5][4] for ch in chains], axis=0), e_ref[...],
                      preferred_element_type=F32)
        for idx, (hh, direction, o_ref, rows, cs, pre) in enumerate(chains):
            o, st = _hgrn_chunk_post(*pre[:4], att[idx * c:(idx + 1) * c, :], st_scr[hh, direction])
            o_ref[rows, cs] = o
            st_scr[hh, direction] = st
        return carry

    lax.fori_loop(0, n, body, 0)

    @pl.when(i == pl.num_programs(2) - 1)
    def _():
        for hh in range(heads):
            for direction in range(2):
                sout_ref[0, direction, hh] = st_scr[hh, direction].T


def _hgrn_call(y, lbp, s0, tri, emat, *, seq, nseq, row0, tile, heads, prev=None):
    n = y.shape[0]
    nt = seq // tile
    w = heads * HEAD_W
    blk0 = row0 // tile

    def col(c0, rev):
        def index(b, g, i):
            t = nt - 1 - i if rev else i
            return (blk0 + b * nt + t, c0 // w + g)
        return pl.BlockSpec((tile, w), index)

    state_spec = pl.BlockSpec((1, 2, heads, HEAD_W, HEAD_W), lambda b, g, i: (b, 0, g, 0, 0))
    out_f = pl.BlockSpec((tile, w), lambda b, g, i: (blk0 + b * nt + i, g))
    out_b = pl.BlockSpec((tile, w), lambda b, g, i: (blk0 + b * nt + nt - 1 - i, g))
    in_specs = [col(COL_HQ, False), col(COL_FF, False), col(COL_HI, False),
                col(COL_HQ, True), col(COL_FB, True), col(COL_HI, True),
                pl.BlockSpec((2, 3, w), lambda b, g, i: (0, 0, g)),
                state_spec,
                pl.BlockSpec((2, HG_CHUNK, HG_CHUNK), lambda b, g, i: (0, 0, 0)),
                pl.BlockSpec((HG_CHUNK * HEAD_W, HG_CHUNK), lambda b, g, i: (0, 0))]
    args = [y, y, y, y, y, y, lbp, s0, tri, emat]
    aliases = {}
    if prev is not None:
        in_specs += [pl.BlockSpec(memory_space=pl.ANY), pl.BlockSpec(memory_space=pl.ANY)]
        aliases = {len(args): 0, len(args) + 1: 1}
        args += list(prev)
    return pl.pallas_call(
        functools.partial(_hgrn_kernel, tile=tile, heads=heads, aliased=prev is not None),
        grid=(nseq, HEADS // heads, nt),
        in_specs=in_specs,
        out_specs=[out_f, out_b, state_spec],
        out_shape=[jax.ShapeDtypeStruct((n, BRANCH_W), F32), jax.ShapeDtypeStruct((n, BRANCH_W), F32),
                   jax.ShapeDtypeStruct((nseq, 2, HEADS, HEAD_W, HEAD_W), F32)],
        scratch_shapes=[pltpu.VMEM((heads, 2, HEAD_W, HEAD_W), F32)],
        input_output_aliases=aliases,
        compiler_params=_params(("parallel", "parallel", "arbitrary")),
        name="hgrn",
    )(*args)


def _hfin_kernel(of_ref, ob_ref, g_ref, gn_ref, o_ref):
    for h in range(HEADS):
        cs = slice(h * HEAD_W, (h + 1) * HEAD_W)
        o = _rms_norm(of_ref[:, cs] + ob_ref[:, cs], gn_ref[:, cs])
        o_ref[:, cs] = (o * _silu(g_ref[:, cs])).astype(BF16)


def _hfin_call(of, ob, y, gnorm, tm):
    n = of.shape[0]
    row = pl.BlockSpec((tm, BRANCH_W), lambda i: (i, 0))
    return pl.pallas_call(
        _hfin_kernel,
        grid=(n // tm,),
        in_specs=[row, row, pl.BlockSpec((tm, BRANCH_W), lambda i: (i, COL_HG // BRANCH_W)),
                  pl.BlockSpec((1, BRANCH_W), lambda i: (0, 0))],
        out_specs=row,
        out_shape=jax.ShapeDtypeStruct((n, BRANCH_W), BF16),
        compiler_params=_params(("parallel",)),
        name="hfin",
    )(of, ob, y, gnorm)


def _rope(x, cos, sin_signed, lo_half):
    rot = jnp.where(lo_half, pltpu.roll(x, LANES - 16, 1), pltpu.roll(x, 16, 1))
    return x * cos + rot * sin_signed


def _prep_kernel(cq_ref, ckv_ref, kpe_ref, dq_ref, dk_ref, qn_ref, kvn_ref, wuq_ref,
                 cosm_ref, sinm_ref, cosd_ref, sind_ref,
                 qcat_ref, ckvn_ref, kper_ref, qd_ref, kd_ref):
    lo_half = (lax.broadcasted_iota(jnp.int32, (1, LANES), 1) % 32) < 16
    cq = _rms_norm(cq_ref[...], qn_ref[...]).astype(BF16)
    qm = jnp.dot(cq, wuq_ref[...], preferred_element_type=F32)
    cos_pe, sin_pe = cosm_ref[...], sinm_ref[...]
    for h in range(HEADS):
        c0 = h * MLA_QK_W
        qcat_ref[:, c0:c0 + LANES] = qm[:, c0:c0 + LANES].astype(BF16)
        pe = _rope(qm[:, c0 + LANES:c0 + 2 * LANES], cos_pe, sin_pe, lo_half)
        qcat_ref[:, c0 + LANES:c0 + 2 * LANES] = pe.astype(BF16)
    ckvn_ref[...] = _rms_norm(ckv_ref[...], kvn_ref[...])
    kper_ref[...] = _rope(kpe_ref[...], cos_pe, sin_pe, lo_half)
    cos_d, sin_d = cosd_ref[...], sind_ref[...]
    for h in range(HEADS):
        cs = slice(h * LANES, (h + 1) * LANES)
        qd_ref[:, cs] = _rope(dq_ref[:, cs], cos_d, sin_d, lo_half).astype(BF16)
        kd_ref[:, cs] = _rope(dk_ref[:, cs], cos_d, sin_d, lo_half)


def _prep_call(y, qn, kvn, wuq, cosm, sinm, cosd, sind, tm):
    n = y.shape[0]
    row = lambda w, c0: pl.BlockSpec((tm, w), lambda i, c0=c0, w=w: (i, c0 // w))
    full = lambda a: pl.BlockSpec(a.shape, lambda i: (0, 0))
    tab = pl.BlockSpec((tm, LANES), lambda i: (i, 0))
    out = lambda w: pl.BlockSpec((tm, w), lambda i: (i, 0))
    return pl.pallas_call(
        _prep_kernel,
        grid=(n // tm,),
        in_specs=[row(MLA_Q_RANK, COL_CQ), row(MLA_KV_RANK, COL_CKV), row(LANES, COL_KPE),
                  row(BRANCH_W, COL_DQ), row(BRANCH_W, COL_DK), full(qn), full(kvn), full(wuq),
                  tab, tab, tab, tab],
        out_specs=[out(HEADS * MLA_QK_W), out(MLA_KV_RANK), out(LANES), out(BRANCH_W), out(BRANCH_W)],
        out_shape=[jax.ShapeDtypeStruct((n, HEADS * MLA_QK_W), BF16),
                   jax.ShapeDtypeStruct((n, MLA_KV_RANK), F32),
                   jax.ShapeDtypeStruct((n, LANES), F32),
                   jax.ShapeDtypeStruct((n, BRANCH_W), BF16),
                   jax.ShapeDtypeStruct((n, BRANCH_W), F32)],
        compiler_params=_params(("parallel",)),
        name="prep",
    )(y, y, y, y, y, qn, kvn, wuq, cosm, sinm, cosd, sind)


def _kvup_kernel(ckv_ref, kpe_ref, wk_ref, wvt_ref, kcat_ref, vt_ref):
    ckv = ckv_ref[...]
    kn = jnp.dot(ckv, wk_ref[...], preferred_element_type=F32)
    kpe = kpe_ref[...]
    for h in range(HEADS):
        c0 = h * MLA_QK_W
        kcat_ref[:, c0:c0 + LANES] = kn[:, h * LANES:(h + 1) * LANES].astype(BF16)
        kcat_ref[:, c0 + LANES:c0 + 2 * LANES] = kpe
    vt_ref[...] = lax.dot_general(wvt_ref[...], ckv, NT_DIMS, preferred_element_type=F32).astype(BF16)


def _kvup_call(ckv_all, kpe_all, wk, wvt, tk):
    r = ckv_all.shape[0]
    return pl.pallas_call(
        _kvup_kernel,
        grid=(r // tk,),
        in_specs=[pl.BlockSpec((tk, MLA_KV_RANK), lambda i: (i, 0)),
                  pl.BlockSpec((tk, LANES), lambda i: (i, 0)),
                  pl.BlockSpec(wk.shape, lambda i: (0, 0)),
                  pl.BlockSpec(wvt.shape, lambda i: (0, 0))],
        out_specs=[pl.BlockSpec((tk, HEADS * MLA_QK_W), lambda i: (i, 0)),
                   pl.BlockSpec((BRANCH_W, tk), lambda i: (0, i))],
        out_shape=[jax.ShapeDtypeStruct((r, HEADS * MLA_QK_W), BF16),
                   jax.ShapeDtypeStruct((BRANCH_W, r), BF16)],
        compiler_params=_params(("parallel",)),
        name="kvup",
    )(ckv_all, kpe_all, wk, wvt)


def _exp_cols(s, scale):
    e = jnp.exp2((s - jnp.max(s, axis=0, keepdims=True)) * (scale * math.log2(math.e)))
    return e, 1.0 / jnp.sum(e, axis=0, keepdims=True)


def _attn_kernel(qm_ref, km_ref, vmt_ref, qd_ref, kd_ref, vdt_ref, lam_ref, gn_ref, *rest,
                 heads, out_scale, aliased):
    om_ref, od_ref = rest[2:] if aliased else rest
    lam = lam_ref[:, 0:1]
    gn = gn_ref[...]
    lane = lax.broadcasted_iota(jnp.int32, (1, LANES), 1)
    for h in range(heads):
        qk = slice(h * MLA_QK_W, (h + 1) * MLA_QK_W)
        hw = slice(h * HEAD_W, (h + 1) * HEAD_W)
        s = lax.dot_general(km_ref[:, qk], qm_ref[:, qk], NT_DIMS, preferred_element_type=F32)
        e, r = _exp_cols(s, (MLA_NOPE + MLA_ROPE) ** -0.5)
        o_t = jnp.dot(vmt_ref[hw, :], (e * r).astype(BF16), preferred_element_type=F32)
        om_ref[:, hw] = o_t.T.astype(BF16)

        q = qd_ref[:, hw]
        zero = jnp.zeros_like(q)
        k = kd_ref[:, hw]
        s1 = lax.dot_general(k, jnp.where(lane < DF_DH, q, zero), NT_DIMS, preferred_element_type=F32)
        s2 = lax.dot_general(k, jnp.where(lane >= DF_DH, q, zero), NT_DIMS, preferred_element_type=F32)
        e1, r1 = _exp_cols(s1, DF_DH ** -0.5)
        e2, r2 = _exp_cols(s2, DF_DH ** -0.5)
        w = (e1 * r1 - e2 * (lam * r2)).astype(BF16)
        o_t = jnp.dot(vdt_ref[hw, :], w, preferred_element_type=F32)
        od_ref[:, hw] = (_rms_norm(o_t.T, gn) * out_scale).astype(BF16)


def _attn_call(qcat, qd, kcat, vmt, kd, vdt, lam, gn, out_scale, *, nseq, tq, q_tiles, q_row0, tk,
               k_block0, heads, prev=None):
    n = qcat.shape[0]
    qb0 = q_row0 // tq
    qmap = lambda b, g, i: (qb0 + b * q_tiles + i, g)
    kmap = lambda b, g, i: (k_block0 + b, g)
    vmap = lambda b, g, i: (g, k_block0 + b)
    in_specs = [pl.BlockSpec((tq, heads * MLA_QK_W), qmap),
                pl.BlockSpec((tk, heads * MLA_QK_W), kmap),
                pl.BlockSpec((heads * HEAD_W, tk), vmap),
                pl.BlockSpec((tq, heads * HEAD_W), qmap),
                pl.BlockSpec((tk, heads * HEAD_W), kmap),
                pl.BlockSpec((heads * HEAD_W, tk), vmap),
                pl.BlockSpec((1, LANES), lambda b, g, i: (0, 0)),
                pl.BlockSpec((1, LANES), lambda b, g, i: (0, 0))]
    args = [qcat, kcat, vmt, qd, kd, vdt, lam, gn]
    aliases = {}
    if prev is not None:
        in_specs += [pl.BlockSpec(memory_space=pl.ANY), pl.BlockSpec(memory_space=pl.ANY)]
        aliases = {len(args): 0, len(args) + 1: 1}
        args += list(prev)
    out_spec = pl.BlockSpec((tq, heads * HEAD_W), qmap)
    return pl.pallas_call(
        functools.partial(_attn_kernel, heads=heads, out_scale=out_scale, aliased=prev is not None),
        grid=(nseq, HEADS // heads, q_tiles),
        in_specs=in_specs,
        out_specs=[out_spec, out_spec],
        out_shape=[jax.ShapeDtypeStruct((n, BRANCH_W), BF16), jax.ShapeDtypeStruct((n, BRANCH_W), BF16)],
        input_output_aliases=aliases,
        compiler_params=_params(("parallel", "parallel", "arbitrary")),
        name="attn",
    )(*args)


def _merge_kernel(oh_ref, om_ref, od_ref, ga_ref, gb_ref, gc_ref, w_ref, o_ref):
    acc = jax.nn.sigmoid(ga_ref[...]) * jnp.dot(oh_ref[...], w_ref[0], preferred_element_type=F32)
    acc += jax.nn.sigmoid(gb_ref[...]) * jnp.dot(om_ref[...], w_ref[1], preferred_element_type=F32)
    acc += jax.nn.sigmoid(gc_ref[...]) * jnp.dot(od_ref[...], w_ref[2], preferred_element_type=F32)
    o_ref[...] = acc.astype(BF16)


def _merge_call(oh, om, od, y, wbr, tm):
    n = oh.shape[0]
    d = wbr.shape[2]
    tn = 1024
    br = pl.BlockSpec((tm, BRANCH_W), lambda j, i: (i, 0))
    gate = lambda g: pl.BlockSpec((tm, tn), lambda j, i, g=g: (i, (COL_GA + g * d) // tn + j))
    return pl.pallas_call(
        _merge_kernel,
        grid=(d // tn, n // tm),
        in_specs=[br, br, br, gate(0), gate(1), gate(2),
                  pl.BlockSpec((3, BRANCH_W, tn), lambda j, i: (0, 0, j))],
        out_specs=pl.BlockSpec((tm, tn), lambda j, i: (i, j)),
        out_shape=jax.ShapeDtypeStruct((n, d), BF16),
        compiler_params=_params(("parallel", "parallel")),
        name="merge",
    )(oh, om, od, y, y, y, wbr)


def _outproj_kernel(m_ref, x_ref, mod_ref, w_ref, g_ref, b_ref, x1_ref, h2_ref, *, alpha):
    gate1 = mod_ref[0, 2:3, :]
    shift2, scale2 = mod_ref[0, 3:4, :], mod_ref[0, 4:5, :]
    y = jnp.dot(m_ref[...], w_ref[...], preferred_element_type=F32)
    x1 = _layer_norm(alpha * x_ref[...] + gate1 * y) * g_ref[...] + b_ref[...]
    x1_ref[...] = x1
    h2_ref[...] = (_layer_norm(x1) * (1.0 + scale2) + shift2).astype(BF16)


def _outproj_call(merged, x, mod, w, g, b, mod_idx, alpha, tm):
    n, d = x.shape
    row = pl.BlockSpec((tm, d), lambda i: (i, 0))
    vec = pl.BlockSpec((1, d), lambda i: (0, 0))
    return pl.pallas_call(
        functools.partial(_outproj_kernel, alpha=alpha),
        grid=(n // tm,),
        in_specs=[row, row, pl.BlockSpec((1, N_MOD, d), lambda i: (mod_idx(i), 0, 0)),
                  pl.BlockSpec((d, d), lambda i: (0, 0)), vec, vec],
        out_specs=[row, row],
        out_shape=[jax.ShapeDtypeStruct((n, d), F32), jax.ShapeDtypeStruct((n, d), BF16)],
        compiler_params=_params(("parallel",)),
        name="outproj",
    )(merged, x, mod, w, g, b)


def _peerq_kernel(h_ref, w_ref, sk_ref, st_ref):
    q = jnp.dot(h_ref[...], w_ref[...], preferred_element_type=F32).astype(BF16)
    for hp in range(2 * HEADS):
        st_ref[hp // 2, hp % 2] = lax.dot_general(sk_ref[hp % 2], q[:, hp * LANES:(hp + 1) * LANES], NT_DIMS,
                                                  preferred_element_type=F32)


def _peerq_call(h2, wq, subkeys, tm):
    n, d = h2.shape
    return pl.pallas_call(
        _peerq_kernel,
        grid=(n // tm,),
        in_specs=[pl.BlockSpec((tm, d), lambda i: (i, 0)),
                  pl.BlockSpec(wq.shape, lambda i: (0, 0)),
                  pl.BlockSpec(subkeys.shape, lambda i: (0, 0, 0))],
        out_specs=pl.BlockSpec((HEADS, 2, PEER_NKEYS, tm), lambda i: (0, 0, 0, i)),
        out_shape=jax.ShapeDtypeStruct((HEADS, 2, PEER_NKEYS, n), F32),
        compiler_params=_params(("parallel",)),
        name="peerq",
    )(h2, wq, subkeys)


def _top_values(x, count):
    rows = lax.broadcasted_iota(jnp.int32, x.shape, 0).astype(F32)
    vals = []
    for r in range(count):
        m = jnp.max(x, axis=0, keepdims=True)
        vals.append(m)
        if r + 1 < count:
            first = jnp.min(jnp.where(x == m, rows, float(x.shape[0])), axis=0, keepdims=True)
            x = jnp.where(rows == first, -jnp.inf, x)
    return vals


def _topk_kernel(st_ref, e1_ref, e2_ref, th_ref, v_scr, cand_scr):
    row8 = lax.broadcasted_iota(jnp.int32, (8, st_ref.shape[3]), 0)
    for h in range(HEADS):
        s1, s2 = st_ref[h, 0], st_ref[h, 1]
        v1 = _top_values(s1, PEER_TOPK)
        v2 = _top_values(s2, PEER_TOPK)
        for r in range(PEER_TOPK):
            v_scr[0, r:r + 1, :] = v1[r]
            v_scr[1, r:r + 1, :] = v2[r]
        cand_scr[0:16, :] = v1[0] + v_scr[1]
        for a in range(1, 8):
            pair = v1[a] + v_scr[1, 0:8, :]
            cand_scr[8 + 8 * a:16 + 8 * a, :] = jnp.where(row8 < PEER_TOPK // (a + 1), pair, -jnp.inf)
        cand_scr[72:80, :] = v_scr[0, 8:16, :] + v2[0]
        top = _top_values(cand_scr[...], PEER_TOPK)
        z = jnp.exp(top[0] - top[0])
        for r in range(1, PEER_TOPK):
            z = z + jnp.exp(top[r] - top[0])
        e1 = jnp.exp(s1 - v1[0]) * (1.0 / z)
        e2_ref[h] = jnp.exp(s2 - v2[0])
        th = jnp.full(s1.shape, jnp.inf, F32)
        for b in range(PEER_TOPK):
            th = jnp.where(s1 + v2[b] >= top[PEER_TOPK - 1], v2[b], th)
        for c in range(s1.shape[1] // LANES):
            e1_ref[h, c] = e1[:, c * LANES:(c + 1) * LANES]
            th_ref[h, c] = th[:, c * LANES:(c + 1) * LANES]


def _topk_call(st, tm):
    n = st.shape[3]
    out = pl.BlockSpec((HEADS, PEER_NKEYS, tm), lambda i: (0, 0, i))
    shape = jax.ShapeDtypeStruct((HEADS, PEER_NKEYS, n), F32)
    out4 = pl.BlockSpec((HEADS, tm // LANES, PEER_NKEYS, LANES), lambda i: (0, i, 0, 0))
    shape4 = jax.ShapeDtypeStruct((HEADS, n // LANES, PEER_NKEYS, LANES), F32)
    return pl.pallas_call(
        _topk_kernel,
        grid=(n // tm,),
        in_specs=[pl.BlockSpec((HEADS, 2, PEER_NKEYS, tm), lambda i: (0, 0, 0, i))],
        out_specs=[out4, out, out4],
        out_shape=[shape4, shape, shape4],
        scratch_shapes=[pltpu.VMEM((2, PEER_TOPK, tm), F32), pltpu.VMEM((80, tm), F32)],
        compiler_params=_params(("parallel",)),
        name="topk",
    )(st)


def _peer_kernel(ht_ref, s2_ref, e1_ref, e2_ref, th_ref, u_ref, vt_ref, ot_ref, act_scr, *, rows_per_step):
    j = pl.program_id(1)
    slot = j % 2
    tm = ht_ref.shape[1]

    @pl.when(j == 0)
    def _():
        ot_ref[...] = jnp.zeros_like(ot_ref)
        act_scr[1] = jnp.zeros(act_scr.shape[1:], F32)

    i0 = jnp.maximum(j - 1, 0) * rows_per_step
    parts = []
    for r in range(rows_per_step):
        rows = slice(r * PEER_NKEYS, (r + 1) * PEER_NKEYS)
        row_i = pl.ds(i0 + r, 1)
        cparts = []
        for c in range(tm // LANES):
            cols = slice(c * LANES, (c + 1) * LANES)
            g = jnp.zeros((PEER_NKEYS, LANES), F32)
            for h in range(HEADS):
                keep = s2_ref[h, :, cols] >= th_ref[h, c, row_i, :]
                g = g + jnp.where(keep, e2_ref[h, :, cols], 0.0) * e1_ref[h, c, row_i, :]
            cparts.append((g * act_scr[1 - slot, rows, cols]).astype(BF16))
        parts.append(jnp.concatenate(cparts, axis=1))
    ga = jnp.concatenate(parts, axis=0)
    ot_ref[...] += jnp.dot(vt_ref[...], ga, preferred_element_type=F32)

    a_t = jnp.dot(u_ref[...], ht_ref[...], preferred_element_type=F32)
    act_scr[slot] = _gelu(a_t)


def _peer_call(h2t, st, e1, e2, th, u, vt, tm, rows_per_step):
    d, n = h2t.shape
    te = rows_per_step * PEER_NKEYS
    nj = u.shape[0] // te
    once = pl.Buffered(1)
    tok = pl.BlockSpec((HEADS, PEER_NKEYS, tm), lambda i, j: (0, 0, i), pipeline_mode=once)
    tok4 = pl.BlockSpec((HEADS, tm // LANES, PEER_NKEYS, LANES), lambda i, j: (0, i, 0, 0), pipeline_mode=once)
    return pl.pallas_call(
        functools.partial(_peer_kernel, rows_per_step=rows_per_step),
        grid=(n // tm, nj + 1),
        in_specs=[pl.BlockSpec((d, tm), lambda i, j: (0, i), pipeline_mode=once),
                  pl.BlockSpec((HEADS, None, PEER_NKEYS, tm), lambda i, j: (0, 1, 0, i), pipeline_mode=once),
                  tok4, tok, tok4,
                  pl.BlockSpec((te, d), lambda i, j: (jnp.minimum(j, nj - 1), 0)),
                  pl.BlockSpec((d, te), lambda i, j: (0, jnp.maximum(j - 1, 0)))],
        out_specs=pl.BlockSpec((d, tm), lambda i, j: (0, i)),
        out_shape=jax.ShapeDtypeStruct((d, n), F32),
        scratch_shapes=[pltpu.VMEM((2, te, tm), F32)],
        compiler_params=_params(("parallel", "arbitrary")),
        name="peer",
    )(h2t, st, e1, e2, th, u, vt)


def _ln2_kernel(x_ref, yt_ref, mod_ref, g_ref, b_ref, o_ref, *, alpha):
    gate2 = mod_ref[0, 5:6, :]
    o_ref[...] = _layer_norm(alpha * x_ref[...] + gate2 * yt_ref[...].T) * g_ref[...] + b_ref[...]


def _ln2_call(x1, y2t, mod, g, b, mod_idx, alpha, tm):
    n, d = x1.shape
    row = pl.BlockSpec((tm, d), lambda i: (i, 0))
    vec = pl.BlockSpec((1, d), lambda i: (0, 0))
    return pl.pallas_call(
        functools.partial(_ln2_kernel, alpha=alpha),
        grid=(n // tm,),
        in_specs=[row, pl.BlockSpec((d, tm), lambda i: (0, i)),
                  pl.BlockSpec((1, N_MOD, d), lambda i: (mod_idx(i), 0, 0)), vec, vec],
        out_specs=row,
        out_shape=jax.ShapeDtypeStruct((n, d), F32),
        compiler_params=_params(("parallel",)),
        name="ln2",
    )(x1, y2t, mod, g, b)


def _rope_tables(n_ctx, n_lat_seq, lat_seq):
    quarter = 16
    t = jnp.arange(lat_seq)
    rowp = (t // GRID_W).astype(F32)
    colp = (t % GRID_W).astype(F32)
    inv_freq = ROPE_BASE ** (-jnp.arange(quarter, dtype=F32) / quarter)
    ang = jnp.stack([rowp[:, None] * inv_freq, colp[:, None] * inv_freq], axis=1)
    ang = jnp.concatenate([ang, ang], axis=-1).reshape(lat_seq, 64)
    sign = jnp.where((jnp.arange(64) % 32) < 16, -1.0, 1.0).astype(F32)
    cos, sin = jnp.cos(ang), jnp.sin(ang) * sign
    ones, zeros = jnp.ones((lat_seq, 64), F32), jnp.zeros((lat_seq, 64), F32)

    def full(tab, ident):
        lat = jnp.tile(tab, (n_lat_seq, 1))
        return jnp.concatenate([jnp.full((n_ctx, LANES), ident, F32), lat], axis=0)

    cosd = full(jnp.concatenate([cos, cos], axis=1), 1.0)
    sind = full(jnp.concatenate([sin, sin], axis=1), 0.0)
    cosm = full(jnp.concatenate([cos, ones], axis=1), 1.0)
    sinm = full(jnp.concatenate([sin, zeros], axis=1), 0.0)
    return cosd, sind, cosm, sinm


def kernel(x_prompt, x_sample, cache_mla_ckv, cache_mla_kpe, cache_diff_k, cache_diff_v, state_hgrn,
           c, c_ctx, w_ada, b_ada, w_in, hg_lb_logits, hg_norm, mla_q_norm, w_uq, mla_kv_norm, w_ukv,
           df_lambda, df_norm, w_branch, w_out, ln1_g, ln1_b, ln2_g, ln2_b,
           peer_wq, peer_subkeys, peer_u, peer_v):
    nb, seq, d = x_prompt.shape
    db, dseq, _ = x_sample.shape
    depth = w_in.shape[0]
    past = cache_mla_ckv.shape[2]
    n_ctx, n_lat = nb * seq, db * dseq
    n = n_ctx + n_lat
    tk_lat = past + dseq
    tm = 512
    tm_in = 2 * tm if n_ctx % (2 * tm) == 0 and dseq % (2 * tm) == 0 else tm
    tq = 256
    hg_tile = min(dseq, 512)
    assert n_ctx % tm == 0 and dseq % tm == 0 and (db * tk_lat) % seq == 0 and seq % tq == 0
    alpha = (2 * depth) ** 0.25

    def mod_idx(i, tile=tm):
        return jnp.where(i < n_ctx // tile, 0, 1 + (i - n_ctx // tile) // (dseq // tile))

    lb = jnp.cumsum(jax.nn.softmax(hg_lb_logits.astype(F32), axis=0), axis=0)
    lb = lb - lb[0:1]
    lbp = jnp.stack([jnp.log(lb), jnp.log1p(-lb), 1.0 - lb], axis=2)
    lq = df_lambda.astype(F32)
    lam_init = np.array([0.8 - 0.6 * math.exp(-0.3 * l) for l in range(depth)], np.float32)
    lam = jnp.exp(jnp.sum(lq[:, 0] * lq[:, 1], axis=-1)) - jnp.exp(jnp.sum(lq[:, 2] * lq[:, 3], axis=-1)) + lam_init

    kpe0 = 5 * 1024 + MLA_Q_RANK + MLA_KV_RANK
    w_in_p = jnp.concatenate(
        [w_in[:, :, :5 * 1024], w_in[:, :, kpe0 + MLA_ROPE:], w_in[:, :, 5 * 1024:kpe0],
         w_in[:, :, kpe0:kpe0 + MLA_ROPE],
         jnp.zeros((depth, d, IN_W_PAD - w_in.shape[2]), w_in.dtype)], axis=2).astype(BF16)
    wq4 = w_uq.reshape(depth, MLA_Q_RANK, HEADS, MLA_NOPE + MLA_ROPE)
    w_uq_p = jnp.concatenate(
        [wq4, jnp.zeros((depth, MLA_Q_RANK, HEADS, MLA_QK_W - MLA_NOPE - MLA_ROPE), w_uq.dtype)],
        axis=3).reshape(depth, MLA_Q_RANK, HEADS * MLA_QK_W).astype(BF16)
    wkv4 = w_ukv.reshape(depth, MLA_KV_RANK, HEADS, 2 * HEAD_W)
    w_uk = wkv4[..., :HEAD_W].reshape(depth, MLA_KV_RANK, BRANCH_W).astype(BF16)
    w_uvt = jnp.swapaxes(wkv4[..., HEAD_W:].reshape(depth, MLA_KV_RANK, BRANCH_W), 1, 2).astype(BF16)
    w_branch_b, w_out_b, peer_wq_b = w_branch.astype(BF16), w_out.astype(BF16), peer_wq.astype(BF16)
    subkeys_b, peer_u_b = peer_subkeys.astype(BF16), peer_u.astype(BF16)
    peer_vt_b = jnp.swapaxes(peer_v.astype(BF16), 1, 2)

    tri = jnp.stack([jnp.tril(jnp.ones((HG_CHUNK, HG_CHUNK), F32)),
                     jnp.triu(jnp.ones((HG_CHUNK, HG_CHUNK), F32))]).astype(BF16)
    emat = jnp.repeat(jnp.eye(HG_CHUNK, dtype=F32), HEAD_W, axis=0).astype(BF16)
    cosd, sind, cosm, sinm = _rope_tables(n_ctx, db, dseq)

    cvec = jnp.concatenate([c_ctx[None, :], c, jnp.zeros((8 - 1 - db, d), F32)], axis=0)
    mods = _ada_call(cvec, w_ada, b_ada)[:, :1 + db].reshape(depth, 1 + db, N_MOD, d)

    x = jnp.concatenate([x_prompt.reshape(n_ctx, d), x_sample.reshape(n_lat, d)], axis=0)
    zero_state = jnp.zeros((nb, 2, HEADS, HEAD_W, HEAD_W), F32)
    new_ckv, new_kpe, new_dk, new_dv, new_st = [], [], [], [], []
    for l in range(depth):
        mod = mods[l]
        y = _inproj_call(x, mod, w_in_p[l], functools.partial(mod_idx, tile=tm_in), tm_in)

        of, ob, st_ctx = _hgrn_call(y, lbp[l], zero_state, tri, emat, seq=seq, nseq=nb, row0=0,
                                    tile=seq, heads=4)
        of, ob, _ = _hgrn_call(y, lbp[l], state_hgrn[:, l], tri, emat, seq=dseq, nseq=db, row0=n_ctx,
                               tile=hg_tile, heads=4, prev=(of, ob))
        oh = _hfin_call(of, ob, y, hg_norm[l][None, :], tm)

        qcat, ckvn, kper, qd, kdr = _prep_call(y, mla_q_norm[l][None, :], mla_kv_norm[l][None, :], w_uq_p[l],
                                               cosm, sinm, cosd, sind, tm)
        vd = y[:, COL_DV:COL_DV + BRANCH_W]
        lat3 = lambda a: a[n_ctx:].reshape(db, dseq, a.shape[1])
        keys = lambda cache, a: jnp.concatenate(
            [jnp.concatenate([cache, lat3(a)], axis=1).reshape(db * tk_lat, a.shape[1]), a[:n_ctx]],
            axis=0).astype(BF16)
        kpe_cache = jnp.pad(cache_mla_kpe[:, l], ((0, 0), (0, 0), (0, LANES - MLA_ROPE)))
        kcat, vmt = _kvup_call(keys(cache_mla_ckv[:, l], ckvn), keys(kpe_cache, kper), w_uk[l], w_uvt[l], seq)
        kd_all = keys(cache_diff_k[:, l].reshape(db, past, BRANCH_W), kdr)
        vdt = keys(cache_diff_v[:, l].reshape(db, past, BRANCH_W), vd).T
        lam_row = jnp.full((1, LANES), lam[l], F32)
        attn_args = (qcat, qd, kcat, vmt, kd_all, vdt, lam_row, df_norm[l][None, :], float(1.0 - lam_init[l]))
        om, od = _attn_call(*attn_args, nseq=nb, tq=seq, q_tiles=1, q_row0=0, tk=seq,
                            k_block0=db * tk_lat // seq, heads=HEADS)
        om, od = _attn_call(*attn_args, nseq=db, tq=tq, q_tiles=dseq // tq, q_row0=n_ctx,
                            tk=tk_lat, k_block0=0, heads=1, prev=(om, od))

        merged = _merge_call(oh, om, od, y, w_branch_b[l], tm)
        x1, h2 = _outproj_call(merged, x, mod, w_out_b[l], ln1_g[l][None, :], ln1_b[l][None, :],
                               functools.partial(mod_idx, tile=256), alpha, 256)

        st = _peerq_call(h2, peer_wq_b[l], subkeys_b[l], tm)
        e1, e2, th = _topk_call(st, tm)
        y2t = _peer_call(h2.T, st, e1, e2, th, peer_u_b[l], peer_vt_b[l], tm, 8)
        x = _ln2_call(x1, y2t, mod, ln2_g[l][None, :], ln2_b[l][None, :], mod_idx, alpha, tm)

        new_ckv.append(ckvn[:n_ctx].reshape(nb, seq, MLA_KV_RANK))
        new_kpe.append(kper[:n_ctx, :MLA_ROPE].reshape(nb, seq, MLA_ROPE))
        new_dk.append(y[:n_ctx, COL_DK:COL_DK + BRANCH_W].reshape(nb, seq, HEADS, HEAD_W))
        new_dv.append(vd[:n_ctx].reshape(nb, seq, HEADS, HEAD_W))
        new_st.append(st_ctx)

    return (x[:n_ctx].reshape(nb, seq, d), x[n_ctx:].reshape(db, dseq, d),
            jnp.stack(new_ckv, axis=1), jnp.stack(new_kpe, axis=1), jnp.stack(new_dk, axis=1),
            jnp.stack(new_dv, axis=1), jnp.stack(new_st, axis=1))
```

```python
import functools
import math

import jax
import jax.numpy as jnp
import numpy as np
from jax import lax
from jax.experimental import pallas as pl
from jax.experimental.pallas import tpu as pltpu

F32 = jnp.float32
BF16 = jnp.bfloat16

GRID_W = 64
ROPE_BASE = 10000.0
HEADS = 8
HEAD_W = 128
HG_CHUNK = 32
MLA_Q_RANK = 512
MLA_KV_RANK = 256
MLA_NOPE = 128
MLA_ROPE = 64
MLA_QK_W = 256
DF_DH = 64
BRANCH_W = HEADS * HEAD_W
PEER_NKEYS = 128
PEER_TOPK = 16
N_MOD = 6
LANES = 128
VMEM_LIMIT = 52 * 1024 * 1024

COL_HQ, COL_FF, COL_FB, COL_HI, COL_HG = 0, 1024, 2048, 3072, 4096
COL_DQ, COL_DK, COL_DV = 5120, 6144, 7168
COL_GA = 8192
COL_CQ, COL_CKV, COL_KPE = 14336, 14848, 15104
IN_W_PAD = 15360

NT_DIMS = (((1,), (1,)), ((), ()))
TN_DIMS = (((0,), (0,)), ((), ()))


def _params(sem, vmem=VMEM_LIMIT):
    return pltpu.CompilerParams(dimension_semantics=sem, vmem_limit_bytes=vmem)


def _layer_norm(x, eps=1e-5):
    xc = x - jnp.mean(x, axis=-1, keepdims=True)
    return xc * lax.rsqrt(jnp.mean(xc * xc, axis=-1, keepdims=True) + eps)


def _rms_norm(x, g, eps=1e-6):
    return x * lax.rsqrt(jnp.mean(x * x, axis=-1, keepdims=True) + eps) * g


def _silu(x):
    return x * jax.nn.sigmoid(x)


def _gelu(x):
    return 0.5 * x * (1.0 + lax.erf(x * (2.0 ** -0.5)))


def _ada_kernel(c_ref, w_ref, b_ref, o_ref):
    s = _silu(c_ref[...]).astype(BF16)
    o_ref[0] = jnp.dot(s, w_ref[0].astype(BF16), preferred_element_type=F32) + b_ref[0]


def _ada_call(cvec, w_ada, b_ada):
    depth, d, w = w_ada.shape
    tn = 1024
    return pl.pallas_call(
        _ada_kernel,
        grid=(depth, w // tn),
        in_specs=[pl.BlockSpec((8, d), lambda l, j: (0, 0)),
                  pl.BlockSpec((1, d, tn), lambda l, j: (l, 0, j)),
                  pl.BlockSpec((1, 1, tn), lambda l, j: (l, 0, j))],
        out_specs=pl.BlockSpec((1, 8, tn), lambda l, j: (l, 0, j)),
        out_shape=jax.ShapeDtypeStruct((depth, 8, w), F32),
        compiler_params=_params(("parallel", "parallel")),
        name="ada",
    )(cvec, w_ada, b_ada.reshape(depth, 1, w))


def _inproj_kernel(x_ref, mod_ref, w_ref, o_ref, h_scr):
    @pl.when(pl.program_id(1) == 0)
    def _():
        shift, scale = mod_ref[0, 0:1, :], mod_ref[0, 1:2, :]
        h_scr[...] = (_layer_norm(x_ref[...]) * (1.0 + scale) + shift).astype(BF16)

    o_ref[...] = jnp.dot(h_scr[...], w_ref[...], preferred_element_type=F32)


def _inproj_call(x, mod, w, layer, mod_idx, tm):
    n, d = x.shape
    wp = w.shape[2]
    tn = 1024
    return pl.pallas_call(
        _inproj_kernel,
        grid=(n // tm, wp // tn),
        in_specs=[pl.BlockSpec((tm, d), lambda i, j: (i, 0)),
                  pl.BlockSpec((1, N_MOD, d), lambda i, j: (mod_idx(i), 0, 0)),
                  pl.BlockSpec((None, d, tn), lambda i, j: (layer, 0, j))],
        out_specs=pl.BlockSpec((tm, tn), lambda i, j: (i, j)),
        out_shape=jax.ShapeDtypeStruct((n, wp), F32),
        scratch_shapes=[pltpu.VMEM((tm, d), BF16)],
        compiler_params=_params(("parallel", "arbitrary")),
        name="inproj",
    )(x, mod, w)


def _hgrn_chunk_pre(direction, q, z, v, lbp, tri):
    c = HG_CHUNK
    loglb, log1mlb, omlb = lbp[0:1, :], lbp[1:2, :], lbp[2:3, :]
    log_f = jnp.logaddexp(loglb, log1mlb + jax.nn.log_sigmoid(z))
    k = omlb * jax.nn.sigmoid(-z)
    q = _silu(q)

    hi = log_f.astype(BF16)
    r1 = log_f - hi.astype(F32)
    mid = r1.astype(BF16)
    lo = (r1 - mid.astype(F32)).astype(BF16)
    cs = jnp.dot(tri, jnp.concatenate([hi, mid, lo], axis=1), preferred_element_type=F32)
    b = cs[:, 0:LANES] + cs[:, LANES:2 * LANES] + cs[:, 2 * LANES:3 * LANES]
    total = b[c - 1:c, :] if direction == 0 else b[0:1, :]

    qe = (q * jnp.exp(b)).astype(BF16)
    kd = (k * jnp.exp(total - b)).astype(BF16)
    vb = v.astype(BF16)

    row = lax.broadcasted_iota(jnp.int32, (8, LANES), 0)
    cols = []
    for s in range(c):
        pieces = []
        for j in range(c // 8):
            lo_t, hi_t = 8 * j, 8 * j + 7
            if direction == 0:
                dead, full = hi_t < s, lo_t >= s
            else:
                dead, full = lo_t > s, hi_t <= s
            if dead:
                pieces.append(jnp.zeros((8, LANES), F32))
                continue
            val = q[lo_t:lo_t + 8, :] * jnp.exp(b[lo_t:lo_t + 8, :] - b[s:s + 1, :]) * k[s:s + 1, :]
            if not full:
                keep = (row + lo_t >= s) if direction == 0 else (row + lo_t <= s)
                val = jnp.where(keep, val, 0.0)
            pieces.append(val)
        cols.append(jnp.concatenate(pieces, axis=0))
    pcat = jnp.concatenate(cols, axis=1).astype(BF16)
    return qe, kd, vb, total, pcat


def _hgrn_chunk_post(qe, kd, vb, total, att, st):
    o = lax.dot_general(qe, st.astype(BF16), NT_DIMS, preferred_element_type=F32)
    o = o + jnp.dot(att.astype(BF16), vb, preferred_element_type=F32)
    st = st * jnp.exp(total) + lax.dot_general(vb, kd, TN_DIMS, preferred_element_type=F32)
    return o, st


def _hgrn_kernel(qf_ref, ff_ref, vf_ref, qb_ref, fb_ref, vb_ref, lbp_ref, s0_ref, tri_ref, e_ref, *rest,
                 tile, heads, aliased):
    of_ref, ob_ref, sout_ref, st_scr = rest[2:] if aliased else rest
    i = pl.program_id(2)
    c = HG_CHUNK
    n = tile // c

    @pl.when(i == 0)
    def _():
        for hh in range(heads):
            for direction in range(2):
                st_scr[hh, direction] = s0_ref[0, direction, hh].T

    def body(j, carry):
        chains = []
        for hh in range(heads):
            cs = slice(hh * HEAD_W, (hh + 1) * HEAD_W)
            for direction, q_ref, f_ref, v_ref, o_ref in ((0, qf_ref, ff_ref, vf_ref, of_ref),
                                                          (1, qb_ref, fb_ref, vb_ref, ob_ref)):
                r0 = pl.multiple_of((j if direction == 0 else n - 1 - j) * c, c)
                rows = pl.ds(r0, c)
                pre = _hgrn_chunk_pre(direction, q_ref[rows, cs], f_ref[rows, cs], v_ref[rows, cs],
                                      lbp_ref[direction, :, cs], tri_ref[direction])
                chains.append((hh, direction, o_ref, rows, cs, pre))
        att = jnp.dot(jnp.concatenate([ch[5][4] for ch in chains], axis=0), e_ref[...],
                      preferred_element_type=F32)
        for idx, (hh, direction, o_ref, rows, cs, pre) in enumerate(chains):
            o, st = _hgrn_chunk_post(*pre[:4], att[idx * c:(idx + 1) * c, :], st_scr[hh, direction])
            o_ref[rows, cs] = o
            st_scr[hh, direction] = st
        return carry

    lax.fori_loop(0, n, body, 0)

    @pl.when(i == pl.num_programs(2) - 1)
    def _():
        for hh in range(heads):
            for direction in range(2):
                sout_ref[0, direction, hh] = st_scr[hh, direction].T


def _hgrn_call(y, lbp, s0, tri, emat, *, seq, nseq, row0, tile, heads, prev=None):
    n = y.shape[0]
    nt = seq // tile
    w = heads * HEAD_W
    blk0 = row0 // tile

    def col(c0, rev):
        def index(b, g, i):
            t = nt - 1 - i if rev else i
            return (blk0 + b * nt + t, c0 // w + g)
        return pl.BlockSpec((tile, w), index)

    state_spec = pl.BlockSpec((1, 2, heads, HEAD_W, HEAD_W), lambda b, g, i: (b, 0, g, 0, 0))
    out_f = pl.BlockSpec((tile, w), lambda b, g, i: (blk0 + b * nt + i, g))
    out_b = pl.BlockSpec((tile, w), lambda b, g, i: (blk0 + b * nt + nt - 1 - i, g))
    in_specs = [col(COL_HQ, False), col(COL_FF, False), col(COL_HI, False),
                col(COL_HQ, True), col(COL_FB, True), col(COL_HI, True),
                pl.BlockSpec((2, 3, w), lambda b, g, i: (0, 0, g)),
                state_spec,
                pl.BlockSpec((2, HG_CHUNK, HG_CHUNK), lambda b, g, i: (0, 0, 0)),
                pl.BlockSpec((HG_CHUNK * HEAD_W, HG_CHUNK), lambda b, g, i: (0, 0))]
    args = [y, y, y, y, y, y, lbp, s0, tri, emat]
    aliases = {}
    if prev is not None:
        in_specs += [pl.BlockSpec(memory_space=pl.ANY), pl.BlockSpec(memory_space=pl.ANY)]
        aliases = {len(args): 0, len(args) + 1: 1}
        args += list(prev)
    return pl.pallas_call(
        functools.partial(_hgrn_kernel, tile=tile, heads=heads, aliased=prev is not None),
        grid=(nseq, HEADS // heads, nt),
        in_specs=in_specs,
        out_specs=[out_f, out_b, state_spec],
        out_shape=[jax.ShapeDtypeStruct((n, BRANCH_W), F32), jax.ShapeDtypeStruct((n, BRANCH_W), F32),
                   jax.ShapeDtypeStruct((nseq, 2, HEADS, HEAD_W, HEAD_W), F32)],
        scratch_shapes=[pltpu.VMEM((heads, 2, HEAD_W, HEAD_W), F32)],
        input_output_aliases=aliases,
        compiler_params=_params(("parallel", "parallel", "arbitrary")),
        name="hgrn",
    )(*args)


def _hfin_kernel(of_ref, ob_ref, g_ref, gn_ref, o_ref):
    for h in range(HEADS):
        cs = slice(h * HEAD_W, (h + 1) * HEAD_W)
        o = _rms_norm(of_ref[:, cs] + ob_ref[:, cs], gn_ref[:, cs])
        o_ref[:, cs] = (o * _silu(g_ref[:, cs])).astype(BF16)


def _hfin_call(of, ob, y, gnorm, tm):
    n = of.shape[0]
    row = pl.BlockSpec((tm, BRANCH_W), lambda i: (i, 0))
    return pl.pallas_call(
        _hfin_kernel,
        grid=(n // tm,),
        in_specs=[row, row, pl.BlockSpec((tm, BRANCH_W), lambda i: (i, COL_HG // BRANCH_W)),
                  pl.BlockSpec((1, BRANCH_W), lambda i: (0, 0))],
        out_specs=row,
        out_shape=jax.ShapeDtypeStruct((n, BRANCH_W), BF16),
        compiler_params=_params(("parallel",)),
        name="hfin",
    )(of, ob, y, gnorm)


def _rope(x, cos, sin_signed, lo_half):
    rot = jnp.where(lo_half, pltpu.roll(x, LANES - 16, 1), pltpu.roll(x, 16, 1))
    return x * cos + rot * sin_signed


def _prep_kernel(cq_ref, ckv_ref, kpe_ref, dq_ref, dk_ref, qn_ref, kvn_ref, wuq_ref,
                 cosm_ref, sinm_ref, cosd_ref, sind_ref,
                 qcat_ref, ckvn_ref, kper_ref, qd_ref, kd_ref):
    lo_half = (lax.broadcasted_iota(jnp.int32, (1, LANES), 1) % 32) < 16
    cq = _rms_norm(cq_ref[...], qn_ref[...]).astype(BF16)
    qm = jnp.dot(cq, wuq_ref[...], preferred_element_type=F32)
    cos_pe, sin_pe = cosm_ref[...], sinm_ref[...]
    for h in range(HEADS):
        c0 = h * MLA_QK_W
        qcat_ref[:, c0:c0 + LANES] = qm[:, c0:c0 + LANES].astype(BF16)
        pe = _rope(qm[:, c0 + LANES:c0 + 2 * LANES], cos_pe, sin_pe, lo_half)
        qcat_ref[:, c0 + LANES:c0 + 2 * LANES] = pe.astype(BF16)
    ckvn_ref[...] = _rms_norm(ckv_ref[...], kvn_ref[...])
    kper_ref[...] = _rope(kpe_ref[...], cos_pe, sin_pe, lo_half)
    cos_d, sin_d = cosd_ref[...], sind_ref[...]
    for h in range(HEADS):
        cs = slice(h * LANES, (h + 1) * LANES)
        qd_ref[:, cs] = _rope(dq_ref[:, cs], cos_d, sin_d, lo_half).astype(BF16)
        kd_ref[:, cs] = _rope(dk_ref[:, cs], cos_d, sin_d, lo_half).astype(BF16)


def _prep_call(y, qn, kvn, wuq, cosm, sinm, cosd, sind, tm):
    n = y.shape[0]
    row = lambda w, c0: pl.BlockSpec((tm, w), lambda i, c0=c0, w=w: (i, c0 // w))
    full = lambda a: pl.BlockSpec(a.shape, lambda i: (0, 0))
    tab = pl.BlockSpec((tm, LANES), lambda i: (i, 0))
    out = lambda w: pl.BlockSpec((tm, w), lambda i: (i, 0))
    return pl.pallas_call(
        _prep_kernel,
        grid=(n // tm,),
        in_specs=[row(MLA_Q_RANK, COL_CQ), row(MLA_KV_RANK, COL_CKV), row(LANES, COL_KPE),
                  row(BRANCH_W, COL_DQ), row(BRANCH_W, COL_DK), full(qn), full(kvn), full(wuq),
                  tab, tab, tab, tab],
        out_specs=[out(HEADS * MLA_QK_W), out(MLA_KV_RANK), out(LANES), out(BRANCH_W), out(BRANCH_W)],
        out_shape=[jax.ShapeDtypeStruct((n, HEADS * MLA_QK_W), BF16),
                   jax.ShapeDtypeStruct((n, MLA_KV_RANK), F32),
                   jax.ShapeDtypeStruct((n, LANES), F32),
                   jax.ShapeDtypeStruct((n, BRANCH_W), BF16),
                   jax.ShapeDtypeStruct((n, BRANCH_W), BF16)],
        compiler_params=_params(("parallel",)),
        name="prep",
    )(y, y, y, y, y, qn, kvn, wuq, cosm, sinm, cosd, sind)


def _kvup_kernel(ckv_ref, kpe_ref, wk_ref, wvt_ref, kcat_ref, vt_ref):
    ckv = ckv_ref[...]
    kn = jnp.dot(ckv, wk_ref[...], preferred_element_type=F32)
    kpe = kpe_ref[...]
    for h in range(HEADS):
        c0 = h * MLA_QK_W
        kcat_ref[:, c0:c0 + LANES] = kn[:, h * LANES:(h + 1) * LANES].astype(BF16)
        kcat_ref[:, c0 + LANES:c0 + 2 * LANES] = kpe
    vt_ref[...] = lax.dot_general(wvt_ref[...], ckv, NT_DIMS, preferred_element_type=F32).astype(BF16)


def _kvup_call(ckv_all, kpe_all, wk, wvt, tk):
    r = ckv_all.shape[0]
    return pl.pallas_call(
        _kvup_kernel,
        grid=(r // tk,),
        in_specs=[pl.BlockSpec((tk, MLA_KV_RANK), lambda i: (i, 0)),
                  pl.BlockSpec((tk, LANES), lambda i: (i, 0)),
                  pl.BlockSpec(wk.shape, lambda i: (0, 0)),
                  pl.BlockSpec(wvt.shape, lambda i: (0, 0))],
        out_specs=[pl.BlockSpec((tk, HEADS * MLA_QK_W), lambda i: (i, 0)),
                   pl.BlockSpec((BRANCH_W, tk), lambda i: (0, i))],
        out_shape=[jax.ShapeDtypeStruct((r, HEADS * MLA_QK_W), BF16),
                   jax.ShapeDtypeStruct((BRANCH_W, r), BF16)],
        compiler_params=_params(("parallel",)),
        name="kvup",
    )(ckv_all, kpe_all, wk, wvt)


def _exp_cols(s, scale):
    e = jnp.exp2((s - jnp.max(s, axis=0, keepdims=True)) * (scale * math.log2(math.e)))
    return e, 1.0 / jnp.sum(e, axis=0, keepdims=True)


def _attn_kernel(qm_ref, km_ref, vmt_ref, qd_ref, kd_ref, vdt_ref, lam_ref, gn_ref, *rest,
                 heads, out_scale, aliased):
    om_ref, od_ref = rest[2:] if aliased else rest
    lam = lam_ref[:, 0:1]
    gn = gn_ref[...]
    lane = lax.broadcasted_iota(jnp.int32, (1, LANES), 1)
    for h in range(heads):
        qk = slice(h * MLA_QK_W, (h + 1) * MLA_QK_W)
        hw = slice(h * HEAD_W, (h + 1) * HEAD_W)
        s = lax.dot_general(km_ref[:, qk], qm_ref[:, qk], NT_DIMS, preferred_element_type=F32)
        e, r = _exp_cols(s, (MLA_NOPE + MLA_ROPE) ** -0.5)
        o_t = jnp.dot(vmt_ref[hw, :], (e * r).astype(BF16), preferred_element_type=F32)
        om_ref[:, hw] = o_t.T.astype(BF16)

        q = qd_ref[:, hw]
        zero = jnp.zeros_like(q)
        k = kd_ref[:, hw]
        s1 = lax.dot_general(k, jnp.where(lane < DF_DH, q, zero), NT_DIMS, preferred_element_type=F32)
        s2 = lax.dot_general(k, jnp.where(lane >= DF_DH, q, zero), NT_DIMS, preferred_element_type=F32)
        e1, r1 = _exp_cols(s1, DF_DH ** -0.5)
        e2, r2 = _exp_cols(s2, DF_DH ** -0.5)
        w = (e1 * r1 - e2 * (lam * r2)).astype(BF16)
        o_t = jnp.dot(vdt_ref[hw, :], w, preferred_element_type=F32)
        od_ref[:, hw] = (_rms_norm(o_t.T, gn) * out_scale).astype(BF16)


def _attn_call(qcat, qd, kcat, vmt, kd, vdt, lam, gn, out_scale, *, nseq, tq, q_tiles, q_row0, tk,
               k_block0, heads, prev=None):
    n = qcat.shape[0]
    qb0 = q_row0 // tq
    qmap = lambda b, g, i: (qb0 + b * q_tiles + i, g)
    kmap = lambda b, g, i: (k_block0 + b, g)
    vmap = lambda b, g, i: (g, k_block0 + b)
    in_specs = [pl.BlockSpec((tq, heads * MLA_QK_W), qmap),
                pl.BlockSpec((tk, heads * MLA_QK_W), kmap),
                pl.BlockSpec((heads * HEAD_W, tk), vmap),
                pl.BlockSpec((tq, heads * HEAD_W), qmap),
                pl.BlockSpec((tk, heads * HEAD_W), kmap),
                pl.BlockSpec((heads * HEAD_W, tk), vmap),
                pl.BlockSpec((1, LANES), lambda b, g, i: (0, 0)),
                pl.BlockSpec((1, LANES), lambda b, g, i: (0, 0))]
    args = [qcat, kcat, vmt, qd, kd, vdt, lam, gn]
    aliases = {}
    if prev is not None:
        in_specs += [pl.BlockSpec(memory_space=pl.ANY), pl.BlockSpec(memory_space=pl.ANY)]
        aliases = {len(args): 0, len(args) + 1: 1}
        args += list(prev)
    out_spec = pl.BlockSpec((tq, heads * HEAD_W), qmap)
    return pl.pallas_call(
        functools.partial(_attn_kernel, heads=heads, out_scale=out_scale, aliased=prev is not None),
        grid=(nseq, HEADS // heads, q_tiles),
        in_specs=in_specs,
        out_specs=[out_spec, out_spec],
        out_shape=[jax.ShapeDtypeStruct((n, BRANCH_W), BF16), jax.ShapeDtypeStruct((n, BRANCH_W), BF16)],
        input_output_aliases=aliases,
        compiler_params=_params(("parallel", "parallel", "arbitrary")),
        name="attn",
    )(*args)


def _merge_kernel(oh_ref, om_ref, od_ref, ga_ref, gb_ref, gc_ref, w_ref, o_ref):
    acc = jax.nn.sigmoid(ga_ref[...]) * jnp.dot(oh_ref[...], w_ref[0], preferred_element_type=F32)
    acc += jax.nn.sigmoid(gb_ref[...]) * jnp.dot(om_ref[...], w_ref[1], preferred_element_type=F32)
    acc += jax.nn.sigmoid(gc_ref[...]) * jnp.dot(od_ref[...], w_ref[2], preferred_element_type=F32)
    o_ref[...] = acc.astype(BF16)


def _merge_call(oh, om, od, y, wbr, layer, tm):
    n = oh.shape[0]
    d = wbr.shape[3]
    tn = 1024
    br = pl.BlockSpec((tm, BRANCH_W), lambda j, i: (i, 0))
    gate = lambda g: pl.BlockSpec((tm, tn), lambda j, i, g=g: (i, (COL_GA + g * d) // tn + j))
    return pl.pallas_call(
        _merge_kernel,
        grid=(d // tn, n // tm),
        in_specs=[br, br, br, gate(0), gate(1), gate(2),
                  pl.BlockSpec((None, 3, BRANCH_W, tn), lambda j, i: (layer, 0, 0, j))],
        out_specs=pl.BlockSpec((tm, tn), lambda j, i: (i, j)),
        out_shape=jax.ShapeDtypeStruct((n, d), BF16),
        compiler_params=_params(("parallel", "parallel")),
        name="merge",
    )(oh, om, od, y, y, y, wbr)


def _outproj_kernel(m_ref, x_ref, mod_ref, w_ref, g_ref, b_ref, x1_ref, h2_ref, h2t_ref, *, alpha):
    gate1 = mod_ref[0, 2:3, :]
    shift2, scale2 = mod_ref[0, 3:4, :], mod_ref[0, 4:5, :]
    y = jnp.dot(m_ref[...], w_ref[...], preferred_element_type=F32)
    x1 = _layer_norm(alpha * x_ref[...] + gate1 * y) * g_ref[...] + b_ref[...]
    x1_ref[...] = x1
    h2 = _layer_norm(x1) * (1.0 + scale2) + shift2
    h2_ref[...] = h2.astype(BF16)
    h2t_ref[...] = h2.T.astype(BF16)


def _outproj_call(merged, x, mod, w, layer, g, b, mod_idx, alpha, tm):
    n, d = x.shape
    row = pl.BlockSpec((tm, d), lambda i: (i, 0))
    vec = pl.BlockSpec((1, d), lambda i: (0, 0))
    return pl.pallas_call(
        functools.partial(_outproj_kernel, alpha=alpha),
        grid=(n // tm,),
        in_specs=[row, row, pl.BlockSpec((1, N_MOD, d), lambda i: (mod_idx(i), 0, 0)),
                  pl.BlockSpec((None, d, d), lambda i: (layer, 0, 0)), vec, vec],
        out_specs=[row, row, pl.BlockSpec((d, tm), lambda i: (0, i))],
        out_shape=[jax.ShapeDtypeStruct((n, d), F32), jax.ShapeDtypeStruct((n, d), BF16),
                   jax.ShapeDtypeStruct((d, n), BF16)],
        compiler_params=_params(("parallel",)),
        name="outproj",
    )(merged, x, mod, w, g, b)


def _peerq_kernel(h_ref, w_ref, sk_ref, st_ref):
    q = jnp.dot(h_ref[...], w_ref[...], preferred_element_type=F32).astype(BF16)
    for hp in range(2 * HEADS):
        st_ref[hp // 2, hp % 2] = lax.dot_general(sk_ref[hp % 2], q[:, hp * LANES:(hp + 1) * LANES], NT_DIMS,
                                                  preferred_element_type=F32)


def _peerq_call(h2, wq, layer, subkeys, tm):
    n, d = h2.shape
    return pl.pallas_call(
        _peerq_kernel,
        grid=(n // tm,),
        in_specs=[pl.BlockSpec((tm, d), lambda i: (i, 0)),
                  pl.BlockSpec((None,) + wq.shape[1:], lambda i: (layer, 0, 0)),
                  pl.BlockSpec(subkeys.shape, lambda i: (0, 0, 0))],
        out_specs=pl.BlockSpec((HEADS, 2, PEER_NKEYS, tm), lambda i: (0, 0, 0, i)),
        out_shape=jax.ShapeDtypeStruct((HEADS, 2, PEER_NKEYS, n), F32),
        compiler_params=_params(("parallel",)),
        name="peerq",
    )(h2, wq, subkeys)


def _top_values(x, count):
    rows = lax.broadcasted_iota(jnp.int32, x.shape, 0).astype(F32)
    vals = []
    for r in range(count):
        m = jnp.max(x, axis=0, keepdims=True)
        vals.append(m)
        if r + 1 < count:
            first = jnp.min(jnp.where(x == m, rows, float(x.shape[0])), axis=0, keepdims=True)
            x = jnp.where(rows == first, -jnp.inf, x)
    return vals


def _topk_kernel(st_ref, e1_ref, e2_ref, th_ref, v_scr, cand_scr):
    row8 = lax.broadcasted_iota(jnp.int32, (8, st_ref.shape[3]), 0)
    for h in range(HEADS):
        s1, s2 = st_ref[h, 0], st_ref[h, 1]
        v1 = _top_values(s1, PEER_TOPK)
        v2 = _top_values(s2, PEER_TOPK)
        for r in range(PEER_TOPK):
            v_scr[0, r:r + 1, :] = v1[r]
            v_scr[1, r:r + 1, :] = v2[r]
        cand_scr[0:16, :] = v1[0] + v_scr[1]
        for a in range(1, 8):
            pair = v1[a] + v_scr[1, 0:8, :]
            cand_scr[8 + 8 * a:16 + 8 * a, :] = jnp.where(row8 < PEER_TOPK // (a + 1), pair, -jnp.inf)
        cand_scr[72:80, :] = v_scr[0, 8:16, :] + v2[0]
        top = _top_values(cand_scr[...], PEER_TOPK)
        z = jnp.exp(top[0] - top[0])
        for r in range(1, PEER_TOPK):
            z = z + jnp.exp(top[r] - top[0])
        e1_ref[h] = jnp.exp(s1 - v1[0]) * (1.0 / z)
        e2_ref[h] = jnp.exp(s2 - v2[0])
        th = jnp.full(s1.shape, jnp.inf, F32)
        for b in range(PEER_TOPK):
            th = jnp.where(s1 + v2[b] >= top[PEER_TOPK - 1], v2[b], th)
        th_ref[h] = th


def _topk_call(st, tm):
    n = st.shape[3]
    out = pl.BlockSpec((HEADS, PEER_NKEYS, tm), lambda i: (0, 0, i))
    shape = jax.ShapeDtypeStruct((HEADS, PEER_NKEYS, n), F32)
    return pl.pallas_call(
        _topk_kernel,
        grid=(n // tm,),
        in_specs=[pl.BlockSpec((HEADS, 2, PEER_NKEYS, tm), lambda i: (0, 0, 0, i))],
        out_specs=[out, out, out],
        out_shape=[shape, shape, shape],
        scratch_shapes=[pltpu.VMEM((2, PEER_TOPK, tm), F32), pltpu.VMEM((80, tm), F32)],
        compiler_params=_params(("parallel",)),
        name="topk",
    )(st)


def _peer_kernel(ht_ref, s2_ref, e1_ref, e2_ref, th_ref, u_ref, vt_ref, ot_ref, ga_scr, *, rows_per_step):
    j = pl.program_id(1)
    nj = pl.num_programs(1) - 1
    slot = j % 2
    tm = ht_ref.shape[1]

    @pl.when(j == 0)
    def _():
        ot_ref[...] = jnp.zeros_like(ot_ref)
        ga_scr[1] = jnp.zeros(ga_scr.shape[1:], BF16)

    ot_ref[...] += jnp.dot(vt_ref[...], ga_scr[1 - slot], preferred_element_type=F32)

    a_t = jnp.dot(u_ref[...], ht_ref[...], preferred_element_type=F32)
    i0 = jnp.minimum(j, nj - 1) * rows_per_step
    for r in range(rows_per_step):
        i = i0 + r
        g = jnp.zeros((PEER_NKEYS, tm), F32)
        for h in range(HEADS):
            keep = s2_ref[h] >= th_ref[h, pl.ds(i, 1), :]
            g = g + jnp.where(keep, e2_ref[h], 0.0) * e1_ref[h, pl.ds(i, 1), :]
        rows = slice(r * PEER_NKEYS, (r + 1) * PEER_NKEYS)
        ga_scr[slot, rows, :] = (g * _gelu(a_t[rows, :])).astype(BF16)


def _peer_call(h2t, st, e1, e2, th, u, vt, layer, tm, rows_per_step):
    d, n = h2t.shape
    te = rows_per_step * PEER_NKEYS
    nj = u.shape[1] // te
    once = pl.Buffered(1)
    tok = pl.BlockSpec((HEADS, PEER_NKEYS, tm), lambda i, j: (0, 0, i), pipeline_mode=once)
    return pl.pallas_call(
        functools.partial(_peer_kernel, rows_per_step=rows_per_step),
        grid=(n // tm, nj + 1),
        in_specs=[pl.BlockSpec((d, tm), lambda i, j: (0, i), pipeline_mode=once),
                  pl.BlockSpec((HEADS, None, PEER_NKEYS, tm), lambda i, j: (0, 1, 0, i), pipeline_mode=once),
                  tok, tok, tok,
                  pl.BlockSpec((None, te, d), lambda i, j: (layer, jnp.minimum(j, nj - 1), 0)),
                  pl.BlockSpec((None, d, te), lambda i, j: (layer, 0, jnp.maximum(j - 1, 0)))],
        out_specs=pl.BlockSpec((d, tm), lambda i, j: (0, i)),
        out_shape=jax.ShapeDtypeStruct((d, n), F32),
        scratch_shapes=[pltpu.VMEM((2, te, tm), BF16)],
        compiler_params=_params(("parallel", "arbitrary")),
        name="peer",
    )(h2t, st, e1, e2, th, u, vt)


def _ln2_kernel(x_ref, yt_ref, mod_ref, g_ref, b_ref, o_ref, *, alpha):
    gate2 = mod_ref[0, 5:6, :]
    o_ref[...] = _layer_norm(alpha * x_ref[...] + gate2 * yt_ref[...].T) * g_ref[...] + b_ref[...]


def _ln2_call(x1, y2t, mod, g, b, mod_idx, alpha, tm):
    n, d = x1.shape
    row = pl.BlockSpec((tm, d), lambda i: (i, 0))
    vec = pl.BlockSpec((1, d), lambda i: (0, 0))
    return pl.pallas_call(
        functools.partial(_ln2_kernel, alpha=alpha),
        grid=(n // tm,),
        in_specs=[row, pl.BlockSpec((d, tm), lambda i: (0, i)),
                  pl.BlockSpec((1, N_MOD, d), lambda i: (mod_idx(i), 0, 0)), vec, vec],
        out_specs=row,
        out_shape=jax.ShapeDtypeStruct((n, d), F32),
        compiler_params=_params(("parallel",)),
        name="ln2",
    )(x1, y2t, mod, g, b)


def _rope_tables(n_ctx, n_lat_seq, lat_seq):
    quarter = 16
    t = jnp.arange(lat_seq)
    rowp = (t // GRID_W).astype(F32)
    colp = (t % GRID_W).astype(F32)
    inv_freq = ROPE_BASE ** (-jnp.arange(quarter, dtype=F32) / quarter)
    ang = jnp.stack([rowp[:, None] * inv_freq, colp[:, None] * inv_freq], axis=1)
    ang = jnp.concatenate([ang, ang], axis=-1).reshape(lat_seq, 64)
    sign = jnp.where((jnp.arange(64) % 32) < 16, -1.0, 1.0).astype(F32)
    cos, sin = jnp.cos(ang), jnp.sin(ang) * sign
    ones, zeros = jnp.ones((lat_seq, 64), F32), jnp.zeros((lat_seq, 64), F32)

    def full(tab, ident):
        lat = jnp.tile(tab, (n_lat_seq, 1))
        return jnp.concatenate([jnp.full((n_ctx, LANES), ident, F32), lat], axis=0)

    cosd = full(jnp.concatenate([cos, cos], axis=1), 1.0)
    sind = full(jnp.concatenate([sin, sin], axis=1), 0.0)
    cosm = full(jnp.concatenate([cos, ones], axis=1), 1.0)
    sinm = full(jnp.concatenate([sin, zeros], axis=1), 0.0)
    return cosd, sind, cosm, sinm


def kernel(x_prompt, x_sample, cache_mla_ckv, cache_mla_kpe, cache_diff_k, cache_diff_v, state_hgrn,
           c, c_ctx, w_ada, b_ada, w_in, hg_lb_logits, hg_norm, mla_q_norm, w_uq, mla_kv_norm, w_ukv,
           df_lambda, df_norm, w_branch, w_out, ln1_g, ln1_b, ln2_g, ln2_b,
           peer_wq, peer_subkeys, peer_u, peer_v):
    nb, seq, d = x_prompt.shape
    db, dseq, _ = x_sample.shape
    depth = w_in.shape[0]
    past = cache_mla_ckv.shape[2]
    n_ctx, n_lat = nb * seq, db * dseq
    n = n_ctx + n_lat
    tk_lat = past + dseq
    tm = 512
    tm_in = 2 * tm if n_ctx % (2 * tm) == 0 and dseq % (2 * tm) == 0 else tm
    tq = 256
    hg_tile = min(dseq, 512)
    assert n_ctx % tm == 0 and dseq % tm == 0 and (db * tk_lat) % seq == 0 and seq % tq == 0
    alpha = (2 * depth) ** 0.25

    def mod_idx(i, tile=tm):
        return jnp.where(i < n_ctx // tile, 0, 1 + (i - n_ctx // tile) // (dseq // tile))

    lb = jnp.cumsum(jax.nn.softmax(hg_lb_logits.astype(F32), axis=0), axis=0)
    lb = lb - lb[0:1]
    lbp = jnp.stack([jnp.log(lb), jnp.log1p(-lb), 1.0 - lb], axis=2)
    lq = df_lambda.astype(F32)
    lam_init = np.array([0.8 - 0.6 * math.exp(-0.3 * l) for l in range(depth)], np.float32)
    lam = jnp.exp(jnp.sum(lq[:, 0] * lq[:, 1], axis=-1)) - jnp.exp(jnp.sum(lq[:, 2] * lq[:, 3], axis=-1)) + lam_init

    kpe0 = 5 * 1024 + MLA_Q_RANK + MLA_KV_RANK
    w_in_p = jnp.concatenate(
        [w_in[:, :, :5 * 1024], w_in[:, :, kpe0 + MLA_ROPE:], w_in[:, :, 5 * 1024:kpe0],
         w_in[:, :, kpe0:kpe0 + MLA_ROPE],
         jnp.zeros((depth, d, IN_W_PAD - w_in.shape[2]), w_in.dtype)], axis=2).astype(BF16)
    wq4 = w_uq.reshape(depth, MLA_Q_RANK, HEADS, MLA_NOPE + MLA_ROPE)
    w_uq_p = jnp.concatenate(
        [wq4, jnp.zeros((depth, MLA_Q_RANK, HEADS, MLA_QK_W - MLA_NOPE - MLA_ROPE), w_uq.dtype)],
        axis=3).reshape(depth, MLA_Q_RANK, HEADS * MLA_QK_W).astype(BF16)
    wkv4 = w_ukv.reshape(depth, MLA_KV_RANK, HEADS, 2 * HEAD_W)
    w_uk = wkv4[..., :HEAD_W].reshape(depth, MLA_KV_RANK, BRANCH_W).astype(BF16)
    w_uvt = jnp.swapaxes(wkv4[..., HEAD_W:].reshape(depth, MLA_KV_RANK, BRANCH_W), 1, 2).astype(BF16)
    w_branch_b, w_out_b, peer_wq_b = w_branch.astype(BF16), w_out.astype(BF16), peer_wq.astype(BF16)
    subkeys_b, peer_u_b = peer_subkeys.astype(BF16), peer_u.astype(BF16)
    peer_vt_b = jnp.swapaxes(peer_v.astype(BF16), 1, 2)

    tri = jnp.stack([jnp.tril(jnp.ones((HG_CHUNK, HG_CHUNK), F32)),
                     jnp.triu(jnp.ones((HG_CHUNK, HG_CHUNK), F32))]).astype(BF16)
    emat = jnp.repeat(jnp.eye(HG_CHUNK, dtype=F32), HEAD_W, axis=0).astype(BF16)
    cosd, sind, cosm, sinm = _rope_tables(n_ctx, db, dseq)

    cvec = jnp.concatenate([c_ctx[None, :], c, jnp.zeros((8 - 1 - db, d), F32)], axis=0)
    mods = _ada_call(cvec, w_ada, b_ada)[:, :1 + db].reshape(depth, 1 + db, N_MOD, d)

    x = jnp.concatenate([x_prompt.reshape(n_ctx, d), x_sample.reshape(n_lat, d)], axis=0)
    zero_state = jnp.zeros((nb, 2, HEADS, HEAD_W, HEAD_W), F32)
    new_ckv, new_kpe, new_dk, new_dv, new_st = [], [], [], [], []
    for l in range(depth):
        mod = mods[l]
        y = _inproj_call(x, mod, w_in_p, l, functools.partial(mod_idx, tile=tm_in), tm_in)

        of, ob, st_ctx = _hgrn_call(y, lbp[l], zero_state, tri, emat, seq=seq, nseq=nb, row0=0,
                                    tile=seq, heads=4)
        of, ob, _ = _hgrn_call(y, lbp[l], state_hgrn[:, l], tri, emat, seq=dseq, nseq=db, row0=n_ctx,
                               tile=hg_tile, heads=4, prev=(of, ob))
        oh = _hfin_call(of, ob, y, hg_norm[l][None, :], tm)

        qcat, ckvn, kper, qd, kdr = _prep_call(y, mla_q_norm[l][None, :], mla_kv_norm[l][None, :], w_uq_p[l],
                                               cosm, sinm, cosd, sind, tm)
        vd = y[:, COL_DV:COL_DV + BRANCH_W]
        lat3 = lambda a: a[n_ctx:].reshape(db, dseq, a.shape[1])
        keys = lambda cache, a: jnp.concatenate(
            [jnp.concatenate([cache, lat3(a)], axis=1).reshape(db * tk_lat, a.shape[1]), a[:n_ctx]],
            axis=0).astype(BF16)
        kpe_cache = jnp.pad(cache_mla_kpe[:, l], ((0, 0), (0, 0), (0, LANES - MLA_ROPE)))
        kcat, vmt = _kvup_call(keys(cache_mla_ckv[:, l], ckvn), keys(kpe_cache, kper), w_uk[l], w_uvt[l], seq)
        kd_all = keys(cache_diff_k[:, l].reshape(db, past, BRANCH_W), kdr)
        vdt = keys(cache_diff_v[:, l].reshape(db, past, BRANCH_W), vd).T
        lam_row = jnp.full((1, LANES), lam[l], F32)
        attn_args = (qcat, qd, kcat, vmt, kd_all, vdt, lam_row, df_norm[l][None, :], float(1.0 - lam_init[l]))
        om, od = _attn_call(*attn_args, nseq=nb, tq=seq, q_tiles=1, q_row0=0, tk=seq,
                            k_block0=db * tk_lat // seq, heads=HEADS)
        om, od = _attn_call(*attn_args, nseq=db, tq=tq, q_tiles=dseq // tq, q_row0=n_ctx,
                            tk=tk_lat, k_block0=0, heads=1, prev=(om, od))

        merged = _merge_call(oh, om, od, y, w_branch_b, l, tm)
        x1, h2, h2t = _outproj_call(merged, x, mod, w_out_b, l, ln1_g[l][None, :], ln1_b[l][None, :],
                               functools.partial(mod_idx, tile=256), alpha, 256)

        st = _peerq_call(h2, peer_wq_b, l, subkeys_b[l], tm)
        e1, e2, th = _topk_call(st, tm)
        y2t = _peer_call(h2t, st, e1, e2, th, peer_u_b, peer_vt_b, l, tm, 8)
        x = _ln2_call(x1, y2t, mod, ln2_g[l][None, :], ln2_b[l][None, :], mod_idx, alpha, tm)

        new_ckv.append(ckvn[:n_ctx].reshape(nb, seq, MLA_KV_RANK))
        new_kpe.append(kper[:n_ctx, :MLA_ROPE].reshape(nb, seq, MLA_ROPE))
        new_dk.append(y[:n_ctx, COL_DK:COL_DK + BRANCH_W].reshape(nb, seq, HEADS, HEAD_W))
        new_dv.append(vd[:n_ctx].reshape(nb, seq, HEADS, HEAD_W))
        new_st.append(st_ctx)

    return (x[:n_ctx].reshape(nb, seq, d), x[n_ctx:].reshape(db, dseq, d),
            jnp.stack(new_ckv, axis=1), jnp.stack(new_kpe, axis=1), jnp.stack(new_dk, axis=1),
            jnp.stack(new_dv, axis=1), jnp.stack(new_st, axis=1))
```

```python
import functools
import math

import jax
import jax.numpy as jnp
import numpy as np
from jax import lax
from jax.experimental import pallas as pl
from jax.experimental.pallas import tpu as pltpu

F32 = jnp.float32
BF16 = jnp.bfloat16

GRID_W = 64
ROPE_BASE = 10000.0
HEADS = 8
HEAD_W = 128
HG_CHUNK = 32
MLA_Q_RANK = 512
MLA_KV_RANK = 256
MLA_NOPE = 128
MLA_ROPE = 64
MLA_QK_W = 256
DF_DH = 64
BRANCH_W = HEADS * HEAD_W
PEER_NKEYS = 128
PEER_TOPK = 16
N_MOD = 6
LANES = 128
VMEM_LIMIT = 52 * 1024 * 1024

COL_HQ, COL_FF, COL_FB, COL_HI, COL_HG = 0, 1024, 2048, 3072, 4096
COL_DQ, COL_DK, COL_DV = 5120, 6144, 7168
COL_GA = 8192
COL_CQ, COL_CKV, COL_KPE = 14336, 14848, 15104
IN_W_PAD = 15360

NT_DIMS = (((1,), (1,)), ((), ()))
TN_DIMS = (((0,), (0,)), ((), ()))


def _params(sem, vmem=VMEM_LIMIT):
    return pltpu.CompilerParams(dimension_semantics=sem, vmem_limit_bytes=vmem)


def _layer_norm(x, eps=1e-5):
    xc = x - jnp.mean(x, axis=-1, keepdims=True)
    return xc * lax.rsqrt(jnp.mean(xc * xc, axis=-1, keepdims=True) + eps)


def _rms_norm(x, g, eps=1e-6):
    return x * lax.rsqrt(jnp.mean(x * x, axis=-1, keepdims=True) + eps) * g


def _silu(x):
    return x * jax.nn.sigmoid(x)


def _gelu(x):
    return 0.5 * x * (1.0 + lax.erf(x * (2.0 ** -0.5)))


def _ada_kernel(c_ref, w_ref, b_ref, o_ref):
    s = _silu(c_ref[...]).astype(BF16)
    o_ref[0] = jnp.dot(s, w_ref[0].astype(BF16), preferred_element_type=F32) + b_ref[0]


def _ada_call(cvec, w_ada, b_ada):
    depth, d, w = w_ada.shape
    tn = 1024
    return pl.pallas_call(
        _ada_kernel,
        grid=(depth, w // tn),
        in_specs=[pl.BlockSpec((8, d), lambda l, j: (0, 0)),
                  pl.BlockSpec((1, d, tn), lambda l, j: (l, 0, j)),
                  pl.BlockSpec((1, 1, tn), lambda l, j: (l, 0, j))],
        out_specs=pl.BlockSpec((1, 8, tn), lambda l, j: (l, 0, j)),
        out_shape=jax.ShapeDtypeStruct((depth, 8, w), F32),
        compiler_params=_params(("parallel", "parallel")),
        name="ada",
    )(cvec, w_ada, b_ada.reshape(depth, 1, w))


def _inproj_kernel(x_ref, mod_ref, w_ref, o_ref, h_scr):
    @pl.when(pl.program_id(1) == 0)
    def _():
        shift, scale = mod_ref[0, 0:1, :], mod_ref[0, 1:2, :]
        h_scr[...] = (_layer_norm(x_ref[...]) * (1.0 + scale) + shift).astype(BF16)

    o_ref[...] = jnp.dot(h_scr[...], w_ref[...], preferred_element_type=F32)


def _inproj_call(x, mod, w, layer, mod_idx, tm):
    n, d = x.shape
    wp = w.shape[2]
    tn = 1024
    return pl.pallas_call(
        _inproj_kernel,
        grid=(n // tm, wp // tn),
        in_specs=[pl.BlockSpec((tm, d), lambda i, j: (i, 0)),
                  pl.BlockSpec((1, N_MOD, d), lambda i, j: (mod_idx(i), 0, 0)),
                  pl.BlockSpec((None, d, tn), lambda i, j: (layer, 0, j))],
        out_specs=pl.BlockSpec((tm, tn), lambda i, j: (i, j)),
        out_shape=jax.ShapeDtypeStruct((n, wp), F32),
        scratch_shapes=[pltpu.VMEM((tm, d), BF16)],
        compiler_params=_params(("parallel", "arbitrary")),
        name="inproj",
    )(x, mod, w)


def _hgrn_chunk_pre(direction, q, z, v, lbp, tri):
    c = HG_CHUNK
    loglb, log1mlb, omlb = lbp[0:1, :], lbp[1:2, :], lbp[2:3, :]
    log_f = jnp.logaddexp(loglb, log1mlb + jax.nn.log_sigmoid(z))
    k = omlb * jax.nn.sigmoid(-z)
    q = _silu(q)

    hi = log_f.astype(BF16)
    r1 = log_f - hi.astype(F32)
    mid = r1.astype(BF16)
    lo = (r1 - mid.astype(F32)).astype(BF16)
    cs = jnp.dot(tri, jnp.concatenate([hi, mid, lo], axis=1), preferred_element_type=F32)
    b = cs[:, 0:LANES] + cs[:, LANES:2 * LANES] + cs[:, 2 * LANES:3 * LANES]
    total = b[c - 1:c, :] if direction == 0 else b[0:1, :]

    qe = (q * jnp.exp(b)).astype(BF16)
    kd = (k * jnp.exp(total - b)).astype(BF16)
    vb = v.astype(BF16)

    row = lax.broadcasted_iota(jnp.int32, (8, LANES), 0)
    cols = []
    for s in range(c):
        pieces = []
        for j in range(c // 8):
            lo_t, hi_t = 8 * j, 8 * j + 7
            if direction == 0:
                dead, full = hi_t < s, lo_t >= s
            else:
                dead, full = lo_t > s, hi_t <= s
            if dead:
                pieces.append(jnp.zeros((8, LANES), F32))
                continue
            val = q[lo_t:lo_t + 8, :] * jnp.exp(b[lo_t:lo_t + 8, :] - b[s:s + 1, :]) * k[s:s + 1, :]
            if not full:
                keep = (row + lo_t >= s) if direction == 0 else (row + lo_t <= s)
                val = jnp.where(keep, val, 0.0)
            pieces.append(val)
        cols.append(jnp.concatenate(pieces, axis=0))
    pcat = jnp.concatenate(cols, axis=1).astype(BF16)
    return qe, kd, vb, total, pcat


def _hgrn_chunk_post(qe, kd, vb, total, att, st):
    o = lax.dot_general(qe, st.astype(BF16), NT_DIMS, preferred_element_type=F32)
    o = o + jnp.dot(att.astype(BF16), vb, preferred_element_type=F32)
    st = st * jnp.exp(total) + lax.dot_general(vb, kd, TN_DIMS, preferred_element_type=F32)
    return o, st


def _hgrn_kernel(qf_ref, ff_ref, vf_ref, qb_ref, fb_ref, vb_ref, lbp_ref, s0_ref, tri_ref, e_ref, *rest,
                 tile, heads, aliased):
    of_ref, ob_ref, sout_ref, st_scr = rest[2:] if aliased else rest
    i = pl.program_id(2)
    c = HG_CHUNK
    n = tile // c

    @pl.when(i == 0)
    def _():
        for hh in range(heads):
            for direction in range(2):
                st_scr[hh, direction] = s0_ref[0, direction, hh].T

    def body(j, carry):
        chains = []
        for hh in range(heads):
            cs = slice(hh * HEAD_W, (hh + 1) * HEAD_W)
            for direction, q_ref, f_ref, v_ref, o_ref in ((0, qf_ref, ff_ref, vf_ref, of_ref),
                                                          (1, qb_ref, fb_ref, vb_ref, ob_ref)):
                r0 = pl.multiple_of((j if direction == 0 else n - 1 - j) * c, c)
                rows = pl.ds(r0, c)
                pre = _hgrn_chunk_pre(direction, q_ref[rows, cs], f_ref[rows, cs], v_ref[rows, cs],
                                      lbp_ref[direction, :, cs], tri_ref[direction])
                chains.append((hh, direction, o_ref, rows, cs, pre))
        att = jnp.dot(jnp.concatenate([ch[5][4] for ch in chains], axis=0), e_ref[...],
                      preferred_element_type=F32)
        for idx, (hh, direction, o_ref, rows, cs, pre) in enumerate(chains):
            o, st = _hgrn_chunk_post(*pre[:4], att[idx * c:(idx + 1) * c, :], st_scr[hh, direction])
            o_ref[rows, cs] = o
            st_scr[hh, direction] = st
        return carry

    lax.fori_loop(0, n, body, 0)

    @pl.when(i == pl.num_programs(2) - 1)
    def _():
        for hh in range(heads):
            for direction in range(2):
                sout_ref[0, direction, hh] = st_scr[hh, direction].T


def _hgrn_call(y, lbp, s0, tri, emat, *, seq, nseq, row0, tile, heads, prev=None):
    n = y.shape[0]
    nt = seq // tile
    w = heads * HEAD_W
    blk0 = row0 // tile

    def col(c0, rev):
        def index(b, g, i):
            t = nt - 1 - i if rev else i
            return (blk0 + b * nt + t, c0 // w + g)
        return pl.BlockSpec((tile, w), index)

    state_spec = pl.BlockSpec((1, 2, heads, HEAD_W, HEAD_W), lambda b, g, i: (b, 0, g, 0, 0))
    out_f = pl.BlockSpec((tile, w), lambda b, g, i: (blk0 + b * nt + i, g))
    out_b = pl.BlockSpec((tile, w), lambda b, g, i: (blk0 + b * nt + nt - 1 - i, g))
    in_specs = [col(COL_HQ, False), col(COL_FF, False), col(COL_HI, False),
                col(COL_HQ, True), col(COL_FB, True), col(COL_HI, True),
                pl.BlockSpec((2, 3, w), lambda b, g, i: (0, 0, g)),
                state_spec,
                pl.BlockSpec((2, HG_CHUNK, HG_CHUNK), lambda b, g, i: (0, 0, 0)),
                pl.BlockSpec((HG_CHUNK * HEAD_W, HG_CHUNK), lambda b, g, i: (0, 0))]
    args = [y, y, y, y, y, y, lbp, s0, tri, emat]
    aliases = {}
    if prev is not None:
        in_specs += [pl.BlockSpec(memory_space=pl.ANY), pl.BlockSpec(memory_space=pl.ANY)]
        aliases = {len(args): 0, len(args) + 1: 1}
        args += list(prev)
    return pl.pallas_call(
        functools.partial(_hgrn_kernel, tile=tile, heads=heads, aliased=prev is not None),
        grid=(nseq, HEADS // heads, nt),
        in_specs=in_specs,
        out_specs=[out_f, out_b, state_spec],
        out_shape=[jax.ShapeDtypeStruct((n, BRANCH_W), F32), jax.ShapeDtypeStruct((n, BRANCH_W), F32),
                   jax.ShapeDtypeStruct((nseq, 2, HEADS, HEAD_W, HEAD_W), F32)],
        scratch_shapes=[pltpu.VMEM((heads, 2, HEAD_W, HEAD_W), F32)],
        input_output_aliases=aliases,
        compiler_params=_params(("parallel", "parallel", "arbitrary")),
        name="hgrn",
    )(*args)


def _hfin_kernel(of_ref, ob_ref, g_ref, gn_ref, o_ref):
    for h in range(HEADS):
        cs = slice(h * HEAD_W, (h + 1) * HEAD_W)
        o = _rms_norm(of_ref[:, cs] + ob_ref[:, cs], gn_ref[:, cs])
        o_ref[:, cs] = (o * _silu(g_ref[:, cs])).astype(BF16)


def _hfin_call(of, ob, y, gnorm, tm):
    n = of.shape[0]
    row = pl.BlockSpec((tm, BRANCH_W), lambda i: (i, 0))
    return pl.pallas_call(
        _hfin_kernel,
        grid=(n // tm,),
        in_specs=[row, row, pl.BlockSpec((tm, BRANCH_W), lambda i: (i, COL_HG // BRANCH_W)),
                  pl.BlockSpec((1, BRANCH_W), lambda i: (0, 0))],
        out_specs=row,
        out_shape=jax.ShapeDtypeStruct((n, BRANCH_W), BF16),
        compiler_params=_params(("parallel",)),
        name="hfin",
    )(of, ob, y, gnorm)


def _rope(x, cos, sin_signed, lo_half):
    rot = jnp.where(lo_half, pltpu.roll(x, LANES - 16, 1), pltpu.roll(x, 16, 1))
    return x * cos + rot * sin_signed


def _prep_kernel(cq_ref, ckv_ref, kpe_ref, dq_ref, dk_ref, dv_ref, qn_ref, kvn_ref, wuq_ref,
                 cosm_ref, sinm_ref, cosd_ref, sind_ref, kd0_ref, vdt0_ref, ckv0_ref, kpe0_ref,
                 qcat_ref, ckvn_ref, kper_ref, qd_ref, kd_ref, vdt_ref, ckvk_ref, kpek_ref):
    del kd0_ref, vdt0_ref, ckv0_ref, kpe0_ref
    lo_half = (lax.broadcasted_iota(jnp.int32, (1, LANES), 1) % 32) < 16
    cq = _rms_norm(cq_ref[...], qn_ref[...]).astype(BF16)
    qm = jnp.dot(cq, wuq_ref[...], preferred_element_type=F32)
    cos_pe, sin_pe = cosm_ref[...], sinm_ref[...]
    for h in range(HEADS):
        c0 = h * MLA_QK_W
        qcat_ref[:, c0:c0 + LANES] = qm[:, c0:c0 + LANES].astype(BF16)
        pe = _rope(qm[:, c0 + LANES:c0 + 2 * LANES], cos_pe, sin_pe, lo_half)
        qcat_ref[:, c0 + LANES:c0 + 2 * LANES] = pe.astype(BF16)
    ckvn = _rms_norm(ckv_ref[...], kvn_ref[...])
    ckvn_ref[...] = ckvn
    ckvk_ref[...] = ckvn.astype(BF16)
    kper = _rope(kpe_ref[...], cos_pe, sin_pe, lo_half)
    kper_ref[...] = kper
    kpek_ref[...] = kper.astype(BF16)
    cos_d, sin_d = cosd_ref[...], sind_ref[...]
    for h in range(HEADS):
        cs = slice(h * LANES, (h + 1) * LANES)
        qd_ref[:, cs] = _rope(dq_ref[:, cs], cos_d, sin_d, lo_half).astype(BF16)
        kd_ref[:, cs] = _rope(dk_ref[:, cs], cos_d, sin_d, lo_half).astype(BF16)
        vdt_ref[cs, :] = dv_ref[:, cs].T.astype(BF16)


def _prep_call(y, qn, kvn, wuq, cosm, sinm, cosd, sind, key_bufs, key_blk, tm):
    n = y.shape[0]
    row = lambda w, c0: pl.BlockSpec((tm, w), lambda i, c0=c0, w=w: (i, c0 // w))
    full = lambda a: pl.BlockSpec(a.shape, lambda i: (0, 0))
    tab = pl.BlockSpec((tm, LANES), lambda i: (i, 0))
    out = lambda w: pl.BlockSpec((tm, w), lambda i: (i, 0))
    keyrow = lambda w: pl.BlockSpec((tm, w), lambda i: (key_blk(i), 0))
    any_spec = pl.BlockSpec(memory_space=pl.ANY)
    n_in = 13
    return pl.pallas_call(
        _prep_kernel,
        grid=(n // tm,),
        in_specs=[row(MLA_Q_RANK, COL_CQ), row(MLA_KV_RANK, COL_CKV), row(LANES, COL_KPE),
                  row(BRANCH_W, COL_DQ), row(BRANCH_W, COL_DK), row(BRANCH_W, COL_DV),
                  full(qn), full(kvn), full(wuq), tab, tab, tab, tab,
                  any_spec, any_spec, any_spec, any_spec],
        out_specs=[out(HEADS * MLA_QK_W), out(MLA_KV_RANK), out(LANES), out(BRANCH_W),
                   keyrow(BRANCH_W), pl.BlockSpec((BRANCH_W, tm), lambda i: (0, key_blk(i))),
                   keyrow(MLA_KV_RANK), keyrow(LANES)],
        out_shape=[jax.ShapeDtypeStruct((n, HEADS * MLA_QK_W), BF16),
                   jax.ShapeDtypeStruct((n, MLA_KV_RANK), F32),
                   jax.ShapeDtypeStruct((n, LANES), F32),
                   jax.ShapeDtypeStruct((n, BRANCH_W), BF16)]
                  + [jax.ShapeDtypeStruct(b.shape, b.dtype) for b in key_bufs],
        input_output_aliases={n_in + k: 4 + k for k in range(4)},
        compiler_params=_params(("parallel",)),
        name="prep",
    )(y, y, y, y, y, y, qn, kvn, wuq, cosm, sinm, cosd, sind, *key_bufs)


def _kvup_kernel(ckv_ref, kpe_ref, wk_ref, wvt_ref, kcat_ref, vt_ref):
    ckv = ckv_ref[...]
    kn = jnp.dot(ckv, wk_ref[...], preferred_element_type=F32)
    kpe = kpe_ref[...]
    for h in range(HEADS):
        c0 = h * MLA_QK_W
        kcat_ref[:, c0:c0 + LANES] = kn[:, h * LANES:(h + 1) * LANES].astype(BF16)
        kcat_ref[:, c0 + LANES:c0 + 2 * LANES] = kpe
    vt_ref[...] = lax.dot_general(wvt_ref[...], ckv, NT_DIMS, preferred_element_type=F32).astype(BF16)


def _kvup_call(ckv_all, kpe_all, wk, wvt, tk):
    r = ckv_all.shape[0]
    return pl.pallas_call(
        _kvup_kernel,
        grid=(r // tk,),
        in_specs=[pl.BlockSpec((tk, MLA_KV_RANK), lambda i: (i, 0)),
                  pl.BlockSpec((tk, LANES), lambda i: (i, 0)),
                  pl.BlockSpec(wk.shape, lambda i: (0, 0)),
                  pl.BlockSpec(wvt.shape, lambda i: (0, 0))],
        out_specs=[pl.BlockSpec((tk, HEADS * MLA_QK_W), lambda i: (i, 0)),
                   pl.BlockSpec((BRANCH_W, tk), lambda i: (0, i))],
        out_shape=[jax.ShapeDtypeStruct((r, HEADS * MLA_QK_W), BF16),
                   jax.ShapeDtypeStruct((BRANCH_W, r), BF16)],
        compiler_params=_params(("parallel",)),
        name="kvup",
    )(ckv_all, kpe_all, wk, wvt)


def _exp_cols(s, scale):
    e = jnp.exp2((s - jnp.max(s, axis=0, keepdims=True)) * (scale * math.log2(math.e)))
    return e, 1.0 / jnp.sum(e, axis=0, keepdims=True)


def _attn_kernel(qm_ref, km_ref, vmt_ref, qd_ref, kd_ref, vdt_ref, lam_ref, gn_ref, *rest,
                 heads, out_scale, aliased):
    om_ref, od_ref = rest[2:] if aliased else rest
    lam = lam_ref[:, 0:1]
    gn = gn_ref[...]
    lane = lax.broadcasted_iota(jnp.int32, (1, LANES), 1)
    for h in range(heads):
        qk = slice(h * MLA_QK_W, (h + 1) * MLA_QK_W)
        hw = slice(h * HEAD_W, (h + 1) * HEAD_W)
        s = lax.dot_general(km_ref[:, qk], qm_ref[:, qk], NT_DIMS, preferred_element_type=F32)
        e, r = _exp_cols(s, (MLA_NOPE + MLA_ROPE) ** -0.5)
        o_t = jnp.dot(vmt_ref[hw, :], (e * r).astype(BF16), preferred_element_type=F32)
        om_ref[:, hw] = o_t.T.astype(BF16)

        q = qd_ref[:, hw]
        zero = jnp.zeros_like(q)
        k = kd_ref[:, hw]
        s1 = lax.dot_general(k, jnp.where(lane < DF_DH, q, zero), NT_DIMS, preferred_element_type=F32)
        s2 = lax.dot_general(k, jnp.where(lane >= DF_DH, q, zero), NT_DIMS, preferred_element_type=F32)
        e1, r1 = _exp_cols(s1, DF_DH ** -0.5)
        e2, r2 = _exp_cols(s2, DF_DH ** -0.5)
        w = (e1 * r1 - e2 * (lam * r2)).astype(BF16)
        o_t = jnp.dot(vdt_ref[hw, :], w, preferred_element_type=F32)
        od_ref[:, hw] = (_rms_norm(o_t.T, gn) * out_scale).astype(BF16)


def _attn_call(qcat, qd, kcat, vmt, kd, vdt, lam, gn, out_scale, *, nseq, tq, q_tiles, q_row0, tk,
               k_block0, heads, prev=None):
    n = qcat.shape[0]
    qb0 = q_row0 // tq
    qmap = lambda b, g, i: (qb0 + b * q_tiles + i, g)
    kmap = lambda b, g, i: (k_block0 + b, g)
    vmap = lambda b, g, i: (g, k_block0 + b)
    in_specs = [pl.BlockSpec((tq, heads * MLA_QK_W), qmap),
                pl.BlockSpec((tk, heads * MLA_QK_W), kmap),
                pl.BlockSpec((heads * HEAD_W, tk), vmap),
                pl.BlockSpec((tq, heads * HEAD_W), qmap),
                pl.BlockSpec((tk, heads * HEAD_W), kmap),
                pl.BlockSpec((heads * HEAD_W, tk), vmap),
                pl.BlockSpec((1, LANES), lambda b, g, i: (0, 0)),
                pl.BlockSpec((1, LANES), lambda b, g, i: (0, 0))]
    args = [qcat, kcat, vmt, qd, kd, vdt, lam, gn]
    aliases = {}
    if prev is not None:
        in_specs += [pl.BlockSpec(memory_space=pl.ANY), pl.BlockSpec(memory_space=pl.ANY)]
        aliases = {len(args): 0, len(args) + 1: 1}
        args += list(prev)
    out_spec = pl.BlockSpec((tq, heads * HEAD_W), qmap)
    return pl.pallas_call(
        functools.partial(_attn_kernel, heads=heads, out_scale=out_scale, aliased=prev is not None),
        grid=(nseq, HEADS // heads, q_tiles),
        in_specs=in_specs,
        out_specs=[out_spec, out_spec],
        out_shape=[jax.ShapeDtypeStruct((n, BRANCH_W), BF16), jax.ShapeDtypeStruct((n, BRANCH_W), BF16)],
        input_output_aliases=aliases,
        compiler_params=_params(("parallel", "parallel", "arbitrary")),
        name="attn",
    )(*args)


def _merge_kernel(oh_ref, om_ref, od_ref, ga_ref, gb_ref, gc_ref, w_ref, o_ref):
    acc = jax.nn.sigmoid(ga_ref[...]) * jnp.dot(oh_ref[...], w_ref[0], preferred_element_type=F32)
    acc += jax.nn.sigmoid(gb_ref[...]) * jnp.dot(om_ref[...], w_ref[1], preferred_element_type=F32)
    acc += jax.nn.sigmoid(gc_ref[...]) * jnp.dot(od_ref[...], w_ref[2], preferred_element_type=F32)
    o_ref[...] = acc.astype(BF16)


def _merge_call(oh, om, od, y, wbr, layer, tm):
    n = oh.shape[0]
    d = wbr.shape[3]
    tn = 1024
    br = pl.BlockSpec((tm, BRANCH_W), lambda j, i: (i, 0))
    gate = lambda g: pl.BlockSpec((tm, tn), lambda j, i, g=g: (i, (COL_GA + g * d) // tn + j))
    return pl.pallas_call(
        _merge_kernel,
        grid=(d // tn, n // tm),
        in_specs=[br, br, br, gate(0), gate(1), gate(2),
                  pl.BlockSpec((None, 3, BRANCH_W, tn), lambda j, i: (layer, 0, 0, j))],
        out_specs=pl.BlockSpec((tm, tn), lambda j, i: (i, j)),
        out_shape=jax.ShapeDtypeStruct((n, d), BF16),
        compiler_params=_params(("parallel", "parallel")),
        name="merge",
    )(oh, om, od, y, y, y, wbr)


def _outproj_kernel(m_ref, x_ref, mod_ref, w_ref, g_ref, b_ref, x1_ref, h2_ref, h2t_ref, *, alpha):
    gate1 = mod_ref[0, 2:3, :]
    shift2, scale2 = mod_ref[0, 3:4, :], mod_ref[0, 4:5, :]
    y = jnp.dot(m_ref[...], w_ref[...], preferred_element_type=F32)
    x1 = _layer_norm(alpha * x_ref[...] + gate1 * y) * g_ref[...] + b_ref[...]
    x1_ref[...] = x1
    h2 = _layer_norm(x1) * (1.0 + scale2) + shift2
    h2_ref[...] = h2.astype(BF16)
    h2t_ref[...] = h2.T.astype(BF16)


def _outproj_call(merged, x, mod, w, layer, g, b, mod_idx, alpha, tm):
    n, d = x.shape
    row = pl.BlockSpec((tm, d), lambda i: (i, 0))
    vec = pl.BlockSpec((1, d), lambda i: (0, 0))
    return pl.pallas_call(
        functools.partial(_outproj_kernel, alpha=alpha),
        grid=(n // tm,),
        in_specs=[row, row, pl.BlockSpec((1, N_MOD, d), lambda i: (mod_idx(i), 0, 0)),
                  pl.BlockSpec((None, d, d), lambda i: (layer, 0, 0)), vec, vec],
        out_specs=[row, row, pl.BlockSpec((d, tm), lambda i: (0, i))],
        out_shape=[jax.ShapeDtypeStruct((n, d), F32), jax.ShapeDtypeStruct((n, d), BF16),
                   jax.ShapeDtypeStruct((d, n), BF16)],
        compiler_params=_params(("parallel",)),
        name="outproj",
    )(merged, x, mod, w, g, b)


def _peerq_kernel(h_ref, w_ref, sk_ref, st_ref):
    q = jnp.dot(h_ref[...], w_ref[...], preferred_element_type=F32).astype(BF16)
    for hp in range(2 * HEADS):
        st_ref[hp // 2, hp % 2] = lax.dot_general(sk_ref[hp % 2], q[:, hp * LANES:(hp + 1) * LANES], NT_DIMS,
                                                  preferred_element_type=F32)


def _peerq_call(h2, wq, layer, subkeys, tm):
    n, d = h2.shape
    return pl.pallas_call(
        _peerq_kernel,
        grid=(n // tm,),
        in_specs=[pl.BlockSpec((tm, d), lambda i: (i, 0)),
                  pl.BlockSpec((None,) + wq.shape[1:], lambda i: (layer, 0, 0)),
                  pl.BlockSpec(subkeys.shape, lambda i: (0, 0, 0))],
        out_specs=pl.BlockSpec((HEADS, 2, PEER_NKEYS, tm), lambda i: (0, 0, 0, i)),
        out_shape=jax.ShapeDtypeStruct((HEADS, 2, PEER_NKEYS, n), F32),
        compiler_params=_params(("parallel",)),
        name="peerq",
    )(h2, wq, subkeys)


def _top_values(x, count):
    rows = lax.broadcasted_iota(jnp.int32, x.shape, 0).astype(F32)
    vals = []
    for r in range(count):
        m = jnp.max(x, axis=0, keepdims=True)
        vals.append(m)
        if r + 1 < count:
            first = jnp.min(jnp.where(x == m, rows, float(x.shape[0])), axis=0, keepdims=True)
            x = jnp.where(rows == first, -jnp.inf, x)
    return vals


def _topk_kernel(st_ref, e1_ref, e2_ref, th_ref, v_scr, cand_scr):
    row8 = lax.broadcasted_iota(jnp.int32, (8, st_ref.shape[3]), 0)
    for h in range(HEADS):
        s1, s2 = st_ref[h, 0], st_ref[h, 1]
        v1 = _top_values(s1, PEER_TOPK)
        v2 = _top_values(s2, PEER_TOPK)
        for r in range(PEER_TOPK):
            v_scr[0, r:r + 1, :] = v1[r]
            v_scr[1, r:r + 1, :] = v2[r]
        cand_scr[0:16, :] = v1[0] + v_scr[1]
        for a in range(1, 8):
            pair = v1[a] + v_scr[1, 0:8, :]
            cand_scr[8 + 8 * a:16 + 8 * a, :] = jnp.where(row8 < PEER_TOPK // (a + 1), pair, -jnp.inf)
        cand_scr[72:80, :] = v_scr[0, 8:16, :] + v2[0]
        top = _top_values(cand_scr[...], PEER_TOPK)
        z = jnp.exp(top[0] - top[0])
        for r in range(1, PEER_TOPK):
            z = z + jnp.exp(top[r] - top[0])
        e1_ref[h] = jnp.exp(s1 - v1[0]) * (1.0 / z)
        e2_ref[h] = jnp.exp(s2 - v2[0])
        th = jnp.full(s1.shape, jnp.inf, F32)
        for b in range(PEER_TOPK):
            th = jnp.where(s1 + v2[b] >= top[PEER_TOPK - 1], v2[b], th)
        th_ref[h] = th


def _topk_call(st, tm):
    n = st.shape[3]
    out = pl.BlockSpec((HEADS, PEER_NKEYS, tm), lambda i: (0, 0, i))
    shape = jax.ShapeDtypeStruct((HEADS, PEER_NKEYS, n), F32)
    return pl.pallas_call(
        _topk_kernel,
        grid=(n // tm,),
        in_specs=[pl.BlockSpec((HEADS, 2, PEER_NKEYS, tm), lambda i: (0, 0, 0, i))],
        out_specs=[out, out, out],
        out_shape=[shape, shape, shape],
        scratch_shapes=[pltpu.VMEM((2, PEER_TOPK, tm), F32), pltpu.VMEM((80, tm), F32)],
        compiler_params=_params(("parallel",)),
        name="topk",
    )(st)


def _peer_kernel(ht_ref, s2_ref, e1_ref, e2_ref, th_ref, u_ref, vt_ref, ot_ref, ga_scr, *, rows_per_step):
    j = pl.program_id(1)
    nj = pl.num_programs(1) - 1
    slot = j % 2
    tm = ht_ref.shape[1]

    @pl.when(j == 0)
    def _():
        ot_ref[...] = jnp.zeros_like(ot_ref)
        ga_scr[1] = jnp.zeros(ga_scr.shape[1:], BF16)

    ot_ref[...] += jnp.dot(vt_ref[...], ga_scr[1 - slot], preferred_element_type=F32)

    a_t = jnp.dot(u_ref[...], ht_ref[...], preferred_element_type=F32)
    i0 = jnp.minimum(j, nj - 1) * rows_per_step
    for r in range(rows_per_step):
        i = i0 + r
        g = jnp.zeros((PEER_NKEYS, tm), F32)
        for h in range(HEADS):
            keep = s2_ref[h] >= th_ref[h, pl.ds(i, 1), :]
            g = g + jnp.where(keep, e2_ref[h], 0.0) * e1_ref[h, pl.ds(i, 1), :]
        rows = slice(r * PEER_NKEYS, (r + 1) * PEER_NKEYS)
        ga_scr[slot, rows, :] = (g * _gelu(a_t[rows, :])).astype(BF16)


def _peer_call(h2t, st, e1, e2, th, u, vt, layer, tm, rows_per_step):
    d, n = h2t.shape
    te = rows_per_step * PEER_NKEYS
    nj = u.shape[1] // te
    once = pl.Buffered(1)
    tok = pl.BlockSpec((HEADS, PEER_NKEYS, tm), lambda i, j: (0, 0, i), pipeline_mode=once)
    return pl.pallas_call(
        functools.partial(_peer_kernel, rows_per_step=rows_per_step),
        grid=(n // tm, nj + 1),
        in_specs=[pl.BlockSpec((d, tm), lambda i, j: (0, i), pipeline_mode=once),
                  pl.BlockSpec((HEADS, None, PEER_NKEYS, tm), lambda i, j: (0, 1, 0, i), pipeline_mode=once),
                  tok, tok, tok,
                  pl.BlockSpec((None, te, d), lambda i, j: (layer, jnp.minimum(j, nj - 1), 0)),
                  pl.BlockSpec((None, d, te), lambda i, j: (layer, 0, jnp.maximum(j - 1, 0)))],
        out_specs=pl.BlockSpec((d, tm), lambda i, j: (0, i)),
        out_shape=jax.ShapeDtypeStruct((d, n), F32),
        scratch_shapes=[pltpu.VMEM((2, te, tm), BF16)],
        compiler_params=_params(("parallel", "arbitrary")),
        name="peer",
    )(h2t, st, e1, e2, th, u, vt)


def _ln2_kernel(x_ref, yt_ref, mod_ref, g_ref, b_ref, o_ref, *, alpha):
    gate2 = mod_ref[0, 5:6, :]
    o_ref[...] = _layer_norm(alpha * x_ref[...] + gate2 * yt_ref[...].T) * g_ref[...] + b_ref[...]


def _ln2_call(x1, y2t, mod, g, b, mod_idx, alpha, tm):
    n, d = x1.shape
    row = pl.BlockSpec((tm, d), lambda i: (i, 0))
    vec = pl.BlockSpec((1, d), lambda i: (0, 0))
    return pl.pallas_call(
        functools.partial(_ln2_kernel, alpha=alpha),
        grid=(n // tm,),
        in_specs=[row, pl.BlockSpec((d, tm), lambda i: (0, i)),
                  pl.BlockSpec((1, N_MOD, d), lambda i: (mod_idx(i), 0, 0)), vec, vec],
        out_specs=row,
        out_shape=jax.ShapeDtypeStruct((n, d), F32),
        compiler_params=_params(("parallel",)),
        name="ln2",
    )(x1, y2t, mod, g, b)


def _rope_tables(n_ctx, n_lat_seq, lat_seq):
    quarter = 16
    t = jnp.arange(lat_seq)
    rowp = (t // GRID_W).astype(F32)
    colp = (t % GRID_W).astype(F32)
    inv_freq = ROPE_BASE ** (-jnp.arange(quarter, dtype=F32) / quarter)
    ang = jnp.stack([rowp[:, None] * inv_freq, colp[:, None] * inv_freq], axis=1)
    ang = jnp.concatenate([ang, ang], axis=-1).reshape(lat_seq, 64)
    sign = jnp.where((jnp.arange(64) % 32) < 16, -1.0, 1.0).astype(F32)
    cos, sin = jnp.cos(ang), jnp.sin(ang) * sign
    ones, zeros = jnp.ones((lat_seq, 64), F32), jnp.zeros((lat_seq, 64), F32)

    def full(tab, ident):
        lat = jnp.tile(tab, (n_lat_seq, 1))
        return jnp.concatenate([jnp.full((n_ctx, LANES), ident, F32), lat], axis=0)

    cosd = full(jnp.concatenate([cos, cos], axis=1), 1.0)
    sind = full(jnp.concatenate([sin, sin], axis=1), 0.0)
    cosm = full(jnp.concatenate([cos, ones], axis=1), 1.0)
    sinm = full(jnp.concatenate([sin, zeros], axis=1), 0.0)
    return cosd, sind, cosm, sinm


def kernel(x_prompt, x_sample, cache_mla_ckv, cache_mla_kpe, cache_diff_k, cache_diff_v, state_hgrn,
           c, c_ctx, w_ada, b_ada, w_in, hg_lb_logits, hg_norm, mla_q_norm, w_uq, mla_kv_norm, w_ukv,
           df_lambda, df_norm, w_branch, w_out, ln1_g, ln1_b, ln2_g, ln2_b,
           peer_wq, peer_subkeys, peer_u, peer_v):
    nb, seq, d = x_prompt.shape
    db, dseq, _ = x_sample.shape
    depth = w_in.shape[0]
    past = cache_mla_ckv.shape[2]
    n_ctx, n_lat = nb * seq, db * dseq
    n = n_ctx + n_lat
    tk_lat = past + dseq
    tm = 512
    tm_in = 2 * tm if n_ctx % (2 * tm) == 0 and dseq % (2 * tm) == 0 else tm
    tq = 256
    hg_tile = min(dseq, 512)
    assert n_ctx % tm == 0 and dseq % tm == 0 and (db * tk_lat) % seq == 0 and seq % tq == 0
    alpha = (2 * depth) ** 0.25

    def mod_idx(i, tile=tm):
        return jnp.where(i < n_ctx // tile, 0, 1 + (i - n_ctx // tile) // (dseq // tile))

    assert past % tm == 0

    def key_blk(i):
        j = i - n_ctx // tm
        lat = (j // (dseq // tm)) * (tk_lat // tm) + past // tm + j % (dseq // tm)
        return jnp.where(i < n_ctx // tm, db * tk_lat // tm + i, lat)

    lb = jnp.cumsum(jax.nn.softmax(hg_lb_logits.astype(F32), axis=0), axis=0)
    lb = lb - lb[0:1]
    lbp = jnp.stack([jnp.log(lb), jnp.log1p(-lb), 1.0 - lb], axis=2)
    lq = df_lambda.astype(F32)
    lam_init = np.array([0.8 - 0.6 * math.exp(-0.3 * l) for l in range(depth)], np.float32)
    lam = jnp.exp(jnp.sum(lq[:, 0] * lq[:, 1], axis=-1)) - jnp.exp(jnp.sum(lq[:, 2] * lq[:, 3], axis=-1)) + lam_init

    kpe0 = 5 * 1024 + MLA_Q_RANK + MLA_KV_RANK
    w_in_p = jnp.concatenate(
        [w_in[:, :, :5 * 1024], w_in[:, :, kpe0 + MLA_ROPE:], w_in[:, :, 5 * 1024:kpe0],
         w_in[:, :, kpe0:kpe0 + MLA_ROPE],
         jnp.zeros((depth, d, IN_W_PAD - w_in.shape[2]), w_in.dtype)], axis=2).astype(BF16)
    wq4 = w_uq.reshape(depth, MLA_Q_RANK, HEADS, MLA_NOPE + MLA_ROPE)
    w_uq_p = jnp.concatenate(
        [wq4, jnp.zeros((depth, MLA_Q_RANK, HEADS, MLA_QK_W - MLA_NOPE - MLA_ROPE), w_uq.dtype)],
        axis=3).reshape(depth, MLA_Q_RANK, HEADS * MLA_QK_W).astype(BF16)
    wkv4 = w_ukv.reshape(depth, MLA_KV_RANK, HEADS, 2 * HEAD_W)
    w_uk = wkv4[..., :HEAD_W].reshape(depth, MLA_KV_RANK, BRANCH_W).astype(BF16)
    w_uvt = jnp.swapaxes(wkv4[..., HEAD_W:].reshape(depth, MLA_KV_RANK, BRANCH_W), 1, 2).astype(BF16)
    w_branch_b, w_out_b, peer_wq_b = w_branch.astype(BF16), w_out.astype(BF16), peer_wq.astype(BF16)
    subkeys_b, peer_u_b = peer_subkeys.astype(BF16), peer_u.astype(BF16)
    peer_vt_b = jnp.swapaxes(peer_v.astype(BF16), 1, 2)

    tri = jnp.stack([jnp.tril(jnp.ones((HG_CHUNK, HG_CHUNK), F32)),
                     jnp.triu(jnp.ones((HG_CHUNK, HG_CHUNK), F32))]).astype(BF16)
    emat = jnp.repeat(jnp.eye(HG_CHUNK, dtype=F32), HEAD_W, axis=0).astype(BF16)
    cosd, sind, cosm, sinm = _rope_tables(n_ctx, db, dseq)

    cvec = jnp.concatenate([c_ctx[None, :], c, jnp.zeros((8 - 1 - db, d), F32)], axis=0)
    mods = _ada_call(cvec, w_ada, b_ada)[:, :1 + db].reshape(depth, 1 + db, N_MOD, d)

    x = jnp.concatenate([x_prompt.reshape(n_ctx, d), x_sample.reshape(n_lat, d)], axis=0)
    zero_state = jnp.zeros((nb, 2, HEADS, HEAD_W, HEAD_W), F32)
    new_ckv, new_kpe, new_dk, new_dv, new_st = [], [], [], [], []
    for l in range(depth):
        mod = mods[l]
        y = _inproj_call(x, mod, w_in_p, l, functools.partial(mod_idx, tile=tm_in), tm_in)

        of, ob, st_ctx = _hgrn_call(y, lbp[l], zero_state, tri, emat, seq=seq, nseq=nb, row0=0,
                                    tile=seq, heads=4)
        of, ob, _ = _hgrn_call(y, lbp[l], state_hgrn[:, l], tri, emat, seq=dseq, nseq=db, row0=n_ctx,
                               tile=hg_tile, heads=4, prev=(of, ob))
        oh = _hfin_call(of, ob, y, hg_norm[l][None, :], tm)

        seed = lambda cache: jnp.pad(jnp.pad(cache.astype(BF16), ((0, 0), (0, dseq), (0, 0)))
                                     .reshape(db * tk_lat, cache.shape[2]), ((0, n_ctx), (0, 0)))
        key_bufs = (seed(cache_diff_k[:, l].reshape(db, past, BRANCH_W)),
                    seed(cache_diff_v[:, l].reshape(db, past, BRANCH_W)).T,
                    seed(cache_mla_ckv[:, l]),
                    seed(jnp.pad(cache_mla_kpe[:, l], ((0, 0), (0, 0), (0, LANES - MLA_ROPE)))))
        qcat, ckvn, kper, qd, kd_all, vdt, ckv_all, kpe_all = _prep_call(
            y, mla_q_norm[l][None, :], mla_kv_norm[l][None, :], w_uq_p[l], cosm, sinm, cosd, sind,
            key_bufs, key_blk, tm)
        kcat, vmt = _kvup_call(ckv_all, kpe_all, w_uk[l], w_uvt[l], seq)
        lam_row = jnp.full((1, LANES), lam[l], F32)
        attn_args = (qcat, qd, kcat, vmt, kd_all, vdt, lam_row, df_norm[l][None, :], float(1.0 - lam_init[l]))
        om, od = _attn_call(*attn_args, nseq=nb, tq=seq, q_tiles=1, q_row0=0, tk=seq,
                            k_block0=db * tk_lat // seq, heads=HEADS)
        om, od = _attn_call(*attn_args, nseq=db, tq=tq, q_tiles=dseq // tq, q_row0=n_ctx,
                            tk=tk_lat, k_block0=0, heads=1, prev=(om, od))

        merged = _merge_call(oh, om, od, y, w_branch_b, l, tm)
        x1, h2, h2t = _outproj_call(merged, x, mod, w_out_b, l, ln1_g[l][None, :], ln1_b[l][None, :],
                               functools.partial(mod_idx, tile=256), alpha, 256)

        st = _peerq_call(h2, peer_wq_b, l, subkeys_b[l], tm)
        e1, e2, th = _topk_call(st, tm)
        y2t = _peer_call(h2t, st, e1, e2, th, peer_u_b, peer_vt_b, l, tm, 8)
        x = _ln2_call(x1, y2t, mod, ln2_g[l][None, :], ln2_b[l][None, :], mod_idx, alpha, tm)

        new_ckv.append(ckvn[:n_ctx].reshape(nb, seq, MLA_KV_RANK))
        new_kpe.append(kper[:n_ctx, :MLA_ROPE].reshape(nb, seq, MLA_ROPE))
        new_dk.append(y[:n_ctx, COL_DK:COL_DK + BRANCH_W].reshape(nb, seq, HEADS, HEAD_W))
        new_dv.append(y[:n_ctx, COL_DV:COL_DV + BRANCH_W].reshape(nb, seq, HEADS, HEAD_W))
        new_st.append(st_ctx)

    return (x[:n_ctx].reshape(nb, seq, d), x[n_ctx:].reshape(db, dseq, d),
            jnp.stack(new_ckv, axis=1), jnp.stack(new_kpe, axis=1), jnp.stack(new_dk, axis=1),
            jnp.stack(new_dv, axis=1), jnp.stack(new_st, axis=1))
```

```python
import functools
import math

import jax
import jax.numpy as jnp
import numpy as np
from jax import lax
from jax.experimental import pallas as pl
from jax.experimental.pallas import tpu as pltpu

F32 = jnp.float32
BF16 = jnp.bfloat16

GRID_W = 64
ROPE_BASE = 10000.0
HEADS = 8
HEAD_W = 128
HG_CHUNK = 32
MLA_Q_RANK = 512
MLA_KV_RANK = 256
MLA_NOPE = 128
MLA_ROPE = 64
MLA_QK_W = 256
DF_DH = 64
BRANCH_W = HEADS * HEAD_W
PEER_NKEYS = 128
PEER_TOPK = 16
N_MOD = 6
LANES = 128
VMEM_LIMIT = 52 * 1024 * 1024

COL_HQ, COL_FF, COL_FB, COL_HI, COL_HG = 0, 1024, 2048, 3072, 4096
COL_DQ, COL_DK, COL_DV = 5120, 6144, 7168
COL_GA = 8192
COL_CQ, COL_CKV, COL_KPE = 14336, 14848, 15104
IN_W_PAD = 15360

NT_DIMS = (((1,), (1,)), ((), ()))
TN_DIMS = (((0,), (0,)), ((), ()))


def _params(sem, vmem=VMEM_LIMIT):
    return pltpu.CompilerParams(dimension_semantics=sem, vmem_limit_bytes=vmem)


def _layer_norm(x, eps=1e-5):
    xc = x - jnp.mean(x, axis=-1, keepdims=True)
    return xc * lax.rsqrt(jnp.mean(xc * xc, axis=-1, keepdims=True) + eps)


def _rms_norm(x, g, eps=1e-6):
    return x * lax.rsqrt(jnp.mean(x * x, axis=-1, keepdims=True) + eps) * g


def _silu(x):
    return x * jax.nn.sigmoid(x)


def _gelu(x):
    return 0.5 * x * (1.0 + lax.erf(x * (2.0 ** -0.5)))


def _ada_kernel(c_ref, w_ref, b_ref, o_ref):
    s = _silu(c_ref[...]).astype(BF16)
    o_ref[0] = jnp.dot(s, w_ref[0].astype(BF16), preferred_element_type=F32) + b_ref[0]


def _ada_call(cvec, w_ada, b_ada):
    depth, d, w = w_ada.shape
    tn = 1024
    return pl.pallas_call(
        _ada_kernel,
        grid=(depth, w // tn),
        in_specs=[pl.BlockSpec((8, d), lambda l, j: (0, 0)),
                  pl.BlockSpec((1, d, tn), lambda l, j: (l, 0, j)),
                  pl.BlockSpec((1, 1, tn), lambda l, j: (l, 0, j))],
        out_specs=pl.BlockSpec((1, 8, tn), lambda l, j: (l, 0, j)),
        out_shape=jax.ShapeDtypeStruct((depth, 8, w), F32),
        compiler_params=_params(("parallel", "parallel")),
        name="ada",
    )(cvec, w_ada, b_ada.reshape(depth, 1, w))


def _inproj_kernel(x_ref, mod_ref, w_ref, o_ref, h_scr):
    @pl.when(pl.program_id(1) == 0)
    def _():
        shift, scale = mod_ref[0, 0:1, :], mod_ref[0, 1:2, :]
        h_scr[...] = (_layer_norm(x_ref[...]) * (1.0 + scale) + shift).astype(BF16)

    o_ref[...] = jnp.dot(h_scr[...], w_ref[...], preferred_element_type=F32)


def _inproj_call(x, mod, w, layer, mod_idx, tm):
    n, d = x.shape
    wp = w.shape[2]
    tn = 1024
    return pl.pallas_call(
        _inproj_kernel,
        grid=(n // tm, wp // tn),
        in_specs=[pl.BlockSpec((tm, d), lambda i, j: (i, 0)),
                  pl.BlockSpec((1, N_MOD, d), lambda i, j: (mod_idx(i), 0, 0)),
                  pl.BlockSpec((None, d, tn), lambda i, j: (layer, 0, j))],
        out_specs=pl.BlockSpec((tm, tn), lambda i, j: (i, j)),
        out_shape=jax.ShapeDtypeStruct((n, wp), F32),
        scratch_shapes=[pltpu.VMEM((tm, d), BF16)],
        compiler_params=_params(("parallel", "arbitrary")),
        name="inproj",
    )(x, mod, w)


def _hgrn_chunk_pre(direction, q, z, v, lbp, tri):
    c = HG_CHUNK
    loglb, log1mlb, omlb = lbp[0:1, :], lbp[1:2, :], lbp[2:3, :]
    log_f = jnp.logaddexp(loglb, log1mlb + jax.nn.log_sigmoid(z))
    k = omlb * jax.nn.sigmoid(-z)
    q = _silu(q)

    hi = log_f.astype(BF16)
    r1 = log_f - hi.astype(F32)
    mid = r1.astype(BF16)
    lo = (r1 - mid.astype(F32)).astype(BF16)
    cs = jnp.dot(tri, jnp.concatenate([hi, mid, lo], axis=1), preferred_element_type=F32)
    b = cs[:, 0:LANES] + cs[:, LANES:2 * LANES] + cs[:, 2 * LANES:3 * LANES]
    total = b[c - 1:c, :] if direction == 0 else b[0:1, :]

    qe = (q * jnp.exp(b)).astype(BF16)
    kd = (k * jnp.exp(total - b)).astype(BF16)
    vb = v.astype(BF16)

    row = lax.broadcasted_iota(jnp.int32, (8, LANES), 0)
    cols = []
    for s in range(c):
        pieces = []
        for j in range(c // 8):
            lo_t, hi_t = 8 * j, 8 * j + 7
            if direction == 0:
                dead, full = hi_t < s, lo_t >= s
            else:
                dead, full = lo_t > s, hi_t <= s
            if dead:
                pieces.append(jnp.zeros((8, LANES), F32))
                continue
            val = q[lo_t:lo_t + 8, :] * jnp.exp(b[lo_t:lo_t + 8, :] - b[s:s + 1, :]) * k[s:s + 1, :]
            if not full:
                keep = (row + lo_t >= s) if direction == 0 else (row + lo_t <= s)
                val = jnp.where(keep, val, 0.0)
            pieces.append(val)
        cols.append(jnp.concatenate(pieces, axis=0))
    pcat = jnp.concatenate(cols, axis=1).astype(BF16)
    return qe, kd, vb, total, pcat


def _hgrn_chunk_post(qe, kd, vb, total, att, st):
    o = lax.dot_general(qe, st.astype(BF16), NT_DIMS, preferred_element_type=F32)
    o = o + jnp.dot(att.astype(BF16), vb, preferred_element_type=F32)
    st = st * jnp.exp(total) + lax.dot_general(vb, kd, TN_DIMS, preferred_element_type=F32)
    return o, st


def _hgrn_kernel(qf_ref, ff_ref, vf_ref, qb_ref, fb_ref, vb_ref, lbp_ref, s0_ref, tri_ref, e_ref, *rest,
                 tile, heads, aliased):
    of_ref, ob_ref, sout_ref, st_scr = rest[2:] if aliased else rest
    i = pl.program_id(2)
    c = HG_CHUNK
    n = tile // c

    @pl.when(i == 0)
    def _():
        for hh in range(heads):
            for direction in range(2):
                st_scr[hh, direction] = s0_ref[0, direction, hh].T

    def body(j, carry):
        chains = []
        for hh in range(heads):
            cs = slice(hh * HEAD_W, (hh + 1) * HEAD_W)
            for direction, q_ref, f_ref, v_ref, o_ref in ((0, qf_ref, ff_ref, vf_ref, of_ref),
                                                          (1, qb_ref, fb_ref, vb_ref, ob_ref)):
                r0 = pl.multiple_of((j if direction == 0 else n - 1 - j) * c, c)
                rows = pl.ds(r0, c)
                pre = _hgrn_chunk_pre(direction, q_ref[rows, cs], f_ref[rows, cs], v_ref[rows, cs],
                                      lbp_ref[direction, :, cs], tri_ref[direction])
                chains.append((hh, direction, o_ref, rows, cs, pre))
        att = jnp.dot(jnp.concatenate([ch[5][4] for ch in chains], axis=0), e_ref[...],
                      preferred_element_type=F32)
        for idx, (hh, direction, o_ref, rows, cs, pre) in enumerate(chains):
            o, st = _hgrn_chunk_post(*pre[:4], att[idx * c:(idx + 1) * c, :], st_scr[hh, direction])
            o_ref[rows, cs] = o
            st_scr[hh, direction] = st
        return carry

    lax.fori_loop(0, n, body, 0)

    @pl.when(i == pl.num_programs(2) - 1)
    def _():
        for hh in range(heads):
            for direction in range(2):
                sout_ref[0, direction, hh] = st_scr[hh, direction].T


def _hgrn_call(y, lbp, s0, tri, emat, *, seq, nseq, row0, tile, heads, prev=None):
    n = y.shape[0]
    nt = seq // tile
    w = heads * HEAD_W
    blk0 = row0 // tile

    def col(c0, rev):
        def index(b, g, i):
            t = nt - 1 - i if rev else i
            return (blk0 + b * nt + t, c0 // w + g)
        return pl.BlockSpec((tile, w), index)

    state_spec = pl.BlockSpec((1, 2, heads, HEAD_W, HEAD_W), lambda b, g, i: (b, 0, g, 0, 0))
    out_f = pl.BlockSpec((tile, w), lambda b, g, i: (blk0 + b * nt + i, g))
    out_b = pl.BlockSpec((tile, w), lambda b, g, i: (blk0 + b * nt + nt - 1 - i, g))
    in_specs = [col(COL_HQ, False), col(COL_FF, False), col(COL_HI, False),
                col(COL_HQ, True), col(COL_FB, True), col(COL_HI, True),
                pl.BlockSpec((2, 3, w), lambda b, g, i: (0, 0, g)),
                state_spec,
                pl.BlockSpec((2, HG_CHUNK, HG_CHUNK), lambda b, g, i: (0, 0, 0)),
                pl.BlockSpec((HG_CHUNK * HEAD_W, HG_CHUNK), lambda b, g, i: (0, 0))]
    args = [y, y, y, y, y, y, lbp, s0, tri, emat]
    aliases = {}
    if prev is not None:
        in_specs += [pl.BlockSpec(memory_space=pl.ANY), pl.BlockSpec(memory_space=pl.ANY)]
        aliases = {len(args): 0, len(args) + 1: 1}
        args += list(prev)
    return pl.pallas_call(
        functools.partial(_hgrn_kernel, tile=tile, heads=heads, aliased=prev is not None),
        grid=(nseq, HEADS // heads, nt),
        in_specs=in_specs,
        out_specs=[out_f, out_b, state_spec],
        out_shape=[jax.ShapeDtypeStruct((n, BRANCH_W), F32), jax.ShapeDtypeStruct((n, BRANCH_W), F32),
                   jax.ShapeDtypeStruct((nseq, 2, HEADS, HEAD_W, HEAD_W), F32)],
        scratch_shapes=[pltpu.VMEM((heads, 2, HEAD_W, HEAD_W), F32)],
        input_output_aliases=aliases,
        compiler_params=_params(("parallel", "parallel", "arbitrary")),
        name="hgrn",
    )(*args)


def _hfin_kernel(of_ref, ob_ref, g_ref, gn_ref, o_ref):
    for h in range(HEADS):
        cs = slice(h * HEAD_W, (h + 1) * HEAD_W)
        o = _rms_norm(of_ref[:, cs] + ob_ref[:, cs], gn_ref[:, cs])
        o_ref[:, cs] = (o * _silu(g_ref[:, cs])).astype(BF16)


def _hfin_call(of, ob, y, gnorm, tm):
    n = of.shape[0]
    row = pl.BlockSpec((tm, BRANCH_W), lambda i: (i, 0))
    return pl.pallas_call(
        _hfin_kernel,
        grid=(n // tm,),
        in_specs=[row, row, pl.BlockSpec((tm, BRANCH_W), lambda i: (i, COL_HG // BRANCH_W)),
                  pl.BlockSpec((1, BRANCH_W), lambda i: (0, 0))],
        out_specs=row,
        out_shape=jax.ShapeDtypeStruct((n, BRANCH_W), BF16),
        compiler_params=_params(("parallel",)),
        name="hfin",
    )(of, ob, y, gnorm)


def _rope(x, cos, sin_signed, lo_half):
    rot = jnp.where(lo_half, pltpu.roll(x, LANES - 16, 1), pltpu.roll(x, 16, 1))
    return x * cos + rot * sin_signed


def _prep_kernel(cq_ref, ckv_ref, kpe_ref, dq_ref, dk_ref, dv_ref, qn_ref, kvn_ref, wuq_ref,
                 cosm_ref, sinm_ref, cosd_ref, sind_ref, kd0_ref, vdt0_ref, ckv0_ref, kpe0_ref,
                 qcat_ref, ckvn_ref, kper_ref, qd_ref, kd_ref, vdt_ref, ckvk_ref, kpek_ref):
    del kd0_ref, vdt0_ref, ckv0_ref, kpe0_ref
    lo_half = (lax.broadcasted_iota(jnp.int32, (1, LANES), 1) % 32) < 16
    cq = _rms_norm(cq_ref[...], qn_ref[...]).astype(BF16)
    qm = jnp.dot(cq, wuq_ref[...], preferred_element_type=F32)
    cos_pe, sin_pe = cosm_ref[...], sinm_ref[...]
    for h in range(HEADS):
        c0 = h * MLA_QK_W
        qcat_ref[:, c0:c0 + LANES] = qm[:, c0:c0 + LANES].astype(BF16)
        pe = _rope(qm[:, c0 + LANES:c0 + 2 * LANES], cos_pe, sin_pe, lo_half)
        qcat_ref[:, c0 + LANES:c0 + 2 * LANES] = pe.astype(BF16)
    ckvn = _rms_norm(ckv_ref[...], kvn_ref[...])
    ckvn_ref[...] = ckvn
    ckvk_ref[...] = ckvn.astype(BF16)
    kper = _rope(kpe_ref[...], cos_pe, sin_pe, lo_half)
    kper_ref[...] = kper
    kpek_ref[...] = kper.astype(BF16)
    cos_d, sin_d = cosd_ref[...], sind_ref[...]
    for h in range(HEADS):
        cs = slice(h * LANES, (h + 1) * LANES)
        qd_ref[:, cs] = _rope(dq_ref[:, cs], cos_d, sin_d, lo_half).astype(BF16)
        kd_ref[:, cs] = _rope(dk_ref[:, cs], cos_d, sin_d, lo_half).astype(BF16)
        vdt_ref[cs, :] = dv_ref[:, cs].T.astype(BF16)


def _prep_call(y, qn, kvn, wuq, cosm, sinm, cosd, sind, key_bufs, key_blk, tm):
    n = y.shape[0]
    row = lambda w, c0: pl.BlockSpec((tm, w), lambda i, c0=c0, w=w: (i, c0 // w))
    full = lambda a: pl.BlockSpec(a.shape, lambda i: (0, 0))
    tab = pl.BlockSpec((tm, LANES), lambda i: (i, 0))
    out = lambda w: pl.BlockSpec((tm, w), lambda i: (i, 0))
    keyrow = lambda w: pl.BlockSpec((tm, w), lambda i: (key_blk(i), 0))
    any_spec = pl.BlockSpec(memory_space=pl.ANY)
    n_in = 13
    return pl.pallas_call(
        _prep_kernel,
        grid=(n // tm,),
        in_specs=[row(MLA_Q_RANK, COL_CQ), row(MLA_KV_RANK, COL_CKV), row(LANES, COL_KPE),
                  row(BRANCH_W, COL_DQ), row(BRANCH_W, COL_DK), row(BRANCH_W, COL_DV),
                  full(qn), full(kvn), full(wuq), tab, tab, tab, tab,
                  any_spec, any_spec, any_spec, any_spec],
        out_specs=[out(HEADS * MLA_QK_W), out(MLA_KV_RANK), out(LANES), out(BRANCH_W),
                   keyrow(BRANCH_W), pl.BlockSpec((BRANCH_W, tm), lambda i: (0, key_blk(i))),
                   keyrow(MLA_KV_RANK), keyrow(LANES)],
        out_shape=[jax.ShapeDtypeStruct((n, HEADS * MLA_QK_W), BF16),
                   jax.ShapeDtypeStruct((n, MLA_KV_RANK), F32),
                   jax.ShapeDtypeStruct((n, LANES), F32),
                   jax.ShapeDtypeStruct((n, BRANCH_W), BF16)]
                  + [jax.ShapeDtypeStruct(b.shape, b.dtype) for b in key_bufs],
        input_output_aliases={n_in + k: 4 + k for k in range(4)},
        compiler_params=_params(("parallel",)),
        name="prep",
    )(y, y, y, y, y, y, qn, kvn, wuq, cosm, sinm, cosd, sind, *key_bufs)


def _kvup_kernel(ckv_ref, kpe_ref, wk_ref, wvt_ref, kcat_ref, vt_ref):
    ckv = ckv_ref[...]
    kn = jnp.dot(ckv, wk_ref[...], preferred_element_type=F32)
    kpe = kpe_ref[...]
    for h in range(HEADS):
        c0 = h * MLA_QK_W
        kcat_ref[:, c0:c0 + LANES] = kn[:, h * LANES:(h + 1) * LANES].astype(BF16)
        kcat_ref[:, c0 + LANES:c0 + 2 * LANES] = kpe
    vt_ref[...] = lax.dot_general(wvt_ref[...], ckv, NT_DIMS, preferred_element_type=F32).astype(BF16)


def _kvup_call(ckv_all, kpe_all, wk, wvt, tk):
    r = ckv_all.shape[0]
    return pl.pallas_call(
        _kvup_kernel,
        grid=(r // tk,),
        in_specs=[pl.BlockSpec((tk, MLA_KV_RANK), lambda i: (i, 0)),
                  pl.BlockSpec((tk, LANES), lambda i: (i, 0)),
                  pl.BlockSpec(wk.shape, lambda i: (0, 0)),
                  pl.BlockSpec(wvt.shape, lambda i: (0, 0))],
        out_specs=[pl.BlockSpec((tk, HEADS * MLA_QK_W), lambda i: (i, 0)),
                   pl.BlockSpec((BRANCH_W, tk), lambda i: (0, i))],
        out_shape=[jax.ShapeDtypeStruct((r, HEADS * MLA_QK_W), BF16),
                   jax.ShapeDtypeStruct((BRANCH_W, r), BF16)],
        compiler_params=_params(("parallel",)),
        name="kvup",
    )(ckv_all, kpe_all, wk, wvt)


def _exp_cols(s, scale):
    e = jnp.exp2((s - jnp.max(s, axis=0, keepdims=True)) * (scale * math.log2(math.e)))
    return e, 1.0 / jnp.sum(e, axis=0, keepdims=True)


def _attn_kernel(qm_ref, km_ref, vmt_ref, qd_ref, kd_ref, vdt_ref, lam_ref, gn_ref, *rest,
                 heads, out_scale, aliased):
    om_ref, od_ref = rest[2:] if aliased else rest
    lam = lam_ref[:, 0:1]
    gn = gn_ref[...]
    lane = lax.broadcasted_iota(jnp.int32, (1, LANES), 1)
    for h in range(heads):
        qk = slice(h * MLA_QK_W, (h + 1) * MLA_QK_W)
        hw = slice(h * HEAD_W, (h + 1) * HEAD_W)
        s = lax.dot_general(km_ref[:, qk], qm_ref[:, qk], NT_DIMS, preferred_element_type=F32)
        e, r = _exp_cols(s, (MLA_NOPE + MLA_ROPE) ** -0.5)
        o_t = jnp.dot(vmt_ref[hw, :], e.astype(BF16), preferred_element_type=F32) * r
        om_ref[:, hw] = o_t.T.astype(BF16)

        q = qd_ref[:, hw]
        zero = jnp.zeros_like(q)
        k = kd_ref[:, hw]
        s1 = lax.dot_general(k, jnp.where(lane < DF_DH, q, zero), NT_DIMS, preferred_element_type=F32)
        s2 = lax.dot_general(k, jnp.where(lane >= DF_DH, q, zero), NT_DIMS, preferred_element_type=F32)
        e1, r1 = _exp_cols(s1, DF_DH ** -0.5)
        e2, r2 = _exp_cols(s2, DF_DH ** -0.5)
        o_t = (jnp.dot(vdt_ref[hw, :], e1.astype(BF16), preferred_element_type=F32) * r1
               - jnp.dot(vdt_ref[hw, :], e2.astype(BF16), preferred_element_type=F32) * (lam * r2))
        od_ref[:, hw] = (_rms_norm(o_t.T, gn) * out_scale).astype(BF16)


def _attn_call(qcat, qd, kcat, vmt, kd, vdt, lam, gn, out_scale, *, nseq, tq, q_tiles, q_row0, tk,
               k_block0, heads, prev=None):
    n = qcat.shape[0]
    qb0 = q_row0 // tq
    qmap = lambda b, g, i: (qb0 + b * q_tiles + i, g)
    kmap = lambda b, g, i: (k_block0 + b, g)
    vmap = lambda b, g, i: (g, k_block0 + b)
    in_specs = [pl.BlockSpec((tq, heads * MLA_QK_W), qmap),
                pl.BlockSpec((tk, heads * MLA_QK_W), kmap),
                pl.BlockSpec((heads * HEAD_W, tk), vmap),
                pl.BlockSpec((tq, heads * HEAD_W), qmap),
                pl.BlockSpec((tk, heads * HEAD_W), kmap),
                pl.BlockSpec((heads * HEAD_W, tk), vmap),
                pl.BlockSpec((1, LANES), lambda b, g, i: (0, 0)),
                pl.BlockSpec((1, LANES), lambda b, g, i: (0, 0))]
    args = [qcat, kcat, vmt, qd, kd, vdt, lam, gn]
    aliases = {}
    if prev is not None:
        in_specs += [pl.BlockSpec(memory_space=pl.ANY), pl.BlockSpec(memory_space=pl.ANY)]
        aliases = {len(args): 0, len(args) + 1: 1}
        args += list(prev)
    out_spec = pl.BlockSpec((tq, heads * HEAD_W), qmap)
    return pl.pallas_call(
        functools.partial(_attn_kernel, heads=heads, out_scale=out_scale, aliased=prev is not None),
        grid=(nseq, HEADS // heads, q_tiles),
        in_specs=in_specs,
        out_specs=[out_spec, out_spec],
        out_shape=[jax.ShapeDtypeStruct((n, BRANCH_W), BF16), jax.ShapeDtypeStruct((n, BRANCH_W), BF16)],
        input_output_aliases=aliases,
        compiler_params=_params(("parallel", "parallel", "arbitrary")),
        name="attn",
    )(*args)


def _merge_kernel(oh_ref, om_ref, od_ref, ga_ref, gb_ref, gc_ref, w_ref, o_ref):
    acc = jax.nn.sigmoid(ga_ref[...]) * jnp.dot(oh_ref[...], w_ref[0], preferred_element_type=F32)
    acc += jax.nn.sigmoid(gb_ref[...]) * jnp.dot(om_ref[...], w_ref[1], preferred_element_type=F32)
    acc += jax.nn.sigmoid(gc_ref[...]) * jnp.dot(od_ref[...], w_ref[2], preferred_element_type=F32)
    o_ref[...] = acc.astype(BF16)


def _merge_call(oh, om, od, y, wbr, layer, tm):
    n = oh.shape[0]
    d = wbr.shape[3]
    tn = 1024
    br = pl.BlockSpec((tm, BRANCH_W), lambda j, i: (i, 0))
    gate = lambda g: pl.BlockSpec((tm, tn), lambda j, i, g=g: (i, (COL_GA + g * d) // tn + j))
    return pl.pallas_call(
        _merge_kernel,
        grid=(d // tn, n // tm),
        in_specs=[br, br, br, gate(0), gate(1), gate(2),
                  pl.BlockSpec((None, 3, BRANCH_W, tn), lambda j, i: (layer, 0, 0, j))],
        out_specs=pl.BlockSpec((tm, tn), lambda j, i: (i, j)),
        out_shape=jax.ShapeDtypeStruct((n, d), BF16),
        compiler_params=_params(("parallel", "parallel")),
        name="merge",
    )(oh, om, od, y, y, y, wbr)


def _outproj_kernel(m_ref, x_ref, mod_ref, w_ref, g_ref, b_ref, x1_ref, h2_ref, h2t_ref, *, alpha):
    gate1 = mod_ref[0, 2:3, :]
    shift2, scale2 = mod_ref[0, 3:4, :], mod_ref[0, 4:5, :]
    y = jnp.dot(m_ref[...], w_ref[...], preferred_element_type=F32)
    x1 = _layer_norm(alpha * x_ref[...] + gate1 * y) * g_ref[...] + b_ref[...]
    x1_ref[...] = x1
    h2 = _layer_norm(x1) * (1.0 + scale2) + shift2
    h2_ref[...] = h2.astype(BF16)
    h2t_ref[...] = h2.T.astype(BF16)


def _outproj_call(merged, x, mod, w, layer, g, b, mod_idx, alpha, tm):
    n, d = x.shape
    row = pl.BlockSpec((tm, d), lambda i: (i, 0))
    vec = pl.BlockSpec((1, d), lambda i: (0, 0))
    return pl.pallas_call(
        functools.partial(_outproj_kernel, alpha=alpha),
        grid=(n // tm,),
        in_specs=[row, row, pl.BlockSpec((1, N_MOD, d), lambda i: (mod_idx(i), 0, 0)),
                  pl.BlockSpec((None, d, d), lambda i: (layer, 0, 0)), vec, vec],
        out_specs=[row, row, pl.BlockSpec((d, tm), lambda i: (0, i))],
        out_shape=[jax.ShapeDtypeStruct((n, d), F32), jax.ShapeDtypeStruct((n, d), BF16),
                   jax.ShapeDtypeStruct((d, n), BF16)],
        compiler_params=_params(("parallel",)),
        name="outproj",
    )(merged, x, mod, w, g, b)


def _peerq_kernel(h_ref, w_ref, sk_ref, st_ref):
    q = jnp.dot(h_ref[...], w_ref[...], preferred_element_type=F32).astype(BF16)
    for hp in range(2 * HEADS):
        st_ref[hp // 2, hp % 2] = lax.dot_general(sk_ref[hp % 2], q[:, hp * LANES:(hp + 1) * LANES], NT_DIMS,
                                                  preferred_element_type=F32)


def _peerq_call(h2, wq, layer, subkeys, tm):
    n, d = h2.shape
    return pl.pallas_call(
        _peerq_kernel,
        grid=(n // tm,),
        in_specs=[pl.BlockSpec((tm, d), lambda i: (i, 0)),
                  pl.BlockSpec((None,) + wq.shape[1:], lambda i: (layer, 0, 0)),
                  pl.BlockSpec(subkeys.shape, lambda i: (0, 0, 0))],
        out_specs=pl.BlockSpec((HEADS, 2, PEER_NKEYS, tm), lambda i: (0, 0, 0, i)),
        out_shape=jax.ShapeDtypeStruct((HEADS, 2, PEER_NKEYS, n), F32),
        compiler_params=_params(("parallel",)),
        name="peerq",
    )(h2, wq, subkeys)


def _top_values(x, count):
    rows = lax.broadcasted_iota(jnp.int32, x.shape, 0).astype(F32)
    vals = []
    for r in range(count):
        m = jnp.max(x, axis=0, keepdims=True)
        vals.append(m)
        if r + 1 < count:
            first = jnp.min(jnp.where(x == m, rows, float(x.shape[0])), axis=0, keepdims=True)
            x = jnp.where(rows == first, -jnp.inf, x)
    return vals


def _topk_kernel(st_ref, e1_ref, e2_ref, th_ref, v_scr, cand_scr):
    row8 = lax.broadcasted_iota(jnp.int32, (8, st_ref.shape[3]), 0)
    for h in range(HEADS):
        s1, s2 = st_ref[h, 0], st_ref[h, 1]
        v1 = _top_values(s1, PEER_TOPK)
        v2 = _top_values(s2, PEER_TOPK)
        for r in range(PEER_TOPK):
            v_scr[0, r:r + 1, :] = v1[r]
            v_scr[1, r:r + 1, :] = v2[r]
        cand_scr[0:16, :] = v1[0] + v_scr[1]
        for a in range(1, 8):
            pair = v1[a] + v_scr[1, 0:8, :]
            cand_scr[8 + 8 * a:16 + 8 * a, :] = jnp.where(row8 < PEER_TOPK // (a + 1), pair, -jnp.inf)
        cand_scr[72:80, :] = v_scr[0, 8:16, :] + v2[0]
        top = _top_values(cand_scr[...], PEER_TOPK)
        z = jnp.exp(top[0] - top[0])
        for r in range(1, PEER_TOPK):
            z = z + jnp.exp(top[r] - top[0])
        e1_ref[h] = jnp.exp(s1 - v1[0]) * (1.0 / z)
        e2_ref[h] = jnp.exp(s2 - v2[0])
        th = jnp.full(s1.shape, jnp.inf, F32)
        for b in range(PEER_TOPK):
            th = jnp.where(s1 + v2[b] >= top[PEER_TOPK - 1], v2[b], th)
        th_ref[h] = th


def _topk_call(st, tm):
    n = st.shape[3]
    out = pl.BlockSpec((HEADS, PEER_NKEYS, tm), lambda i: (0, 0, i))
    shape = jax.ShapeDtypeStruct((HEADS, PEER_NKEYS, n), F32)
    return pl.pallas_call(
        _topk_kernel,
        grid=(n // tm,),
        in_specs=[pl.BlockSpec((HEADS, 2, PEER_NKEYS, tm), lambda i: (0, 0, 0, i))],
        out_specs=[out, out, out],
        out_shape=[shape, shape, shape],
        scratch_shapes=[pltpu.VMEM((2, PEER_TOPK, tm), F32), pltpu.VMEM((80, tm), F32)],
        compiler_params=_params(("parallel",)),
        name="topk",
    )(st)


def _peer_kernel(ht_ref, s2_ref, e1_ref, e2_ref, th_ref, u_ref, vt_ref, ot_ref, ga_scr, *, rows_per_step):
    j = pl.program_id(1)
    nj = pl.num_programs(1) - 1
    slot = j % 2
    tm = ht_ref.shape[1]

    @pl.when(j == 0)
    def _():
        ot_ref[...] = jnp.zeros_like(ot_ref)
        ga_scr[1] = jnp.zeros(ga_scr.shape[1:], BF16)

    ot_ref[...] += jnp.dot(vt_ref[...], ga_scr[1 - slot], preferred_element_type=F32)

    a_t = jnp.dot(u_ref[...], ht_ref[...], preferred_element_type=F32)
    i0 = jnp.minimum(j, nj - 1) * rows_per_step
    for r in range(rows_per_step):
        i = i0 + r
        g = jnp.zeros((PEER_NKEYS, tm), F32)
        for h in range(HEADS):
            keep = s2_ref[h] >= th_ref[h, pl.ds(i, 1), :]
            g = g + jnp.where(keep, e2_ref[h], 0.0) * e1_ref[h, pl.ds(i, 1), :]
        rows = slice(r * PEER_NKEYS, (r + 1) * PEER_NKEYS)
        ga_scr[slot, rows, :] = (g * _gelu(a_t[rows, :])).astype(BF16)


def _peer_call(h2t, st, e1, e2, th, u, vt, layer, tm, rows_per_step):
    d, n = h2t.shape
    te = rows_per_step * PEER_NKEYS
    nj = u.shape[1] // te
    once = pl.Buffered(1)
    tok = pl.BlockSpec((HEADS, PEER_NKEYS, tm), lambda i, j: (0, 0, i), pipeline_mode=once)
    return pl.pallas_call(
        functools.partial(_peer_kernel, rows_per_step=rows_per_step),
        grid=(n // tm, nj + 1),
        in_specs=[pl.BlockSpec((d, tm), lambda i, j: (0, i), pipeline_mode=once),
                  pl.BlockSpec((HEADS, None, PEER_NKEYS, tm), lambda i, j: (0, 1, 0, i), pipeline_mode=once),
                  tok, tok, tok,
                  pl.BlockSpec((None, te, d), lambda i, j: (layer, jnp.minimum(j, nj - 1), 0)),
                  pl.BlockSpec((None, d, te), lambda i, j: (layer, 0, jnp.maximum(j - 1, 0)))],
        out_specs=pl.BlockSpec((d, tm), lambda i, j: (0, i)),
        out_shape=jax.ShapeDtypeStruct((d, n), F32),
        scratch_shapes=[pltpu.VMEM((2, te, tm), BF16)],
        compiler_params=_params(("parallel", "arbitrary")),
        name="peer",
    )(h2t, st, e1, e2, th, u, vt)


def _ln2_kernel(x_ref, yt_ref, mod_ref, g_ref, b_ref, o_ref, *, alpha):
    gate2 = mod_ref[0, 5:6, :]
    o_ref[...] = _layer_norm(alpha * x_ref[...] + gate2 * yt_ref[...].T) * g_ref[...] + b_ref[...]


def _ln2_call(x1, y2t, mod, g, b, mod_idx, alpha, tm):
    n, d = x1.shape
    row = pl.BlockSpec((tm, d), lambda i: (i, 0))
    vec = pl.BlockSpec((1, d), lambda i: (0, 0))
    return pl.pallas_call(
        functools.partial(_ln2_kernel, alpha=alpha),
        grid=(n // tm,),
        in_specs=[row, pl.BlockSpec((d, tm), lambda i: (0, i)),
                  pl.BlockSpec((1, N_MOD, d), lambda i: (mod_idx(i), 0, 0)), vec, vec],
        out_specs=row,
        out_shape=jax.ShapeDtypeStruct((n, d), F32),
        compiler_params=_params(("parallel",)),
        name="ln2",
    )(x1, y2t, mod, g, b)


def _rope_tables(n_ctx, n_lat_seq, lat_seq):
    quarter = 16
    t = jnp.arange(lat_seq)
    rowp = (t // GRID_W).astype(F32)
    colp = (t % GRID_W).astype(F32)
    inv_freq = ROPE_BASE ** (-jnp.arange(quarter, dtype=F32) / quarter)
    ang = jnp.stack([rowp[:, None] * inv_freq, colp[:, None] * inv_freq], axis=1)
    ang = jnp.concatenate([ang, ang], axis=-1).reshape(lat_seq, 64)
    sign = jnp.where((jnp.arange(64) % 32) < 16, -1.0, 1.0).astype(F32)
    cos, sin = jnp.cos(ang), jnp.sin(ang) * sign
    ones, zeros = jnp.ones((lat_seq, 64), F32), jnp.zeros((lat_seq, 64), F32)

    def full(tab, ident):
        lat = jnp.tile(tab, (n_lat_seq, 1))
        return jnp.concatenate([jnp.full((n_ctx, LANES), ident, F32), lat], axis=0)

    cosd = full(jnp.concatenate([cos, cos], axis=1), 1.0)
    sind = full(jnp.concatenate([sin, sin], axis=1), 0.0)
    cosm = full(jnp.concatenate([cos, ones], axis=1), 1.0)
    sinm = full(jnp.concatenate([sin, zeros], axis=1), 0.0)
    return cosd, sind, cosm, sinm


def kernel(x_prompt, x_sample, cache_mla_ckv, cache_mla_kpe, cache_diff_k, cache_diff_v, state_hgrn,
           c, c_ctx, w_ada, b_ada, w_in, hg_lb_logits, hg_norm, mla_q_norm, w_uq, mla_kv_norm, w_ukv,
           df_lambda, df_norm, w_branch, w_out, ln1_g, ln1_b, ln2_g, ln2_b,
           peer_wq, peer_subkeys, peer_u, peer_v):
    nb, seq, d = x_prompt.shape
    db, dseq, _ = x_sample.shape
    depth = w_in.shape[0]
    past = cache_mla_ckv.shape[2]
    n_ctx, n_lat = nb * seq, db * dseq
    n = n_ctx + n_lat
    tk_lat = past + dseq
    tm = 512
    tm_in = 2 * tm if n_ctx % (2 * tm) == 0 and dseq % (2 * tm) == 0 else tm
    tq = 256
    hg_tile = min(dseq, 512)
    assert n_ctx % tm == 0 and dseq % tm == 0 and (db * tk_lat) % seq == 0 and seq % tq == 0
    alpha = (2 * depth) ** 0.25

    def mod_idx(i, tile=tm):
        return jnp.where(i < n_ctx // tile, 0, 1 + (i - n_ctx // tile) // (dseq // tile))

    assert past % tm == 0

    def key_blk(i):
        j = i - n_ctx // tm
        lat = (j // (dseq // tm)) * (tk_lat // tm) + past // tm + j % (dseq // tm)
        return jnp.where(i < n_ctx // tm, db * tk_lat // tm + i, lat)

    lb = jnp.cumsum(jax.nn.softmax(hg_lb_logits.astype(F32), axis=0), axis=0)
    lb = lb - lb[0:1]
    lbp = jnp.stack([jnp.log(lb), jnp.log1p(-lb), 1.0 - lb], axis=2)
    lq = df_lambda.astype(F32)
    lam_init = np.array([0.8 - 0.6 * math.exp(-0.3 * l) for l in range(depth)], np.float32)
    lam = jnp.exp(jnp.sum(lq[:, 0] * lq[:, 1], axis=-1)) - jnp.exp(jnp.sum(lq[:, 2] * lq[:, 3], axis=-1)) + lam_init

    kpe0 = 5 * 1024 + MLA_Q_RANK + MLA_KV_RANK
    w_in_p = jnp.concatenate(
        [w_in[:, :, :5 * 1024], w_in[:, :, kpe0 + MLA_ROPE:], w_in[:, :, 5 * 1024:kpe0],
         w_in[:, :, kpe0:kpe0 + MLA_ROPE],
         jnp.zeros((depth, d, IN_W_PAD - w_in.shape[2]), w_in.dtype)], axis=2).astype(BF16)
    wq4 = w_uq.reshape(depth, MLA_Q_RANK, HEADS, MLA_NOPE + MLA_ROPE)
    w_uq_p = jnp.concatenate(
        [wq4, jnp.zeros((depth, MLA_Q_RANK, HEADS, MLA_QK_W - MLA_NOPE - MLA_ROPE), w_uq.dtype)],
        axis=3).reshape(depth, MLA_Q_RANK, HEADS * MLA_QK_W).astype(BF16)
    wkv4 = w_ukv.reshape(depth, MLA_KV_RANK, HEADS, 2 * HEAD_W)
    w_uk = wkv4[..., :HEAD_W].reshape(depth, MLA_KV_RANK, BRANCH_W).astype(BF16)
    w_uvt = jnp.swapaxes(wkv4[..., HEAD_W:].reshape(depth, MLA_KV_RANK, BRANCH_W), 1, 2).astype(BF16)
    w_branch_b, w_out_b, peer_wq_b = w_branch.astype(BF16), w_out.astype(BF16), peer_wq.astype(BF16)
    subkeys_b, peer_u_b = peer_subkeys.astype(BF16), peer_u.astype(BF16)
    peer_vt_b = jnp.swapaxes(peer_v.astype(BF16), 1, 2)

    tri = jnp.stack([jnp.tril(jnp.ones((HG_CHUNK, HG_CHUNK), F32)),
                     jnp.triu(jnp.ones((HG_CHUNK, HG_CHUNK), F32))]).astype(BF16)
    emat = jnp.repeat(jnp.eye(HG_CHUNK, dtype=F32), HEAD_W, axis=0).astype(BF16)
    cosd, sind, cosm, sinm = _rope_tables(n_ctx, db, dseq)

    cvec = jnp.concatenate([c_ctx[None, :], c, jnp.zeros((8 - 1 - db, d), F32)], axis=0)
    mods = _ada_call(cvec, w_ada, b_ada)[:, :1 + db].reshape(depth, 1 + db, N_MOD, d)

    x = jnp.concatenate([x_prompt.reshape(n_ctx, d), x_sample.reshape(n_lat, d)], axis=0)
    zero_state = jnp.zeros((nb, 2, HEADS, HEAD_W, HEAD_W), F32)
    new_ckv, new_kpe, new_dk, new_dv, new_st = [], [], [], [], []
    for l in range(depth):
        mod = mods[l]
        y = _inproj_call(x, mod, w_in_p, l, functools.partial(mod_idx, tile=tm_in), tm_in)

        of, ob, st_ctx = _hgrn_call(y, lbp[l], zero_state, tri, emat, seq=seq, nseq=nb, row0=0,
                                    tile=seq, heads=4)
        of, ob, _ = _hgrn_call(y, lbp[l], state_hgrn[:, l], tri, emat, seq=dseq, nseq=db, row0=n_ctx,
                               tile=hg_tile, heads=4, prev=(of, ob))
        oh = _hfin_call(of, ob, y, hg_norm[l][None, :], tm)

        seed = lambda cache: jnp.pad(jnp.pad(cache.astype(BF16), ((0, 0), (0, dseq), (0, 0)))
                                     .reshape(db * tk_lat, cache.shape[2]), ((0, n_ctx), (0, 0)))
        key_bufs = (seed(cache_diff_k[:, l].reshape(db, past, BRANCH_W)),
                    seed(cache_diff_v[:, l].reshape(db, past, BRANCH_W)).T,
                    seed(cache_mla_ckv[:, l]),
                    seed(jnp.pad(cache_mla_kpe[:, l], ((0, 0), (0, 0), (0, LANES - MLA_ROPE)))))
        qcat, ckvn, kper, qd, kd_all, vdt, ckv_all, kpe_all = _prep_call(
            y, mla_q_norm[l][None, :], mla_kv_norm[l][None, :], w_uq_p[l], cosm, sinm, cosd, sind,
            key_bufs, key_blk, tm)
        kcat, vmt = _kvup_call(ckv_all, kpe_all, w_uk[l], w_uvt[l], seq)
        lam_row = jnp.full((1, LANES), lam[l], F32)
        attn_args = (qcat, qd, kcat, vmt, kd_all, vdt, lam_row, df_norm[l][None, :], float(1.0 - lam_init[l]))
        om, od = _attn_call(*attn_args, nseq=nb, tq=seq, q_tiles=1, q_row0=0, tk=seq,
                            k_block0=db * tk_lat // seq, heads=HEADS)
        om, od = _attn_call(*attn_args, nseq=db, tq=tq, q_tiles=dseq // tq, q_row0=n_ctx,
                            tk=tk_lat, k_block0=0, heads=1, prev=(om, od))

        merged = _merge_call(oh, om, od, y, w_branch_b, l, tm)
        x1, h2, h2t = _outproj_call(merged, x, mod, w_out_b, l, ln1_g[l][None, :], ln1_b[l][None, :],
                               functools.partial(mod_idx, tile=256), alpha, 256)

        st = _peerq_call(h2, peer_wq_b, l, subkeys_b[l], tm)
        e1, e2, th = _topk_call(st, tm)
        y2t = _peer_call(h2t, st, e1, e2, th, peer_u_b, peer_vt_b, l, tm, 8)
        x = _ln2_call(x1, y2t, mod, ln2_g[l][None, :], ln2_b[l][None, :], mod_idx, alpha, tm)

        new_ckv.append(ckvn[:n_ctx].reshape(nb, seq, MLA_KV_RANK))
        new_kpe.append(kper[:n_ctx, :MLA_ROPE].reshape(nb, seq, MLA_ROPE))
        new_dk.append(y[:n_ctx, COL_DK:COL_DK + BRANCH_W].reshape(nb, seq, HEADS, HEAD_W))
        new_dv.append(y[:n_ctx, COL_DV:COL_DV + BRANCH_W].reshape(nb, seq, HEADS, HEAD_W))
        new_st.append(st_ctx)

    return (x[:n_ctx].reshape(nb, seq, d), x[n_ctx:].reshape(db, dseq, d),
            jnp.stack(new_ckv, axis=1), jnp.stack(new_kpe, axis=1), jnp.stack(new_dk, axis=1),
            jnp.stack(new_dv, axis=1), jnp.stack(new_st, axis=1))
```

```python
import functools
import math

import jax
import jax.numpy as jnp
import numpy as np
from jax import lax
from jax.experimental import pallas as pl
from jax.experimental.pallas import tpu as pltpu

F32 = jnp.float32
BF16 = jnp.bfloat16

GRID_W = 64
ROPE_BASE = 10000.0
HEADS = 8
HEAD_W = 128
HG_CHUNK = 32
MLA_Q_RANK = 512
MLA_KV_RANK = 256
MLA_NOPE = 128
MLA_ROPE = 64
MLA_QK_W = 256
DF_DH = 64
BRANCH_W = HEADS * HEAD_W
PEER_NKEYS = 128
PEER_TOPK = 16
N_MOD = 6
LANES = 128
VMEM_LIMIT = 52 * 1024 * 1024

COL_HQ, COL_FF, COL_FB, COL_HI, COL_HG = 0, 1024, 2048, 3072, 4096
COL_DQ, COL_DK, COL_DV = 5120, 6144, 7168
COL_GA = 8192
COL_CQ, COL_CKV, COL_KPE = 14336, 14848, 15104
IN_W_PAD = 15360

NT_DIMS = (((1,), (1,)), ((), ()))
TN_DIMS = (((0,), (0,)), ((), ()))


def _params(sem, vmem=VMEM_LIMIT):
    return pltpu.CompilerParams(dimension_semantics=sem, vmem_limit_bytes=vmem)


def _layer_norm(x, eps=1e-5):
    xc = x - jnp.mean(x, axis=-1, keepdims=True)
    return xc * lax.rsqrt(jnp.mean(xc * xc, axis=-1, keepdims=True) + eps)


def _rms_norm(x, g, eps=1e-6):
    return x * lax.rsqrt(jnp.mean(x * x, axis=-1, keepdims=True) + eps) * g


def _silu(x):
    return x * jax.nn.sigmoid(x)


def _gelu(x):
    return 0.5 * x * (1.0 + lax.erf(x * (2.0 ** -0.5)))


def _ada_kernel(c_ref, w_ref, b_ref, o_ref):
    s = _silu(c_ref[...]).astype(BF16)
    o_ref[0] = jnp.dot(s, w_ref[0].astype(BF16), preferred_element_type=F32) + b_ref[0]


def _ada_call(cvec, w_ada, b_ada):
    depth, d, w = w_ada.shape
    tn = 1024
    return pl.pallas_call(
        _ada_kernel,
        grid=(depth, w // tn),
        in_specs=[pl.BlockSpec((8, d), lambda l, j: (0, 0)),
                  pl.BlockSpec((1, d, tn), lambda l, j: (l, 0, j)),
                  pl.BlockSpec((1, 1, tn), lambda l, j: (l, 0, j))],
        out_specs=pl.BlockSpec((1, 8, tn), lambda l, j: (l, 0, j)),
        out_shape=jax.ShapeDtypeStruct((depth, 8, w), F32),
        compiler_params=_params(("parallel", "parallel")),
        name="ada",
    )(cvec, w_ada, b_ada.reshape(depth, 1, w))


def _inproj_kernel(x_ref, mod_ref, w_ref, o_ref, h_scr):
    @pl.when(pl.program_id(1) == 0)
    def _():
        shift, scale = mod_ref[0, 0:1, :], mod_ref[0, 1:2, :]
        h_scr[...] = (_layer_norm(x_ref[...]) * (1.0 + scale) + shift).astype(BF16)

    o_ref[...] = jnp.dot(h_scr[...], w_ref[...], preferred_element_type=F32)


def _inproj_call(x, mod, w, layer, mod_idx, tm):
    n, d = x.shape
    wp = w.shape[2]
    tn = 1024
    return pl.pallas_call(
        _inproj_kernel,
        grid=(n // tm, wp // tn),
        in_specs=[pl.BlockSpec((tm, d), lambda i, j: (i, 0)),
                  pl.BlockSpec((1, N_MOD, d), lambda i, j: (mod_idx(i), 0, 0)),
                  pl.BlockSpec((None, d, tn), lambda i, j: (layer, 0, j))],
        out_specs=pl.BlockSpec((tm, tn), lambda i, j: (i, j)),
        out_shape=jax.ShapeDtypeStruct((n, wp), F32),
        scratch_shapes=[pltpu.VMEM((tm, d), BF16)],
        compiler_params=_params(("parallel", "arbitrary")),
        name="inproj",
    )(x, mod, w)


def _hgrn_chunk_pre(direction, q, z, v, lbp, tri):
    c = HG_CHUNK
    loglb, log1mlb, omlb = lbp[0:1, :], lbp[1:2, :], lbp[2:3, :]
    log_f = jnp.logaddexp(loglb, log1mlb + jax.nn.log_sigmoid(z))
    k = omlb * jax.nn.sigmoid(-z)
    q = _silu(q)

    hi = log_f.astype(BF16)
    r1 = log_f - hi.astype(F32)
    mid = r1.astype(BF16)
    lo = (r1 - mid.astype(F32)).astype(BF16)
    cs = jnp.dot(tri, jnp.concatenate([hi, mid, lo], axis=1), preferred_element_type=F32)
    b = cs[:, 0:LANES] + cs[:, LANES:2 * LANES] + cs[:, 2 * LANES:3 * LANES]
    total = b[c - 1:c, :] if direction == 0 else b[0:1, :]

    qe = (q * jnp.exp(b)).astype(BF16)
    kd = (k * jnp.exp(total - b)).astype(BF16)
    vb = v.astype(BF16)

    row = lax.broadcasted_iota(jnp.int32, (8, LANES), 0)
    cols = []
    for s in range(c):
        pieces = []
        for j in range(c // 8):
            lo_t, hi_t = 8 * j, 8 * j + 7
            if direction == 0:
                dead, full = hi_t < s, lo_t >= s
            else:
                dead, full = lo_t > s, hi_t <= s
            if dead:
                pieces.append(jnp.zeros((8, LANES), F32))
                continue
            val = q[lo_t:lo_t + 8, :] * jnp.exp(b[lo_t:lo_t + 8, :] - b[s:s + 1, :]) * k[s:s + 1, :]
            if not full:
                keep = (row + lo_t >= s) if direction == 0 else (row + lo_t <= s)
                val = jnp.where(keep, val, 0.0)
            pieces.append(val)
        cols.append(jnp.concatenate(pieces, axis=0))
    pcat = jnp.concatenate(cols, axis=1).astype(BF16)
    return qe, kd, vb, total, pcat


def _hgrn_chunk_post(qe, kd, vb, total, att, st):
    o = lax.dot_general(qe, st.astype(BF16), NT_DIMS, preferred_element_type=F32)
    o = o + jnp.dot(att.astype(BF16), vb, preferred_element_type=F32)
    st = st * jnp.exp(total) + lax.dot_general(vb, kd, TN_DIMS, preferred_element_type=F32)
    return o, st


def _hgrn_kernel(qf_ref, ff_ref, vf_ref, qb_ref, fb_ref, vb_ref, lbp_ref, s0_ref, tri_ref, e_ref, *rest,
                 tile, heads, aliased):
    of_ref, ob_ref, sout_ref, st_scr = rest[2:] if aliased else rest
    i = pl.program_id(2)
    c = HG_CHUNK
    n = tile // c

    @pl.when(i == 0)
    def _():
        for hh in range(heads):
            for direction in range(2):
                st_scr[hh, direction] = s0_ref[0, direction, hh].T

    def body(j, carry):
        chains = []
        for hh in range(heads):
            cs = slice(hh * HEAD_W, (hh + 1) * HEAD_W)
            for direction, q_ref, f_ref, v_ref, o_ref in ((0, qf_ref, ff_ref, vf_ref, of_ref),
                                                          (1, qb_ref, fb_ref, vb_ref, ob_ref)):
                r0 = pl.multiple_of((j if direction == 0 else n - 1 - j) * c, c)
                rows = pl.ds(r0, c)
                pre = _hgrn_chunk_pre(direction, q_ref[rows, cs], f_ref[rows, cs], v_ref[rows, cs],
                                      lbp_ref[direction, :, cs], tri_ref[direction])
                chains.append((hh, direction, o_ref, rows, cs, pre))
        att = jnp.dot(jnp.concatenate([ch[5][4] for ch in chains], axis=0), e_ref[...],
                      preferred_element_type=F32)
        for idx, (hh, direction, o_ref, rows, cs, pre) in enumerate(chains):
            o, st = _hgrn_chunk_post(*pre[:4], att[idx * c:(idx + 1) * c, :], st_scr[hh, direction])
            o_ref[rows, cs] = o
            st_scr[hh, direction] = st
        return carry

    lax.fori_loop(0, n, body, 0)

    @pl.when(i == pl.num_programs(2) - 1)
    def _():
        for hh in range(heads):
            for direction in range(2):
                sout_ref[0, direction, hh] = st_scr[hh, direction].T


def _hgrn_call(y, lbp, s0, tri, emat, *, seq, nseq, row0, tile, heads, prev=None):
    n = y.shape[0]
    nt = seq // tile
    w = heads * HEAD_W
    blk0 = row0 // tile

    def col(c0, rev):
        def index(b, g, i):
            t = nt - 1 - i if rev else i
            return (blk0 + b * nt + t, c0 // w + g)
        return pl.BlockSpec((tile, w), index)

    state_spec = pl.BlockSpec((1, 2, heads, HEAD_W, HEAD_W), lambda b, g, i: (b, 0, g, 0, 0))
    out_f = pl.BlockSpec((tile, w), lambda b, g, i: (blk0 + b * nt + i, g))
    out_b = pl.BlockSpec((tile, w), lambda b, g, i: (blk0 + b * nt + nt - 1 - i, g))
    in_specs = [col(COL_HQ, False), col(COL_FF, False), col(COL_HI, False),
                col(COL_HQ, True), col(COL_FB, True), col(COL_HI, True),
                pl.BlockSpec((2, 3, w), lambda b, g, i: (0, 0, g)),
                state_spec,
                pl.BlockSpec((2, HG_CHUNK, HG_CHUNK), lambda b, g, i: (0, 0, 0)),
                pl.BlockSpec((HG_CHUNK * HEAD_W, HG_CHUNK), lambda b, g, i: (0, 0))]
    args = [y, y, y, y, y, y, lbp, s0, tri, emat]
    aliases = {}
    if prev is not None:
        in_specs += [pl.BlockSpec(memory_space=pl.ANY), pl.BlockSpec(memory_space=pl.ANY)]
        aliases = {len(args): 0, len(args) + 1: 1}
        args += list(prev)
    return pl.pallas_call(
        functools.partial(_hgrn_kernel, tile=tile, heads=heads, aliased=prev is not None),
        grid=(nseq, HEADS // heads, nt),
        in_specs=in_specs,
        out_specs=[out_f, out_b, state_spec],
        out_shape=[jax.ShapeDtypeStruct((n, BRANCH_W), F32), jax.ShapeDtypeStruct((n, BRANCH_W), F32),
                   jax.ShapeDtypeStruct((nseq, 2, HEADS, HEAD_W, HEAD_W), F32)],
        scratch_shapes=[pltpu.VMEM((heads, 2, HEAD_W, HEAD_W), F32)],
        input_output_aliases=aliases,
        compiler_params=_params(("parallel", "parallel", "arbitrary")),
        name="hgrn",
    )(*args)


def _hfin_kernel(of_ref, ob_ref, g_ref, gn_ref, o_ref):
    for h in range(HEADS):
        cs = slice(h * HEAD_W, (h + 1) * HEAD_W)
        o = _rms_norm(of_ref[:, cs] + ob_ref[:, cs], gn_ref[:, cs])
        o_ref[:, cs] = (o * _silu(g_ref[:, cs])).astype(BF16)


def _hfin_call(of, ob, y, gnorm, tm):
    n = of.shape[0]
    row = pl.BlockSpec((tm, BRANCH_W), lambda i: (i, 0))
    return pl.pallas_call(
        _hfin_kernel,
        grid=(n // tm,),
        in_specs=[row, row, pl.BlockSpec((tm, BRANCH_W), lambda i: (i, COL_HG // BRANCH_W)),
                  pl.BlockSpec((1, BRANCH_W), lambda i: (0, 0))],
        out_specs=row,
        out_shape=jax.ShapeDtypeStruct((n, BRANCH_W), BF16),
        compiler_params=_params(("parallel",)),
        name="hfin",
    )(of, ob, y, gnorm)


def _rope(x, cos, sin_signed, lo_half):
    rot = jnp.where(lo_half, pltpu.roll(x, LANES - 16, 1), pltpu.roll(x, 16, 1))
    return x * cos + rot * sin_signed


def _prep_kernel(cq_ref, ckv_ref, kpe_ref, dq_ref, dk_ref, dv_ref, qn_ref, kvn_ref, wuq_ref,
                 cosm_ref, sinm_ref, cosd_ref, sind_ref, kd0_ref, vdt0_ref, ckv0_ref, kpe0_ref,
                 qcat_ref, ckvn_ref, kper_ref, qd_ref, kd_ref, vdt_ref, ckvk_ref, kpek_ref):
    del kd0_ref, vdt0_ref, ckv0_ref, kpe0_ref
    lo_half = (lax.broadcasted_iota(jnp.int32, (1, LANES), 1) % 32) < 16
    cq = _rms_norm(cq_ref[...], qn_ref[...]).astype(BF16)
    qm = jnp.dot(cq, wuq_ref[...], preferred_element_type=F32)
    cos_pe, sin_pe = cosm_ref[...], sinm_ref[...]
    for h in range(HEADS):
        c0 = h * MLA_QK_W
        qcat_ref[:, c0:c0 + LANES] = qm[:, c0:c0 + LANES].astype(BF16)
        pe = _rope(qm[:, c0 + LANES:c0 + 2 * LANES], cos_pe, sin_pe, lo_half)
        qcat_ref[:, c0 + LANES:c0 + 2 * LANES] = pe.astype(BF16)
    ckvn = _rms_norm(ckv_ref[...], kvn_ref[...])
    ckvn_ref[...] = ckvn
    ckvk_ref[...] = ckvn.astype(BF16)
    kper = _rope(kpe_ref[...], cos_pe, sin_pe, lo_half)
    kper_ref[...] = kper
    kpek_ref[...] = kper.astype(BF16)
    cos_d, sin_d = cosd_ref[...], sind_ref[...]
    for h in range(HEADS):
        cs = slice(h * LANES, (h + 1) * LANES)
        qd_ref[:, cs] = _rope(dq_ref[:, cs], cos_d, sin_d, lo_half).astype(BF16)
        kd_ref[:, cs] = _rope(dk_ref[:, cs], cos_d, sin_d, lo_half).astype(BF16)
        vdt_ref[cs, :] = dv_ref[:, cs].T.astype(BF16)


def _prep_call(y, qn, kvn, wuq, cosm, sinm, cosd, sind, key_bufs, key_blk, tm):
    n = y.shape[0]
    row = lambda w, c0: pl.BlockSpec((tm, w), lambda i, c0=c0, w=w: (i, c0 // w))
    full = lambda a: pl.BlockSpec(a.shape, lambda i: (0, 0))
    tab = pl.BlockSpec((tm, LANES), lambda i: (i, 0))
    out = lambda w: pl.BlockSpec((tm, w), lambda i: (i, 0))
    keyrow = lambda w: pl.BlockSpec((tm, w), lambda i: (key_blk(i), 0))
    any_spec = pl.BlockSpec(memory_space=pl.ANY)
    n_in = 13
    return pl.pallas_call(
        _prep_kernel,
        grid=(n // tm,),
        in_specs=[row(MLA_Q_RANK, COL_CQ), row(MLA_KV_RANK, COL_CKV), row(LANES, COL_KPE),
                  row(BRANCH_W, COL_DQ), row(BRANCH_W, COL_DK), row(BRANCH_W, COL_DV),
                  full(qn), full(kvn), full(wuq), tab, tab, tab, tab,
                  any_spec, any_spec, any_spec, any_spec],
        out_specs=[out(HEADS * MLA_QK_W), out(MLA_KV_RANK), out(LANES), out(BRANCH_W),
                   keyrow(BRANCH_W), pl.BlockSpec((BRANCH_W, tm), lambda i: (0, key_blk(i))),
                   keyrow(MLA_KV_RANK), keyrow(LANES)],
        out_shape=[jax.ShapeDtypeStruct((n, HEADS * MLA_QK_W), BF16),
                   jax.ShapeDtypeStruct((n, MLA_KV_RANK), F32),
                   jax.ShapeDtypeStruct((n, LANES), F32),
                   jax.ShapeDtypeStruct((n, BRANCH_W), BF16)]
                  + [jax.ShapeDtypeStruct(b.shape, b.dtype) for b in key_bufs],
        input_output_aliases={n_in + k: 4 + k for k in range(4)},
        compiler_params=_params(("parallel",)),
        name="prep",
    )(y, y, y, y, y, y, qn, kvn, wuq, cosm, sinm, cosd, sind, *key_bufs)


def _kvup_kernel(ckv_ref, kpe_ref, wk_ref, wvt_ref, kcat_ref, vt_ref):
    ckv = ckv_ref[...]
    kn = jnp.dot(ckv, wk_ref[...], preferred_element_type=F32)
    kpe = kpe_ref[...]
    for h in range(HEADS):
        c0 = h * MLA_QK_W
        kcat_ref[:, c0:c0 + LANES] = kn[:, h * LANES:(h + 1) * LANES].astype(BF16)
        kcat_ref[:, c0 + LANES:c0 + 2 * LANES] = kpe
    vt_ref[...] = lax.dot_general(wvt_ref[...], ckv, NT_DIMS, preferred_element_type=F32).astype(BF16)


def _kvup_call(ckv_all, kpe_all, wk, wvt, tk):
    r = ckv_all.shape[0]
    return pl.pallas_call(
        _kvup_kernel,
        grid=(r // tk,),
        in_specs=[pl.BlockSpec((tk, MLA_KV_RANK), lambda i: (i, 0)),
                  pl.BlockSpec((tk, LANES), lambda i: (i, 0)),
                  pl.BlockSpec(wk.shape, lambda i: (0, 0)),
                  pl.BlockSpec(wvt.shape, lambda i: (0, 0))],
        out_specs=[pl.BlockSpec((tk, HEADS * MLA_QK_W), lambda i: (i, 0)),
                   pl.BlockSpec((BRANCH_W, tk), lambda i: (0, i))],
        out_shape=[jax.ShapeDtypeStruct((r, HEADS * MLA_QK_W), BF16),
                   jax.ShapeDtypeStruct((BRANCH_W, r), BF16)],
        compiler_params=_params(("parallel",)),
        name="kvup",
    )(ckv_all, kpe_all, wk, wvt)


def _exp_cols(s, scale):
    e = jnp.exp2((s - jnp.max(s, axis=0, keepdims=True)) * (scale * math.log2(math.e)))
    return e, 1.0 / jnp.sum(e, axis=0, keepdims=True)


def _attn_kernel(qm_ref, km_ref, vmt_ref, qd_ref, kd_ref, vdt_ref, lam_ref, gn_ref, *rest,
                 heads, out_scale, aliased):
    om_ref, od_ref = rest[2:] if aliased else rest
    lam = lam_ref[:, 0:1]
    gn = gn_ref[...]
    lane = lax.broadcasted_iota(jnp.int32, (1, LANES), 1)
    for h in range(heads):
        qk = slice(h * MLA_QK_W, (h + 1) * MLA_QK_W)
        hw = slice(h * HEAD_W, (h + 1) * HEAD_W)
        s = lax.dot_general(km_ref[:, qk], qm_ref[:, qk], NT_DIMS, preferred_element_type=F32)
        e, r = _exp_cols(s, (MLA_NOPE + MLA_ROPE) ** -0.5)
        o_t = jnp.dot(vmt_ref[hw, :], e.astype(BF16), preferred_element_type=F32) * r
        om_ref[:, hw] = o_t.T.astype(BF16)

        q = qd_ref[:, hw]
        zero = jnp.zeros_like(q)
        k = kd_ref[:, hw]
        s1 = lax.dot_general(k, jnp.where(lane < DF_DH, q, zero), NT_DIMS, preferred_element_type=F32)
        s2 = lax.dot_general(k, jnp.where(lane >= DF_DH, q, zero), NT_DIMS, preferred_element_type=F32)
        e1, r1 = _exp_cols(s1, DF_DH ** -0.5)
        e2, r2 = _exp_cols(s2, DF_DH ** -0.5)
        o_t = (jnp.dot(vdt_ref[hw, :], e1.astype(BF16), preferred_element_type=F32) * r1
               - jnp.dot(vdt_ref[hw, :], e2.astype(BF16), preferred_element_type=F32) * (lam * r2))
        od_ref[:, hw] = (_rms_norm(o_t.T, gn) * out_scale).astype(BF16)


def _attn_call(qcat, qd, kcat, vmt, kd, vdt, lam, gn, out_scale, *, nseq, tq, q_tiles, q_row0, tk,
               k_block0, heads, prev=None):
    n = qcat.shape[0]
    qb0 = q_row0 // tq
    qmap = lambda b, g, i: (qb0 + b * q_tiles + i, g)
    kmap = lambda b, g, i: (k_block0 + b, g)
    vmap = lambda b, g, i: (g, k_block0 + b)
    in_specs = [pl.BlockSpec((tq, heads * MLA_QK_W), qmap),
                pl.BlockSpec((tk, heads * MLA_QK_W), kmap),
                pl.BlockSpec((heads * HEAD_W, tk), vmap),
                pl.BlockSpec((tq, heads * HEAD_W), qmap),
                pl.BlockSpec((tk, heads * HEAD_W), kmap),
                pl.BlockSpec((heads * HEAD_W, tk), vmap),
                pl.BlockSpec((1, LANES), lambda b, g, i: (0, 0)),
                pl.BlockSpec((1, LANES), lambda b, g, i: (0, 0))]
    args = [qcat, kcat, vmt, qd, kd, vdt, lam, gn]
    aliases = {}
    if prev is not None:
        in_specs += [pl.BlockSpec(memory_space=pl.ANY), pl.BlockSpec(memory_space=pl.ANY)]
        aliases = {len(args): 0, len(args) + 1: 1}
        args += list(prev)
    out_spec = pl.BlockSpec((tq, heads * HEAD_W), qmap)
    return pl.pallas_call(
        functools.partial(_attn_kernel, heads=heads, out_scale=out_scale, aliased=prev is not None),
        grid=(nseq, HEADS // heads, q_tiles),
        in_specs=in_specs,
        out_specs=[out_spec, out_spec],
        out_shape=[jax.ShapeDtypeStruct((n, BRANCH_W), BF16), jax.ShapeDtypeStruct((n, BRANCH_W), BF16)],
        input_output_aliases=aliases,
        compiler_params=_params(("parallel", "parallel", "arbitrary")),
        name="attn",
    )(*args)


def _merge_kernel(oh_ref, om_ref, od_ref, ga_ref, gb_ref, gc_ref, w_ref, o_ref):
    acc = jax.nn.sigmoid(ga_ref[...]) * jnp.dot(oh_ref[...], w_ref[0], preferred_element_type=F32)
    acc += jax.nn.sigmoid(gb_ref[...]) * jnp.dot(om_ref[...], w_ref[1], preferred_element_type=F32)
    acc += jax.nn.sigmoid(gc_ref[...]) * jnp.dot(od_ref[...], w_ref[2], preferred_element_type=F32)
    o_ref[...] = acc.astype(BF16)


def _merge_call(oh, om, od, y, wbr, layer, tm):
    n = oh.shape[0]
    d = wbr.shape[3]
    tn = 1024
    br = pl.BlockSpec((tm, BRANCH_W), lambda j, i: (i, 0))
    gate = lambda g: pl.BlockSpec((tm, tn), lambda j, i, g=g: (i, (COL_GA + g * d) // tn + j))
    return pl.pallas_call(
        _merge_kernel,
        grid=(d // tn, n // tm),
        in_specs=[br, br, br, gate(0), gate(1), gate(2),
                  pl.BlockSpec((None, 3, BRANCH_W, tn), lambda j, i: (layer, 0, 0, j))],
        out_specs=pl.BlockSpec((tm, tn), lambda j, i: (i, j)),
        out_shape=jax.ShapeDtypeStruct((n, d), BF16),
        compiler_params=_params(("parallel", "parallel")),
        name="merge",
    )(oh, om, od, y, y, y, wbr)


def _outproj_kernel(m_ref, x_ref, mod_ref, w_ref, g_ref, b_ref, x1_ref, h2_ref, h2t_ref, *, alpha):
    gate1 = mod_ref[0, 2:3, :]
    shift2, scale2 = mod_ref[0, 3:4, :], mod_ref[0, 4:5, :]
    y = jnp.dot(m_ref[...], w_ref[...], preferred_element_type=F32)
    x1 = _layer_norm(alpha * x_ref[...] + gate1 * y) * g_ref[...] + b_ref[...]
    x1_ref[...] = x1
    h2 = _layer_norm(x1) * (1.0 + scale2) + shift2
    h2_ref[...] = h2.astype(BF16)
    h2t_ref[...] = h2.T.astype(BF16)


def _outproj_call(merged, x, mod, w, layer, g, b, mod_idx, alpha, tm):
    n, d = x.shape
    row = pl.BlockSpec((tm, d), lambda i: (i, 0))
    vec = pl.BlockSpec((1, d), lambda i: (0, 0))
    return pl.pallas_call(
        functools.partial(_outproj_kernel, alpha=alpha),
        grid=(n // tm,),
        in_specs=[row, row, pl.BlockSpec((1, N_MOD, d), lambda i: (mod_idx(i), 0, 0)),
                  pl.BlockSpec((None, d, d), lambda i: (layer, 0, 0)), vec, vec],
        out_specs=[row, row, pl.BlockSpec((d, tm), lambda i: (0, i))],
        out_shape=[jax.ShapeDtypeStruct((n, d), F32), jax.ShapeDtypeStruct((n, d), BF16),
                   jax.ShapeDtypeStruct((d, n), BF16)],
        compiler_params=_params(("parallel",)),
        name="outproj",
    )(merged, x, mod, w, g, b)


def _peerq_kernel(h_ref, w_ref, sk_ref, st_ref):
    q = jnp.dot(h_ref[...], w_ref[...], preferred_element_type=F32).astype(BF16)
    for hp in range(2 * HEADS):
        st_ref[hp // 2, hp % 2] = lax.dot_general(sk_ref[hp % 2], q[:, hp * LANES:(hp + 1) * LANES], NT_DIMS,
                                                  preferred_element_type=F32)


def _peerq_call(h2, wq, layer, subkeys, tm):
    n, d = h2.shape
    return pl.pallas_call(
        _peerq_kernel,
        grid=(n // tm,),
        in_specs=[pl.BlockSpec((tm, d), lambda i: (i, 0)),
                  pl.BlockSpec((None,) + wq.shape[1:], lambda i: (layer, 0, 0)),
                  pl.BlockSpec(subkeys.shape, lambda i: (0, 0, 0))],
        out_specs=pl.BlockSpec((HEADS, 2, PEER_NKEYS, tm), lambda i: (0, 0, 0, i)),
        out_shape=jax.ShapeDtypeStruct((HEADS, 2, PEER_NKEYS, n), F32),
        compiler_params=_params(("parallel",)),
        name="peerq",
    )(h2, wq, subkeys)


def _top_values(x, count):
    rows = lax.broadcasted_iota(jnp.int32, x.shape, 0).astype(F32)
    vals = []
    for r in range(count):
        m = jnp.max(x, axis=0, keepdims=True)
        vals.append(m)
        if r + 1 < count:
            first = jnp.min(jnp.where(x == m, rows, float(x.shape[0])), axis=0, keepdims=True)
            x = jnp.where(rows == first, -jnp.inf, x)
    return vals


def _top_values_distinct(x, count):
    vals = []
    for r in range(count):
        m = jnp.max(x, axis=0, keepdims=True)
        vals.append(m)
        if r + 1 < count:
            x = jnp.where(x == m, -jnp.inf, x)
    return vals


def _count_ge(x, t):
    return jnp.sum(jnp.where(x >= t, 1.0, 0.0), axis=0, keepdims=True)


def _topk_head(st_ref, e1_ref, e2_ref, th_ref, v_scr, cand_scr, h, top_fn, check):
    row8 = lax.broadcasted_iota(jnp.int32, (8, st_ref.shape[3]), 0)
    s1, s2 = st_ref[h, 0], st_ref[h, 1]
    v1 = top_fn(s1, PEER_TOPK)
    v2 = top_fn(s2, PEER_TOPK)
    for r in range(PEER_TOPK):
        v_scr[0, r:r + 1, :] = v1[r]
        v_scr[1, r:r + 1, :] = v2[r]
    cand_scr[0:16, :] = v1[0] + v_scr[1]
    for a in range(1, 8):
        pair = v1[a] + v_scr[1, 0:8, :]
        cand_scr[8 + 8 * a:16 + 8 * a, :] = jnp.where(row8 < PEER_TOPK // (a + 1), pair, -jnp.inf)
    cand_scr[72:80, :] = v_scr[0, 8:16, :] + v2[0]
    cand = cand_scr[...]
    top = top_fn(cand, PEER_TOPK)
    z = jnp.exp(top[0] - top[0])
    for r in range(1, PEER_TOPK):
        z = z + jnp.exp(top[r] - top[0])
    e1_ref[h] = jnp.exp(s1 - v1[0]) * (1.0 / z)
    e2_ref[h] = jnp.exp(s2 - v2[0])
    th = jnp.full(s1.shape, jnp.inf, F32)
    for b in range(PEER_TOPK):
        th = jnp.where(s1 + v2[b] >= top[PEER_TOPK - 1], v2[b], th)
    th_ref[h] = th
    if not check:
        return None
    n = float(PEER_TOPK)
    tied = (_count_ge(s1, v1[-1]) != n) | (_count_ge(s2, v2[-1]) != n) | (_count_ge(cand, top[-1]) != n)
    return jnp.where(tied, 1.0, 0.0)


def _topk_kernel(st_ref, e1_ref, e2_ref, th_ref, v_scr, cand_scr):
    tied = jnp.zeros((1, st_ref.shape[3]), F32)
    for h in range(HEADS):
        tied = jnp.maximum(tied, _topk_head(st_ref, e1_ref, e2_ref, th_ref, v_scr, cand_scr, h,
                                            _top_values_distinct, True))

    @pl.when(jnp.max(tied) > 0.0)
    def _():
        for h in range(HEADS):
            _topk_head(st_ref, e1_ref, e2_ref, th_ref, v_scr, cand_scr, h, _top_values, False)


def _topk_call(st, tm):
    n = st.shape[3]
    out = pl.BlockSpec((HEADS, PEER_NKEYS, tm), lambda i: (0, 0, i))
    shape = jax.ShapeDtypeStruct((HEADS, PEER_NKEYS, n), F32)
    return pl.pallas_call(
        _topk_kernel,
        grid=(n // tm,),
        in_specs=[pl.BlockSpec((HEADS, 2, PEER_NKEYS, tm), lambda i: (0, 0, 0, i))],
        out_specs=[out, out, out],
        out_shape=[shape, shape, shape],
        scratch_shapes=[pltpu.VMEM((2, PEER_TOPK, tm), F32), pltpu.VMEM((80, tm), F32)],
        compiler_params=_params(("parallel",)),
        name="topk",
    )(st)


def _peer_kernel(ht_ref, s2_ref, e1_ref, e2_ref, th_ref, u_ref, vt_ref, ot_ref, ga_scr, *, rows_per_step):
    j = pl.program_id(1)
    nj = pl.num_programs(1) - 1
    slot = j % 2
    tm = ht_ref.shape[1]

    @pl.when(j == 0)
    def _():
        ot_ref[...] = jnp.zeros_like(ot_ref)
        ga_scr[1] = jnp.zeros(ga_scr.shape[1:], BF16)

    ot_ref[...] += jnp.dot(vt_ref[...], ga_scr[1 - slot], preferred_element_type=F32)

    a_t = jnp.dot(u_ref[...], ht_ref[...], preferred_element_type=F32)
    i0 = jnp.minimum(j, nj - 1) * rows_per_step
    for r in range(rows_per_step):
        i = i0 + r
        g = jnp.zeros((PEER_NKEYS, tm), F32)
        for h in range(HEADS):
            keep = s2_ref[h] >= th_ref[h, pl.ds(i, 1), :]
            g = g + jnp.where(keep, e2_ref[h], 0.0) * e1_ref[h, pl.ds(i, 1), :]
        rows = slice(r * PEER_NKEYS, (r + 1) * PEER_NKEYS)
        ga_scr[slot, rows, :] = (g * _gelu(a_t[rows, :])).astype(BF16)


def _peer_call(h2t, st, e1, e2, th, u, vt, layer, tm, rows_per_step):
    d, n = h2t.shape
    te = rows_per_step * PEER_NKEYS
    nj = u.shape[1] // te
    once = pl.Buffered(1)
    tok = pl.BlockSpec((HEADS, PEER_NKEYS, tm), lambda i, j: (0, 0, i), pipeline_mode=once)
    return pl.pallas_call(
        functools.partial(_peer_kernel, rows_per_step=rows_per_step),
        grid=(n // tm, nj + 1),
        in_specs=[pl.BlockSpec((d, tm), lambda i, j: (0, i), pipeline_mode=once),
                  pl.BlockSpec((HEADS, None, PEER_NKEYS, tm), lambda i, j: (0, 1, 0, i), pipeline_mode=once),
                  tok, tok, tok,
                  pl.BlockSpec((None, te, d), lambda i, j: (layer, jnp.minimum(j, nj - 1), 0)),
                  pl.BlockSpec((None, d, te), lambda i, j: (layer, 0, jnp.maximum(j - 1, 0)))],
        out_specs=pl.BlockSpec((d, tm), lambda i, j: (0, i)),
        out_shape=jax.ShapeDtypeStruct((d, n), F32),
        scratch_shapes=[pltpu.VMEM((2, te, tm), BF16)],
        compiler_params=_params(("parallel", "arbitrary")),
        name="peer",
    )(h2t, st, e1, e2, th, u, vt)


def _ln2_kernel(x_ref, yt_ref, mod_ref, g_ref, b_ref, o_ref, *, alpha):
    gate2 = mod_ref[0, 5:6, :]
    o_ref[...] = _layer_norm(alpha * x_ref[...] + gate2 * yt_ref[...].T) * g_ref[...] + b_ref[...]


def _ln2_call(x1, y2t, mod, g, b, mod_idx, alpha, tm):
    n, d = x1.shape
    row = pl.BlockSpec((tm, d), lambda i: (i, 0))
    vec = pl.BlockSpec((1, d), lambda i: (0, 0))
    return pl.pallas_call(
        functools.partial(_ln2_kernel, alpha=alpha),
        grid=(n // tm,),
        in_specs=[row, pl.BlockSpec((d, tm), lambda i: (0, i)),
                  pl.BlockSpec((1, N_MOD, d), lambda i: (mod_idx(i), 0, 0)), vec, vec],
        out_specs=row,
        out_shape=jax.ShapeDtypeStruct((n, d), F32),
        compiler_params=_params(("parallel",)),
        name="ln2",
    )(x1, y2t, mod, g, b)


def _rope_tables(n_ctx, n_lat_seq, lat_seq):
    quarter = 16
    t = jnp.arange(lat_seq)
    rowp = (t // GRID_W).astype(F32)
    colp = (t % GRID_W).astype(F32)
    inv_freq = ROPE_BASE ** (-jnp.arange(quarter, dtype=F32) / quarter)
    ang = jnp.stack([rowp[:, None] * inv_freq, colp[:, None] * inv_freq], axis=1)
    ang = jnp.concatenate([ang, ang], axis=-1).reshape(lat_seq, 64)
    sign = jnp.where((jnp.arange(64) % 32) < 16, -1.0, 1.0).astype(F32)
    cos, sin = jnp.cos(ang), jnp.sin(ang) * sign
    ones, zeros = jnp.ones((lat_seq, 64), F32), jnp.zeros((lat_seq, 64), F32)

    def full(tab, ident):
        lat = jnp.tile(tab, (n_lat_seq, 1))
        return jnp.concatenate([jnp.full((n_ctx, LANES), ident, F32), lat], axis=0)

    cosd = full(jnp.concatenate([cos, cos], axis=1), 1.0)
    sind = full(jnp.concatenate([sin, sin], axis=1), 0.0)
    cosm = full(jnp.concatenate([cos, ones], axis=1), 1.0)
    sinm = full(jnp.concatenate([sin, zeros], axis=1), 0.0)
    return cosd, sind, cosm, sinm


def kernel(x_prompt, x_sample, cache_mla_ckv, cache_mla_kpe, cache_diff_k, cache_diff_v, state_hgrn,
           c, c_ctx, w_ada, b_ada, w_in, hg_lb_logits, hg_norm, mla_q_norm, w_uq, mla_kv_norm, w_ukv,
           df_lambda, df_norm, w_branch, w_out, ln1_g, ln1_b, ln2_g, ln2_b,
           peer_wq, peer_subkeys, peer_u, peer_v):
    nb, seq, d = x_prompt.shape
    db, dseq, _ = x_sample.shape
    depth = w_in.shape[0]
    past = cache_mla_ckv.shape[2]
    n_ctx, n_lat = nb * seq, db * dseq
    n = n_ctx + n_lat
    tk_lat = past + dseq
    tm = 512
    tm_in = 2 * tm if n_ctx % (2 * tm) == 0 and dseq % (2 * tm) == 0 else tm
    tq = 256
    hg_tile = min(dseq, 512)
    assert n_ctx % tm == 0 and dseq % tm == 0 and (db * tk_lat) % seq == 0 and seq % tq == 0
    alpha = (2 * depth) ** 0.25

    def mod_idx(i, tile=tm):
        return jnp.where(i < n_ctx // tile, 0, 1 + (i - n_ctx // tile) // (dseq // tile))

    assert past % tm == 0

    def key_blk(i):
        j = i - n_ctx // tm
        lat = (j // (dseq // tm)) * (tk_lat // tm) + past // tm + j % (dseq // tm)
        return jnp.where(i < n_ctx // tm, db * tk_lat // tm + i, lat)

    lb = jnp.cumsum(jax.nn.softmax(hg_lb_logits.astype(F32), axis=0), axis=0)
    lb = lb - lb[0:1]
    lbp = jnp.stack([jnp.log(lb), jnp.log1p(-lb), 1.0 - lb], axis=2)
    lq = df_lambda.astype(F32)
    lam_init = np.array([0.8 - 0.6 * math.exp(-0.3 * l) for l in range(depth)], np.float32)
    lam = jnp.exp(jnp.sum(lq[:, 0] * lq[:, 1], axis=-1)) - jnp.exp(jnp.sum(lq[:, 2] * lq[:, 3], axis=-1)) + lam_init

    kpe0 = 5 * 1024 + MLA_Q_RANK + MLA_KV_RANK
    w_in_p = jnp.concatenate(
        [w_in[:, :, :5 * 1024], w_in[:, :, kpe0 + MLA_ROPE:], w_in[:, :, 5 * 1024:kpe0],
         w_in[:, :, kpe0:kpe0 + MLA_ROPE],
         jnp.zeros((depth, d, IN_W_PAD - w_in.shape[2]), w_in.dtype)], axis=2).astype(BF16)
    wq4 = w_uq.reshape(depth, MLA_Q_RANK, HEADS, MLA_NOPE + MLA_ROPE)
    w_uq_p = jnp.concatenate(
        [wq4, jnp.zeros((depth, MLA_Q_RANK, HEADS, MLA_QK_W - MLA_NOPE - MLA_ROPE), w_uq.dtype)],
        axis=3).reshape(depth, MLA_Q_RANK, HEADS * MLA_QK_W).astype(BF16)
    wkv4 = w_ukv.reshape(depth, MLA_KV_RANK, HEADS, 2 * HEAD_W)
    w_uk = wkv4[..., :HEAD_W].reshape(depth, MLA_KV_RANK, BRANCH_W).astype(BF16)
    w_uvt = jnp.swapaxes(wkv4[..., HEAD_W:].reshape(depth, MLA_KV_RANK, BRANCH_W), 1, 2).astype(BF16)
    w_branch_b, w_out_b, peer_wq_b = w_branch.astype(BF16), w_out.astype(BF16), peer_wq.astype(BF16)
    subkeys_b, peer_u_b = peer_subkeys.astype(BF16), peer_u.astype(BF16)
    peer_vt_b = jnp.swapaxes(peer_v.astype(BF16), 1, 2)

    tri = jnp.stack([jnp.tril(jnp.ones((HG_CHUNK, HG_CHUNK), F32)),
                     jnp.triu(jnp.ones((HG_CHUNK, HG_CHUNK), F32))]).astype(BF16)
    emat = jnp.repeat(jnp.eye(HG_CHUNK, dtype=F32), HEAD_W, axis=0).astype(BF16)
    cosd, sind, cosm, sinm = _rope_tables(n_ctx, db, dseq)

    cvec = jnp.concatenate([c_ctx[None, :], c, jnp.zeros((8 - 1 - db, d), F32)], axis=0)
    mods = _ada_call(cvec, w_ada, b_ada)[:, :1 + db].reshape(depth, 1 + db, N_MOD, d)

    x = jnp.concatenate([x_prompt.reshape(n_ctx, d), x_sample.reshape(n_lat, d)], axis=0)
    zero_state = jnp.zeros((nb, 2, HEADS, HEAD_W, HEAD_W), F32)
    new_ckv, new_kpe, new_dk, new_dv, new_st = [], [], [], [], []
    for l in range(depth):
        mod = mods[l]
        y = _inproj_call(x, mod, w_in_p, l, functools.partial(mod_idx, tile=tm_in), tm_in)

        of, ob, st_ctx = _hgrn_call(y, lbp[l], zero_state, tri, emat, seq=seq, nseq=nb, row0=0,
                                    tile=seq, heads=4)
        of, ob, _ = _hgrn_call(y, lbp[l], state_hgrn[:, l], tri, emat, seq=dseq, nseq=db, row0=n_ctx,
                               tile=hg_tile, heads=4, prev=(of, ob))
        oh = _hfin_call(of, ob, y, hg_norm[l][None, :], tm)

        seed = lambda cache: jnp.pad(jnp.pad(cache.astype(BF16), ((0, 0), (0, dseq), (0, 0)))
                                     .reshape(db * tk_lat, cache.shape[2]), ((0, n_ctx), (0, 0)))
        key_bufs = (seed(cache_diff_k[:, l].reshape(db, past, BRANCH_W)),
                    seed(cache_diff_v[:, l].reshape(db, past, BRANCH_W)).T,
                    seed(cache_mla_ckv[:, l]),
                    seed(jnp.pad(cache_mla_kpe[:, l], ((0, 0), (0, 0), (0, LANES - MLA_ROPE)))))
        qcat, ckvn, kper, qd, kd_all, vdt, ckv_all, kpe_all = _prep_call(
            y, mla_q_norm[l][None, :], mla_kv_norm[l][None, :], w_uq_p[l], cosm, sinm, cosd, sind,
            key_bufs, key_blk, tm)
        kcat, vmt = _kvup_call(ckv_all, kpe_all, w_uk[l], w_uvt[l], seq)
        lam_row = jnp.full((1, LANES), lam[l], F32)
        attn_args = (qcat, qd, kcat, vmt, kd_all, vdt, lam_row, df_norm[l][None, :], float(1.0 - lam_init[l]))
        om, od = _attn_call(*attn_args, nseq=nb, tq=seq, q_tiles=1, q_row0=0, tk=seq,
                            k_block0=db * tk_lat // seq, heads=HEADS)
        om, od = _attn_call(*attn_args, nseq=db, tq=tq, q_tiles=dseq // tq, q_row0=n_ctx,
                            tk=tk_lat, k_block0=0, heads=1, prev=(om, od))

        merged = _merge_call(oh, om, od, y, w_branch_b, l, tm)
        x1, h2, h2t = _outproj_call(merged, x, mod, w_out_b, l, ln1_g[l][None, :], ln1_b[l][None, :],
                               functools.partial(mod_idx, tile=256), alpha, 256)

        st = _peerq_call(h2, peer_wq_b, l, subkeys_b[l], tm)
        e1, e2, th = _topk_call(st, tm)
        y2t = _peer_call(h2t, st, e1, e2, th, peer_u_b, peer_vt_b, l, tm, 8)
        x = _ln2_call(x1, y2t, mod, ln2_g[l][None, :], ln2_b[l][None, :], mod_idx, alpha, tm)

        new_ckv.append(ckvn[:n_ctx].reshape(nb, seq, MLA_KV_RANK))
        new_kpe.append(kper[:n_ctx, :MLA_ROPE].reshape(nb, seq, MLA_ROPE))
        new_dk.append(y[:n_ctx, COL_DK:COL_DK + BRANCH_W].reshape(nb, seq, HEADS, HEAD_W))
        new_dv.append(y[:n_ctx, COL_DV:COL_DV + BRANCH_W].reshape(nb, seq, HEADS, HEAD_W))
        new_st.append(st_ctx)

    return (x[:n_ctx].reshape(nb, seq, d), x[n_ctx:].reshape(db, dseq, d),
            jnp.stack(new_ckv, axis=1), jnp.stack(new_kpe, axis=1), jnp.stack(new_dk, axis=1),
            jnp.stack(new_dv, axis=1), jnp.stack(new_st, axis=1))
```

```python
import functools
import math

import jax
import jax.numpy as jnp
import numpy as np
from jax import lax
from jax.experimental import pallas as pl
from jax.experimental.pallas import tpu as pltpu

F32 = jnp.float32
BF16 = jnp.bfloat16

GRID_W = 64
ROPE_BASE = 10000.0
HEADS = 8
HEAD_W = 128
HG_CHUNK = 32
MLA_Q_RANK = 512
MLA_KV_RANK = 256
MLA_NOPE = 128
MLA_ROPE = 64
MLA_QK_W = 256
DF_DH = 64
BRANCH_W = HEADS * HEAD_W
PEER_NKEYS = 128
PEER_TOPK = 16
N_MOD = 6
LANES = 128
VMEM_LIMIT = 52 * 1024 * 1024

COL_HQ, COL_FF, COL_FB, COL_HI, COL_HG = 0, 1024, 2048, 3072, 4096
COL_DQ, COL_DK, COL_DV = 5120, 6144, 7168
COL_GA = 8192
COL_CQ, COL_CKV, COL_KPE = 14336, 14848, 15104
IN_W_PAD = 15360

NT_DIMS = (((1,), (1,)), ((), ()))
TN_DIMS = (((0,), (0,)), ((), ()))


def _params(sem, vmem=VMEM_LIMIT):
    return pltpu.CompilerParams(dimension_semantics=sem, vmem_limit_bytes=vmem)


def _layer_norm(x, eps=1e-5):
    xc = x - jnp.mean(x, axis=-1, keepdims=True)
    return xc * lax.rsqrt(jnp.mean(xc * xc, axis=-1, keepdims=True) + eps)


def _rms_norm(x, g, eps=1e-6):
    return x * lax.rsqrt(jnp.mean(x * x, axis=-1, keepdims=True) + eps) * g


def _silu(x):
    return x * jax.nn.sigmoid(x)


def _gelu(x):
    return 0.5 * x * (1.0 + lax.erf(x * (2.0 ** -0.5)))


def _ada_kernel(c_ref, w_ref, b_ref, o_ref):
    s = _silu(c_ref[...]).astype(BF16)
    o_ref[0] = jnp.dot(s, w_ref[0].astype(BF16), preferred_element_type=F32) + b_ref[0]


def _ada_call(cvec, w_ada, b_ada):
    depth, d, w = w_ada.shape
    tn = 1024
    return pl.pallas_call(
        _ada_kernel,
        grid=(depth, w // tn),
        in_specs=[pl.BlockSpec((8, d), lambda l, j: (0, 0)),
                  pl.BlockSpec((1, d, tn), lambda l, j: (l, 0, j)),
                  pl.BlockSpec((1, 1, tn), lambda l, j: (l, 0, j))],
        out_specs=pl.BlockSpec((1, 8, tn), lambda l, j: (l, 0, j)),
        out_shape=jax.ShapeDtypeStruct((depth, 8, w), F32),
        compiler_params=_params(("parallel", "parallel")),
        name="ada",
    )(cvec, w_ada, b_ada.reshape(depth, 1, w))


def _inproj_kernel(x_ref, mod_ref, w_ref, o_ref, h_scr):
    @pl.when(pl.program_id(1) == 0)
    def _():
        shift, scale = mod_ref[0, 0:1, :], mod_ref[0, 1:2, :]
        h_scr[...] = (_layer_norm(x_ref[...]) * (1.0 + scale) + shift).astype(BF16)

    o_ref[...] = jnp.dot(h_scr[...], w_ref[...], preferred_element_type=F32)


def _inproj_call(x, mod, w, layer, mod_idx, tm):
    n, d = x.shape
    wp = w.shape[2]
    tn = 1024
    return pl.pallas_call(
        _inproj_kernel,
        grid=(n // tm, wp // tn),
        in_specs=[pl.BlockSpec((tm, d), lambda i, j: (i, 0)),
                  pl.BlockSpec((1, N_MOD, d), lambda i, j: (mod_idx(i), 0, 0)),
                  pl.BlockSpec((None, d, tn), lambda i, j: (layer, 0, j))],
        out_specs=pl.BlockSpec((tm, tn), lambda i, j: (i, j)),
        out_shape=jax.ShapeDtypeStruct((n, wp), F32),
        scratch_shapes=[pltpu.VMEM((tm, d), BF16)],
        compiler_params=_params(("parallel", "arbitrary")),
        name="inproj",
    )(x, mod, w)


def _hgrn_chunk_pre(direction, q, z, v, lbp, tri):
    c = HG_CHUNK
    loglb, log1mlb, omlb = lbp[0:1, :], lbp[1:2, :], lbp[2:3, :]
    log_f = jnp.logaddexp(loglb, log1mlb + jax.nn.log_sigmoid(z))
    k = omlb * jax.nn.sigmoid(-z)
    q = _silu(q)

    hi = log_f.astype(BF16)
    r1 = log_f - hi.astype(F32)
    mid = r1.astype(BF16)
    lo = (r1 - mid.astype(F32)).astype(BF16)
    cs = jnp.dot(tri, jnp.concatenate([hi, mid, lo], axis=1), preferred_element_type=F32)
    b = cs[:, 0:LANES] + cs[:, LANES:2 * LANES] + cs[:, 2 * LANES:3 * LANES]
    total = b[c - 1:c, :] if direction == 0 else b[0:1, :]

    qe = (q * jnp.exp(b)).astype(BF16)
    kd = (k * jnp.exp(total - b)).astype(BF16)
    vb = v.astype(BF16)

    row = lax.broadcasted_iota(jnp.int32, (8, LANES), 0)
    cols = []
    for s in range(c):
        pieces = []
        for j in range(c // 8):
            lo_t, hi_t = 8 * j, 8 * j + 7
            if direction == 0:
                dead, full = hi_t < s, lo_t >= s
            else:
                dead, full = lo_t > s, hi_t <= s
            if dead:
                pieces.append(jnp.zeros((8, LANES), F32))
                continue
            val = q[lo_t:lo_t + 8, :] * jnp.exp(b[lo_t:lo_t + 8, :] - b[s:s + 1, :]) * k[s:s + 1, :]
            if not full:
                keep = (row + lo_t >= s) if direction == 0 else (row + lo_t <= s)
                val = jnp.where(keep, val, 0.0)
            pieces.append(val)
        cols.append(jnp.concatenate(pieces, axis=0))
    pcat = jnp.concatenate(cols, axis=1).astype(BF16)
    return qe, kd, vb, total, pcat


def _hgrn_chunk_post(qe, kd, vb, total, att, st):
    o = lax.dot_general(qe, st.astype(BF16), NT_DIMS, preferred_element_type=F32)
    o = o + jnp.dot(att.astype(BF16), vb, preferred_element_type=F32)
    st = st * jnp.exp(total) + lax.dot_general(vb, kd, TN_DIMS, preferred_element_type=F32)
    return o, st


def _hgrn_kernel(qf_ref, ff_ref, vf_ref, qb_ref, fb_ref, vb_ref, lbp_ref, s0_ref, tri_ref, e_ref, *rest,
                 tile, heads, aliased):
    of_ref, ob_ref, sout_ref, st_scr = rest[2:] if aliased else rest
    i = pl.program_id(2)
    c = HG_CHUNK
    n = tile // c

    @pl.when(i == 0)
    def _():
        for hh in range(heads):
            for direction in range(2):
                st_scr[hh, direction] = s0_ref[0, direction, hh].T

    def body(j, carry):
        chains = []
        for hh in range(heads):
            cs = slice(hh * HEAD_W, (hh + 1) * HEAD_W)
            for direction, q_ref, f_ref, v_ref, o_ref in ((0, qf_ref, ff_ref, vf_ref, of_ref),
                                                          (1, qb_ref, fb_ref, vb_ref, ob_ref)):
                r0 = pl.multiple_of((j if direction == 0 else n - 1 - j) * c, c)
                rows = pl.ds(r0, c)
                pre = _hgrn_chunk_pre(direction, q_ref[rows, cs], f_ref[rows, cs], v_ref[rows, cs],
                                      lbp_ref[direction, :, cs], tri_ref[direction])
                chains.append((hh, direction, o_ref, rows, cs, pre))
        att = jnp.dot(jnp.concatenate([ch[5][4] for ch in chains], axis=0), e_ref[...],
                      preferred_element_type=F32)
        for idx, (hh, direction, o_ref, rows, cs, pre) in enumerate(chains):
            o, st = _hgrn_chunk_post(*pre[:4], att[idx * c:(idx + 1) * c, :], st_scr[hh, direction])
            o_ref[rows, cs] = o
            st_scr[hh, direction] = st
        return carry

    lax.fori_loop(0, n, body, 0)

    @pl.when(i == pl.num_programs(2) - 1)
    def _():
        for hh in range(heads):
            for direction in range(2):
                sout_ref[0, direction, hh] = st_scr[hh, direction].T


def _hgrn_call(y, lbp, s0, tri, emat, *, seq, nseq, row0, tile, heads, prev=None):
    n = y.shape[0]
    nt = seq // tile
    w = heads * HEAD_W
    blk0 = row0 // tile

    def col(c0, rev):
        def index(b, g, i):
            t = nt - 1 - i if rev else i
            return (blk0 + b * nt + t, c0 // w + g)
        return pl.BlockSpec((tile, w), index)

    state_spec = pl.BlockSpec((1, 2, heads, HEAD_W, HEAD_W), lambda b, g, i: (b, 0, g, 0, 0))
    out_f = pl.BlockSpec((tile, w), lambda b, g, i: (blk0 + b * nt + i, g))
    out_b = pl.BlockSpec((tile, w), lambda b, g, i: (blk0 + b * nt + nt - 1 - i, g))
    in_specs = [col(COL_HQ, False), col(COL_FF, False), col(COL_HI, False),
                col(COL_HQ, True), col(COL_FB, True), col(COL_HI, True),
                pl.BlockSpec((2, 3, w), lambda b, g, i: (0, 0, g)),
                state_spec,
                pl.BlockSpec((2, HG_CHUNK, HG_CHUNK), lambda b, g, i: (0, 0, 0)),
                pl.BlockSpec((HG_CHUNK * HEAD_W, HG_CHUNK), lambda b, g, i: (0, 0))]
    args = [y, y, y, y, y, y, lbp, s0, tri, emat]
    aliases = {}
    if prev is not None:
        in_specs += [pl.BlockSpec(memory_space=pl.ANY), pl.BlockSpec(memory_space=pl.ANY)]
        aliases = {len(args): 0, len(args) + 1: 1}
        args += list(prev)
    return pl.pallas_call(
        functools.partial(_hgrn_kernel, tile=tile, heads=heads, aliased=prev is not None),
        grid=(nseq, HEADS // heads, nt),
        in_specs=in_specs,
        out_specs=[out_f, out_b, state_spec],
        out_shape=[jax.ShapeDtypeStruct((n, BRANCH_W), F32), jax.ShapeDtypeStruct((n, BRANCH_W), F32),
                   jax.ShapeDtypeStruct((nseq, 2, HEADS, HEAD_W, HEAD_W), F32)],
        scratch_shapes=[pltpu.VMEM((heads, 2, HEAD_W, HEAD_W), F32)],
        input_output_aliases=aliases,
        compiler_params=_params(("parallel", "parallel", "arbitrary")),
        name="hgrn",
    )(*args)


def _hfin_kernel(of_ref, ob_ref, g_ref, gn_ref, o_ref):
    for h in range(HEADS):
        cs = slice(h * HEAD_W, (h + 1) * HEAD_W)
        o = _rms_norm(of_ref[:, cs] + ob_ref[:, cs], gn_ref[:, cs])
        o_ref[:, cs] = (o * _silu(g_ref[:, cs])).astype(BF16)


def _hfin_call(of, ob, y, gnorm, tm):
    n = of.shape[0]
    row = pl.BlockSpec((tm, BRANCH_W), lambda i: (i, 0))
    return pl.pallas_call(
        _hfin_kernel,
        grid=(n // tm,),
        in_specs=[row, row, pl.BlockSpec((tm, BRANCH_W), lambda i: (i, COL_HG // BRANCH_W)),
                  pl.BlockSpec((1, BRANCH_W), lambda i: (0, 0))],
        out_specs=row,
        out_shape=jax.ShapeDtypeStruct((n, BRANCH_W), BF16),
        compiler_params=_params(("parallel",)),
        name="hfin",
    )(of, ob, y, gnorm)


def _rope(x, cos, sin_signed, lo_half):
    rot = jnp.where(lo_half, pltpu.roll(x, LANES - 16, 1), pltpu.roll(x, 16, 1))
    return x * cos + rot * sin_signed


def _prep_kernel(cq_ref, ckv_ref, kpe_ref, dq_ref, dk_ref, dv_ref, qn_ref, kvn_ref, wuq_ref,
                 cosm_ref, sinm_ref, cosd_ref, sind_ref, kd0_ref, vdt0_ref, ckv0_ref, kpe0_ref,
                 qcat_ref, ckvn_ref, kper_ref, qd_ref, kd_ref, vdt_ref, ckvk_ref, kpek_ref):
    del kd0_ref, vdt0_ref, ckv0_ref, kpe0_ref
    lo_half = (lax.broadcasted_iota(jnp.int32, (1, LANES), 1) % 32) < 16
    cq = _rms_norm(cq_ref[...], qn_ref[...]).astype(BF16)
    qm = jnp.dot(cq, wuq_ref[...], preferred_element_type=F32)
    cos_pe, sin_pe = cosm_ref[...], sinm_ref[...]
    for h in range(HEADS):
        c0 = h * MLA_QK_W
        qcat_ref[:, c0:c0 + LANES] = qm[:, c0:c0 + LANES].astype(BF16)
        pe = _rope(qm[:, c0 + LANES:c0 + 2 * LANES], cos_pe, sin_pe, lo_half)
        qcat_ref[:, c0 + LANES:c0 + 2 * LANES] = pe.astype(BF16)
    ckvn = _rms_norm(ckv_ref[...], kvn_ref[...])
    ckvn_ref[...] = ckvn
    ckvk_ref[...] = ckvn.astype(BF16)
    kper = _rope(kpe_ref[...], cos_pe, sin_pe, lo_half)
    kper_ref[...] = kper
    kpek_ref[...] = kper.astype(BF16)
    cos_d, sin_d = cosd_ref[...], sind_ref[...]
    for h in range(HEADS):
        cs = slice(h * LANES, (h + 1) * LANES)
        qd_ref[:, cs] = _rope(dq_ref[:, cs], cos_d, sin_d, lo_half).astype(BF16)
        kd_ref[:, cs] = _rope(dk_ref[:, cs], cos_d, sin_d, lo_half).astype(BF16)
        vdt_ref[cs, :] = dv_ref[:, cs].T.astype(BF16)


def _prep_call(y, qn, kvn, wuq, cosm, sinm, cosd, sind, key_bufs, key_blk, tm):
    n = y.shape[0]
    row = lambda w, c0: pl.BlockSpec((tm, w), lambda i, c0=c0, w=w: (i, c0 // w))
    full = lambda a: pl.BlockSpec(a.shape, lambda i: (0, 0))
    tab = pl.BlockSpec((tm, LANES), lambda i: (i, 0))
    out = lambda w: pl.BlockSpec((tm, w), lambda i: (i, 0))
    keyrow = lambda w: pl.BlockSpec((tm, w), lambda i: (key_blk(i), 0))
    any_spec = pl.BlockSpec(memory_space=pl.ANY)
    n_in = 13
    return pl.pallas_call(
        _prep_kernel,
        grid=(n // tm,),
        in_specs=[row(MLA_Q_RANK, COL_CQ), row(MLA_KV_RANK, COL_CKV), row(LANES, COL_KPE),
                  row(BRANCH_W, COL_DQ), row(BRANCH_W, COL_DK), row(BRANCH_W, COL_DV),
                  full(qn), full(kvn), full(wuq), tab, tab, tab, tab,
                  any_spec, any_spec, any_spec, any_spec],
        out_specs=[out(HEADS * MLA_QK_W), out(MLA_KV_RANK), out(LANES), out(BRANCH_W),
                   keyrow(BRANCH_W), pl.BlockSpec((BRANCH_W, tm), lambda i: (0, key_blk(i))),
                   keyrow(MLA_KV_RANK), keyrow(LANES)],
        out_shape=[jax.ShapeDtypeStruct((n, HEADS * MLA_QK_W), BF16),
                   jax.ShapeDtypeStruct((n, MLA_KV_RANK), F32),
                   jax.ShapeDtypeStruct((n, LANES), F32),
                   jax.ShapeDtypeStruct((n, BRANCH_W), BF16)]
                  + [jax.ShapeDtypeStruct(b.shape, b.dtype) for b in key_bufs],
        input_output_aliases={n_in + k: 4 + k for k in range(4)},
        compiler_params=_params(("parallel",)),
        name="prep",
    )(y, y, y, y, y, y, qn, kvn, wuq, cosm, sinm, cosd, sind, *key_bufs)


def _kvup_kernel(ckv_ref, kpe_ref, wk_ref, wvt_ref, kcat_ref, vt_ref):
    ckv = ckv_ref[...]
    kn = jnp.dot(ckv, wk_ref[...], preferred_element_type=F32)
    kpe = kpe_ref[...]
    for h in range(HEADS):
        c0 = h * MLA_QK_W
        kcat_ref[:, c0:c0 + LANES] = kn[:, h * LANES:(h + 1) * LANES].astype(BF16)
        kcat_ref[:, c0 + LANES:c0 + 2 * LANES] = kpe
    vt_ref[...] = lax.dot_general(wvt_ref[...], ckv, NT_DIMS, preferred_element_type=F32).astype(BF16)


def _kvup_call(ckv_all, kpe_all, wk, wvt, tk):
    r = ckv_all.shape[0]
    return pl.pallas_call(
        _kvup_kernel,
        grid=(r // tk,),
        in_specs=[pl.BlockSpec((tk, MLA_KV_RANK), lambda i: (i, 0)),
                  pl.BlockSpec((tk, LANES), lambda i: (i, 0)),
                  pl.BlockSpec(wk.shape, lambda i: (0, 0)),
                  pl.BlockSpec(wvt.shape, lambda i: (0, 0))],
        out_specs=[pl.BlockSpec((tk, HEADS * MLA_QK_W), lambda i: (i, 0)),
                   pl.BlockSpec((BRANCH_W, tk), lambda i: (0, i))],
        out_shape=[jax.ShapeDtypeStruct((r, HEADS * MLA_QK_W), BF16),
                   jax.ShapeDtypeStruct((BRANCH_W, r), BF16)],
        compiler_params=_params(("parallel",)),
        name="kvup",
    )(ckv_all, kpe_all, wk, wvt)


def _exp_cols(s, scale):
    e = jnp.exp2((s - jnp.max(s, axis=0, keepdims=True)) * (scale * math.log2(math.e)))
    return e, 1.0 / jnp.sum(e, axis=0, keepdims=True)


def _attn_kernel(qm_ref, km_ref, vmt_ref, qd_ref, kd_ref, vdt_ref, lam_ref, gn_ref, *rest,
                 heads, out_scale, aliased):
    om_ref, od_ref = rest[2:] if aliased else rest
    lam = lam_ref[:, 0:1]
    gn = gn_ref[...]
    lane = lax.broadcasted_iota(jnp.int32, (1, LANES), 1)
    for h in range(heads):
        qk = slice(h * MLA_QK_W, (h + 1) * MLA_QK_W)
        hw = slice(h * HEAD_W, (h + 1) * HEAD_W)
        s = lax.dot_general(km_ref[:, qk], qm_ref[:, qk], NT_DIMS, preferred_element_type=F32)
        e, r = _exp_cols(s, (MLA_NOPE + MLA_ROPE) ** -0.5)
        o_t = jnp.dot(vmt_ref[hw, :], e.astype(BF16), preferred_element_type=F32) * r
        om_ref[:, hw] = o_t.T.astype(BF16)

        q = qd_ref[:, hw]
        zero = jnp.zeros_like(q)
        k = kd_ref[:, hw]
        s1 = lax.dot_general(k, jnp.where(lane < DF_DH, q, zero), NT_DIMS, preferred_element_type=F32)
        s2 = lax.dot_general(k, jnp.where(lane >= DF_DH, q, zero), NT_DIMS, preferred_element_type=F32)
        e1, r1 = _exp_cols(s1, DF_DH ** -0.5)
        e2, r2 = _exp_cols(s2, DF_DH ** -0.5)
        o_t = (jnp.dot(vdt_ref[hw, :], e1.astype(BF16), preferred_element_type=F32) * r1
               - jnp.dot(vdt_ref[hw, :], e2.astype(BF16), preferred_element_type=F32) * (lam * r2))
        od_ref[:, hw] = (_rms_norm(o_t.T, gn) * out_scale).astype(BF16)


def _attn_call(qcat, qd, kcat, vmt, kd, vdt, lam, gn, out_scale, *, nseq, tq, q_tiles, q_row0, tk,
               k_block0, heads, prev=None):
    n = qcat.shape[0]
    qb0 = q_row0 // tq
    qmap = lambda b, g, i: (qb0 + b * q_tiles + i, g)
    kmap = lambda b, g, i: (k_block0 + b, g)
    vmap = lambda b, g, i: (g, k_block0 + b)
    in_specs = [pl.BlockSpec((tq, heads * MLA_QK_W), qmap),
                pl.BlockSpec((tk, heads * MLA_QK_W), kmap),
                pl.BlockSpec((heads * HEAD_W, tk), vmap),
                pl.BlockSpec((tq, heads * HEAD_W), qmap),
                pl.BlockSpec((tk, heads * HEAD_W), kmap),
                pl.BlockSpec((heads * HEAD_W, tk), vmap),
                pl.BlockSpec((1, LANES), lambda b, g, i: (0, 0)),
                pl.BlockSpec((1, LANES), lambda b, g, i: (0, 0))]
    args = [qcat, kcat, vmt, qd, kd, vdt, lam, gn]
    aliases = {}
    if prev is not None:
        in_specs += [pl.BlockSpec(memory_space=pl.ANY), pl.BlockSpec(memory_space=pl.ANY)]
        aliases = {len(args): 0, len(args) + 1: 1}
        args += list(prev)
    out_spec = pl.BlockSpec((tq, heads * HEAD_W), qmap)
    return pl.pallas_call(
        functools.partial(_attn_kernel, heads=heads, out_scale=out_scale, aliased=prev is not None),
        grid=(nseq, HEADS // heads, q_tiles),
        in_specs=in_specs,
        out_specs=[out_spec, out_spec],
        out_shape=[jax.ShapeDtypeStruct((n, BRANCH_W), BF16), jax.ShapeDtypeStruct((n, BRANCH_W), BF16)],
        input_output_aliases=aliases,
        compiler_params=_params(("parallel", "parallel", "arbitrary")),
        name="attn",
    )(*args)


def _merge_kernel(oh_ref, om_ref, od_ref, ga_ref, gb_ref, gc_ref, w_ref, o_ref):
    acc = jax.nn.sigmoid(ga_ref[...]) * jnp.dot(oh_ref[...], w_ref[0], preferred_element_type=F32)
    acc += jax.nn.sigmoid(gb_ref[...]) * jnp.dot(om_ref[...], w_ref[1], preferred_element_type=F32)
    acc += jax.nn.sigmoid(gc_ref[...]) * jnp.dot(od_ref[...], w_ref[2], preferred_element_type=F32)
    o_ref[...] = acc.astype(BF16)


def _merge_call(oh, om, od, y, wbr, layer, tm):
    n = oh.shape[0]
    d = wbr.shape[3]
    tn = 1024
    br = pl.BlockSpec((tm, BRANCH_W), lambda j, i: (i, 0))
    gate = lambda g: pl.BlockSpec((tm, tn), lambda j, i, g=g: (i, (COL_GA + g * d) // tn + j))
    return pl.pallas_call(
        _merge_kernel,
        grid=(d // tn, n // tm),
        in_specs=[br, br, br, gate(0), gate(1), gate(2),
                  pl.BlockSpec((None, 3, BRANCH_W, tn), lambda j, i: (layer, 0, 0, j))],
        out_specs=pl.BlockSpec((tm, tn), lambda j, i: (i, j)),
        out_shape=jax.ShapeDtypeStruct((n, d), BF16),
        compiler_params=_params(("parallel", "parallel")),
        name="merge",
    )(oh, om, od, y, y, y, wbr)


def _outproj_kernel(m_ref, x_ref, mod_ref, w_ref, g_ref, b_ref, x1_ref, h2_ref, h2t_ref, *, alpha):
    gate1 = mod_ref[0, 2:3, :]
    shift2, scale2 = mod_ref[0, 3:4, :], mod_ref[0, 4:5, :]
    y = jnp.dot(m_ref[...], w_ref[...], preferred_element_type=F32)
    x1 = _layer_norm(alpha * x_ref[...] + gate1 * y) * g_ref[...] + b_ref[...]
    x1_ref[...] = x1
    h2 = _layer_norm(x1) * (1.0 + scale2) + shift2
    h2_ref[...] = h2.astype(BF16)
    h2t_ref[...] = h2.T.astype(BF16)


def _outproj_call(merged, x, mod, w, layer, g, b, mod_idx, alpha, tm):
    n, d = x.shape
    row = pl.BlockSpec((tm, d), lambda i: (i, 0))
    vec = pl.BlockSpec((1, d), lambda i: (0, 0))
    return pl.pallas_call(
        functools.partial(_outproj_kernel, alpha=alpha),
        grid=(n // tm,),
        in_specs=[row, row, pl.BlockSpec((1, N_MOD, d), lambda i: (mod_idx(i), 0, 0)),
                  pl.BlockSpec((None, d, d), lambda i: (layer, 0, 0)), vec, vec],
        out_specs=[row, row, pl.BlockSpec((d, tm), lambda i: (0, i))],
        out_shape=[jax.ShapeDtypeStruct((n, d), F32), jax.ShapeDtypeStruct((n, d), BF16),
                   jax.ShapeDtypeStruct((d, n), BF16)],
        compiler_params=_params(("parallel",)),
        name="outproj",
    )(merged, x, mod, w, g, b)


def _peerq_kernel(h_ref, w_ref, sk_ref, st_ref):
    q = jnp.dot(h_ref[...], w_ref[...], preferred_element_type=F32).astype(BF16)
    for hp in range(2 * HEADS):
        st_ref[hp // 2, hp % 2] = lax.dot_general(sk_ref[hp % 2], q[:, hp * LANES:(hp + 1) * LANES], NT_DIMS,
                                                  preferred_element_type=F32)


def _peerq_call(h2, wq, layer, subkeys, tm):
    n, d = h2.shape
    return pl.pallas_call(
        _peerq_kernel,
        grid=(n // tm,),
        in_specs=[pl.BlockSpec((tm, d), lambda i: (i, 0)),
                  pl.BlockSpec((None,) + wq.shape[1:], lambda i: (layer, 0, 0)),
                  pl.BlockSpec(subkeys.shape, lambda i: (0, 0, 0))],
        out_specs=pl.BlockSpec((HEADS, 2, PEER_NKEYS, tm), lambda i: (0, 0, 0, i)),
        out_shape=jax.ShapeDtypeStruct((HEADS, 2, PEER_NKEYS, n), F32),
        compiler_params=_params(("parallel",)),
        name="peerq",
    )(h2, wq, subkeys)


def _top_values(x, count):
    rows = lax.broadcasted_iota(jnp.int32, x.shape, 0).astype(F32)
    vals = []
    for r in range(count):
        m = jnp.max(x, axis=0, keepdims=True)
        vals.append(m)
        if r + 1 < count:
            first = jnp.min(jnp.where(x == m, rows, float(x.shape[0])), axis=0, keepdims=True)
            x = jnp.where(rows == first, -jnp.inf, x)
    return vals


def _top_values_distinct(x, count):
    vals = []
    for r in range(count):
        m = jnp.max(x, axis=0, keepdims=True)
        vals.append(m)
        if r + 1 < count:
            x = jnp.where(x == m, -jnp.inf, x)
    return vals


def _count_ge(x, t):
    return jnp.sum(jnp.where(x >= t, 1.0, 0.0), axis=0, keepdims=True)


def _topk_head(st_ref, e1_ref, e2_ref, th_ref, v_scr, cand_scr, h, top_fn, check):
    row8 = lax.broadcasted_iota(jnp.int32, (8, st_ref.shape[3]), 0)
    s1, s2 = st_ref[h, 0], st_ref[h, 1]
    v1 = top_fn(s1, PEER_TOPK)
    v2 = top_fn(s2, PEER_TOPK)
    for r in range(PEER_TOPK):
        v_scr[0, r:r + 1, :] = v1[r]
        v_scr[1, r:r + 1, :] = v2[r]
    cand_scr[0:16, :] = v1[0] + v_scr[1]
    for a in range(1, 8):
        pair = v1[a] + v_scr[1, 0:8, :]
        cand_scr[8 + 8 * a:16 + 8 * a, :] = jnp.where(row8 < PEER_TOPK // (a + 1), pair, -jnp.inf)
    cand_scr[72:80, :] = v_scr[0, 8:16, :] + v2[0]
    cand = cand_scr[...]
    top = top_fn(cand, PEER_TOPK)
    z = jnp.exp(top[0] - top[0])
    for r in range(1, PEER_TOPK):
        z = z + jnp.exp(top[r] - top[0])
    e1_ref[h] = jnp.exp(s1 - v1[0]) * (1.0 / z)
    e2_ref[h] = jnp.exp(s2 - v2[0])
    th = jnp.full(s1.shape, jnp.inf, F32)
    for b in range(PEER_TOPK):
        th = jnp.where(s1 + v2[b] >= top[PEER_TOPK - 1], v2[b], th)
    th_ref[h] = th
    if not check:
        return None
    n = float(PEER_TOPK)
    tied = (_count_ge(s1, v1[-1]) != n) | (_count_ge(s2, v2[-1]) != n) | (_count_ge(cand, top[-1]) != n)
    return jnp.where(tied, 1.0, 0.0)


def _topk_kernel(st_ref, e1_ref, e2_ref, th_ref, v_scr, cand_scr):
    for h in range(HEADS):
        tied = _topk_head(st_ref, e1_ref, e2_ref, th_ref, v_scr, cand_scr, h, _top_values_distinct, True)

        @pl.when(jnp.max(tied) > 0.0)
        def _(h=h):
            _topk_head(st_ref, e1_ref, e2_ref, th_ref, v_scr, cand_scr, h, _top_values, False)


def _topk_call(st, tm):
    n = st.shape[3]
    out = pl.BlockSpec((HEADS, PEER_NKEYS, tm), lambda i: (0, 0, i))
    shape = jax.ShapeDtypeStruct((HEADS, PEER_NKEYS, n), F32)
    return pl.pallas_call(
        _topk_kernel,
        grid=(n // tm,),
        in_specs=[pl.BlockSpec((HEADS, 2, PEER_NKEYS, tm), lambda i: (0, 0, 0, i))],
        out_specs=[out, out, out],
        out_shape=[shape, shape, shape],
        scratch_shapes=[pltpu.VMEM((2, PEER_TOPK, tm), F32), pltpu.VMEM((80, tm), F32)],
        compiler_params=_params(("parallel",)),
        name="topk",
    )(st)


def _peer_kernel(ht_ref, s2_ref, e1_ref, e2_ref, th_ref, u_ref, vt_ref, ot_ref, ga_scr, *, rows_per_step):
    j = pl.program_id(1)
    nj = pl.num_programs(1) - 1
    slot = j % 2
    tm = ht_ref.shape[1]

    @pl.when(j == 0)
    def _():
        ot_ref[...] = jnp.zeros_like(ot_ref)
        ga_scr[1] = jnp.zeros(ga_scr.shape[1:], BF16)

    ot_ref[...] += jnp.dot(vt_ref[...], ga_scr[1 - slot], preferred_element_type=F32)

    a_t = jnp.dot(u_ref[...], ht_ref[...], preferred_element_type=F32)
    i0 = jnp.minimum(j, nj - 1) * rows_per_step
    for r in range(rows_per_step):
        i = i0 + r
        g = jnp.zeros((PEER_NKEYS, tm), F32)
        for h in range(HEADS):
            keep = s2_ref[h] >= th_ref[h, pl.ds(i, 1), :]
            g = g + jnp.where(keep, e2_ref[h], 0.0) * e1_ref[h, pl.ds(i, 1), :]
        rows = slice(r * PEER_NKEYS, (r + 1) * PEER_NKEYS)
        ga_scr[slot, rows, :] = (g * _gelu(a_t[rows, :])).astype(BF16)


def _peer_call(h2t, st, e1, e2, th, u, vt, layer, tm, rows_per_step):
    d, n = h2t.shape
    te = rows_per_step * PEER_NKEYS
    nj = u.shape[1] // te
    once = pl.Buffered(1)
    tok = pl.BlockSpec((HEADS, PEER_NKEYS, tm), lambda i, j: (0, 0, i), pipeline_mode=once)
    return pl.pallas_call(
        functools.partial(_peer_kernel, rows_per_step=rows_per_step),
        grid=(n // tm, nj + 1),
        in_specs=[pl.BlockSpec((d, tm), lambda i, j: (0, i), pipeline_mode=once),
                  pl.BlockSpec((HEADS, None, PEER_NKEYS, tm), lambda i, j: (0, 1, 0, i), pipeline_mode=once),
                  tok, tok, tok,
                  pl.BlockSpec((None, te, d), lambda i, j: (layer, jnp.minimum(j, nj - 1), 0)),
                  pl.BlockSpec((None, d, te), lambda i, j: (layer, 0, jnp.maximum(j - 1, 0)))],
        out_specs=pl.BlockSpec((d, tm), lambda i, j: (0, i)),
        out_shape=jax.ShapeDtypeStruct((d, n), F32),
        scratch_shapes=[pltpu.VMEM((2, te, tm), BF16)],
        compiler_params=_params(("parallel", "arbitrary")),
        name="peer",
    )(h2t, st, e1, e2, th, u, vt)


def _ln2_kernel(x_ref, yt_ref, mod_ref, g_ref, b_ref, o_ref, *, alpha):
    gate2 = mod_ref[0, 5:6, :]
    o_ref[...] = _layer_norm(alpha * x_ref[...] + gate2 * yt_ref[...].T) * g_ref[...] + b_ref[...]


def _ln2_call(x1, y2t, mod, g, b, mod_idx, alpha, tm):
    n, d = x1.shape
    row = pl.BlockSpec((tm, d), lambda i: (i, 0))
    vec = pl.BlockSpec((1, d), lambda i: (0, 0))
    return pl.pallas_call(
        functools.partial(_ln2_kernel, alpha=alpha),
        grid=(n // tm,),
        in_specs=[row, pl.BlockSpec((d, tm), lambda i: (0, i)),
                  pl.BlockSpec((1, N_MOD, d), lambda i: (mod_idx(i), 0, 0)), vec, vec],
        out_specs=row,
        out_shape=jax.ShapeDtypeStruct((n, d), F32),
        compiler_params=_params(("parallel",)),
        name="ln2",
    )(x1, y2t, mod, g, b)


def _rope_tables(n_ctx, n_lat_seq, lat_seq):
    quarter = 16
    t = jnp.arange(lat_seq)
    rowp = (t // GRID_W).astype(F32)
    colp = (t % GRID_W).astype(F32)
    inv_freq = ROPE_BASE ** (-jnp.arange(quarter, dtype=F32) / quarter)
    ang = jnp.stack([rowp[:, None] * inv_freq, colp[:, None] * inv_freq], axis=1)
    ang = jnp.concatenate([ang, ang], axis=-1).reshape(lat_seq, 64)
    sign = jnp.where((jnp.arange(64) % 32) < 16, -1.0, 1.0).astype(F32)
    cos, sin = jnp.cos(ang), jnp.sin(ang) * sign
    ones, zeros = jnp.ones((lat_seq, 64), F32), jnp.zeros((lat_seq, 64), F32)

    def full(tab, ident):
        lat = jnp.tile(tab, (n_lat_seq, 1))
        return jnp.concatenate([jnp.full((n_ctx, LANES), ident, F32), lat], axis=0)

    cosd = full(jnp.concatenate([cos, cos], axis=1), 1.0)
    sind = full(jnp.concatenate([sin, sin], axis=1), 0.0)
    cosm = full(jnp.concatenate([cos, ones], axis=1), 1.0)
    sinm = full(jnp.concatenate([sin, zeros], axis=1), 0.0)
    return cosd, sind, cosm, sinm


def kernel(x_prompt, x_sample, cache_mla_ckv, cache_mla_kpe, cache_diff_k, cache_diff_v, state_hgrn,
           c, c_ctx, w_ada, b_ada, w_in, hg_lb_logits, hg_norm, mla_q_norm, w_uq, mla_kv_norm, w_ukv,
           df_lambda, df_norm, w_branch, w_out, ln1_g, ln1_b, ln2_g, ln2_b,
           peer_wq, peer_subkeys, peer_u, peer_v):
    nb, seq, d = x_prompt.shape
    db, dseq, _ = x_sample.shape
    depth = w_in.shape[0]
    past = cache_mla_ckv.shape[2]
    n_ctx, n_lat = nb * seq, db * dseq
    n = n_ctx + n_lat
    tk_lat = past + dseq
    tm = 512
    tm_in = 2 * tm if n_ctx % (2 * tm) == 0 and dseq % (2 * tm) == 0 else tm
    tq = 256
    hg_tile = min(dseq, 512)
    assert n_ctx % tm == 0 and dseq % tm == 0 and (db * tk_lat) % seq == 0 and seq % tq == 0
    alpha = (2 * depth) ** 0.25

    def mod_idx(i, tile=tm):
        return jnp.where(i < n_ctx // tile, 0, 1 + (i - n_ctx // tile) // (dseq // tile))

    assert past % tm == 0

    def key_blk(i):
        j = i - n_ctx // tm
        lat = (j // (dseq // tm)) * (tk_lat // tm) + past // tm + j % (dseq // tm)
        return jnp.where(i < n_ctx // tm, db * tk_lat // tm + i, lat)

    lb = jnp.cumsum(jax.nn.softmax(hg_lb_logits.astype(F32), axis=0), axis=0)
    lb = lb - lb[0:1]
    lbp = jnp.stack([jnp.log(lb), jnp.log1p(-lb), 1.0 - lb], axis=2)
    lq = df_lambda.astype(F32)
    lam_init = np.array([0.8 - 0.6 * math.exp(-0.3 * l) for l in range(depth)], np.float32)
    lam = jnp.exp(jnp.sum(lq[:, 0] * lq[:, 1], axis=-1)) - jnp.exp(jnp.sum(lq[:, 2] * lq[:, 3], axis=-1)) + lam_init

    kpe0 = 5 * 1024 + MLA_Q_RANK + MLA_KV_RANK
    w_in_p = jnp.concatenate(
        [w_in[:, :, :5 * 1024], w_in[:, :, kpe0 + MLA_ROPE:], w_in[:, :, 5 * 1024:kpe0],
         w_in[:, :, kpe0:kpe0 + MLA_ROPE],
         jnp.zeros((depth, d, IN_W_PAD - w_in.shape[2]), w_in.dtype)], axis=2).astype(BF16)
    wq4 = w_uq.reshape(depth, MLA_Q_RANK, HEADS, MLA_NOPE + MLA_ROPE)
    w_uq_p = jnp.concatenate(
        [wq4, jnp.zeros((depth, MLA_Q_RANK, HEADS, MLA_QK_W - MLA_NOPE - MLA_ROPE), w_uq.dtype)],
        axis=3).reshape(depth, MLA_Q_RANK, HEADS * MLA_QK_W).astype(BF16)
    wkv4 = w_ukv.reshape(depth, MLA_KV_RANK, HEADS, 2 * HEAD_W)
    w_uk = wkv4[..., :HEAD_W].reshape(depth, MLA_KV_RANK, BRANCH_W).astype(BF16)
    w_uvt = jnp.swapaxes(wkv4[..., HEAD_W:].reshape(depth, MLA_KV_RANK, BRANCH_W), 1, 2).astype(BF16)
    w_branch_b, w_out_b, peer_wq_b = w_branch.astype(BF16), w_out.astype(BF16), peer_wq.astype(BF16)
    subkeys_b, peer_u_b = peer_subkeys.astype(BF16), peer_u.astype(BF16)
    peer_vt_b = jnp.swapaxes(peer_v.astype(BF16), 1, 2)

    tri = jnp.stack([jnp.tril(jnp.ones((HG_CHUNK, HG_CHUNK), F32)),
                     jnp.triu(jnp.ones((HG_CHUNK, HG_CHUNK), F32))]).astype(BF16)
    emat = jnp.repeat(jnp.eye(HG_CHUNK, dtype=F32), HEAD_W, axis=0).astype(BF16)
    cosd, sind, cosm, sinm = _rope_tables(n_ctx, db, dseq)

    cvec = jnp.concatenate([c_ctx[None, :], c, jnp.zeros((8 - 1 - db, d), F32)], axis=0)
    mods = _ada_call(cvec, w_ada, b_ada)[:, :1 + db].reshape(depth, 1 + db, N_MOD, d)

    x = jnp.concatenate([x_prompt.reshape(n_ctx, d), x_sample.reshape(n_lat, d)], axis=0)
    zero_state = jnp.zeros((nb, 2, HEADS, HEAD_W, HEAD_W), F32)
    new_ckv, new_kpe, new_dk, new_dv, new_st = [], [], [], [], []
    for l in range(depth):
        mod = mods[l]
        y = _inproj_call(x, mod, w_in_p, l, functools.partial(mod_idx, tile=tm_in), tm_in)

        of, ob, st_ctx = _hgrn_call(y, lbp[l], zero_state, tri, emat, seq=seq, nseq=nb, row0=0,
                                    tile=seq, heads=4)
        of, ob, _ = _hgrn_call(y, lbp[l], state_hgrn[:, l], tri, emat, seq=dseq, nseq=db, row0=n_ctx,
                               tile=hg_tile, heads=4, prev=(of, ob))
        oh = _hfin_call(of, ob, y, hg_norm[l][None, :], tm)

        seed = lambda cache: jnp.pad(jnp.pad(cache.astype(BF16), ((0, 0), (0, dseq), (0, 0)))
                                     .reshape(db * tk_lat, cache.shape[2]), ((0, n_ctx), (0, 0)))
        key_bufs = (seed(cache_diff_k[:, l].reshape(db, past, BRANCH_W)),
                    seed(cache_diff_v[:, l].reshape(db, past, BRANCH_W)).T,
                    seed(cache_mla_ckv[:, l]),
                    seed(jnp.pad(cache_mla_kpe[:, l], ((0, 0), (0, 0), (0, LANES - MLA_ROPE)))))
        qcat, ckvn, kper, qd, kd_all, vdt, ckv_all, kpe_all = _prep_call(
            y, mla_q_norm[l][None, :], mla_kv_norm[l][None, :], w_uq_p[l], cosm, sinm, cosd, sind,
            key_bufs, key_blk, tm)
        kcat, vmt = _kvup_call(ckv_all, kpe_all, w_uk[l], w_uvt[l], seq)
        lam_row = jnp.full((1, LANES), lam[l], F32)
        attn_args = (qcat, qd, kcat, vmt, kd_all, vdt, lam_row, df_norm[l][None, :], float(1.0 - lam_init[l]))
        om, od = _attn_call(*attn_args, nseq=nb, tq=seq, q_tiles=1, q_row0=0, tk=seq,
                            k_block0=db * tk_lat // seq, heads=HEADS)
        om, od = _attn_call(*attn_args, nseq=db, tq=tq, q_tiles=dseq // tq, q_row0=n_ctx,
                            tk=tk_lat, k_block0=0, heads=1, prev=(om, od))

        merged = _merge_call(oh, om, od, y, w_branch_b, l, tm)
        x1, h2, h2t = _outproj_call(merged, x, mod, w_out_b, l, ln1_g[l][None, :], ln1_b[l][None, :],
                               functools.partial(mod_idx, tile=256), alpha, 256)

        st = _peerq_call(h2, peer_wq_b, l, subkeys_b[l], tm)
        e1, e2, th = _topk_call(st, tm)
        y2t = _peer_call(h2t, st, e1, e2, th, peer_u_b, peer_vt_b, l, tm, 8)
        x = _ln2_call(x1, y2t, mod, ln2_g[l][None, :], ln2_b[l][None, :], mod_idx, alpha, tm)

        new_ckv.append(ckvn[:n_ctx].reshape(nb, seq, MLA_KV_RANK))
        new_kpe.append(kper[:n_ctx, :MLA_ROPE].reshape(nb, seq, MLA_ROPE))
        new_dk.append(y[:n_ctx, COL_DK:COL_DK + BRANCH_W].reshape(nb, seq, HEADS, HEAD_W))
        new_dv.append(y[:n_ctx, COL_DV:COL_DV + BRANCH_W].reshape(nb, seq, HEADS, HEAD_W))
        new_st.append(st_ctx)

    return (x[:n_ctx].reshape(nb, seq, d), x[n_ctx:].reshape(db, dseq, d),
            jnp.stack(new_ckv, axis=1), jnp.stack(new_kpe, axis=1), jnp.stack(new_dk, axis=1),
            jnp.stack(new_dv, axis=1), jnp.stack(new_st, axis=1))
```

```python
import functools
import math

import jax
import jax.numpy as jnp
import numpy as np
from jax import lax
from jax.experimental import pallas as pl
from jax.experimental.pallas import tpu as pltpu

F32 = jnp.float32
BF16 = jnp.bfloat16

GRID_W = 64
ROPE_BASE = 10000.0
HEADS = 8
HEAD_W = 128
HG_CHUNK = 32
MLA_Q_RANK = 512
MLA_KV_RANK = 256
MLA_NOPE = 128
MLA_ROPE = 64
MLA_QK_W = 256
DF_DH = 64
BRANCH_W = HEADS * HEAD_W
PEER_NKEYS = 128
PEER_TOPK = 16
N_MOD = 6
LANES = 128
VMEM_LIMIT = 52 * 1024 * 1024

COL_HQ, COL_FF, COL_FB, COL_HI, COL_HG = 0, 1024, 2048, 3072, 4096
COL_DQ, COL_DK, COL_DV = 5120, 6144, 7168
COL_GA = 8192
COL_CQ, COL_CKV, COL_KPE = 14336, 14848, 15104
IN_W_PAD = 15360

NT_DIMS = (((1,), (1,)), ((), ()))
TN_DIMS = (((0,), (0,)), ((), ()))


def _params(sem, vmem=VMEM_LIMIT):
    return pltpu.CompilerParams(dimension_semantics=sem, vmem_limit_bytes=vmem)


def _layer_norm(x, eps=1e-5):
    xc = x - jnp.mean(x, axis=-1, keepdims=True)
    return xc * lax.rsqrt(jnp.mean(xc * xc, axis=-1, keepdims=True) + eps)


def _rms_norm(x, g, eps=1e-6):
    return x * lax.rsqrt(jnp.mean(x * x, axis=-1, keepdims=True) + eps) * g


def _silu(x):
    return x * jax.nn.sigmoid(x)


def _gelu(x):
    return 0.5 * x * (1.0 + lax.erf(x * (2.0 ** -0.5)))


def _ada_kernel(c_ref, w_ref, b_ref, o_ref):
    s = _silu(c_ref[...]).astype(BF16)
    o_ref[0] = jnp.dot(s, w_ref[0].astype(BF16), preferred_element_type=F32) + b_ref[0]


def _ada_call(cvec, w_ada, b_ada):
    depth, d, w = w_ada.shape
    tn = 1024
    return pl.pallas_call(
        _ada_kernel,
        grid=(depth, w // tn),
        in_specs=[pl.BlockSpec((8, d), lambda l, j: (0, 0)),
                  pl.BlockSpec((1, d, tn), lambda l, j: (l, 0, j)),
                  pl.BlockSpec((1, 1, tn), lambda l, j: (l, 0, j))],
        out_specs=pl.BlockSpec((1, 8, tn), lambda l, j: (l, 0, j)),
        out_shape=jax.ShapeDtypeStruct((depth, 8, w), F32),
        compiler_params=_params(("parallel", "parallel")),
        name="ada",
    )(cvec, w_ada, b_ada.reshape(depth, 1, w))


def _inproj_kernel(x_ref, mod_ref, w_ref, o_ref, h_scr):
    @pl.when(pl.program_id(1) == 0)
    def _():
        shift, scale = mod_ref[0, 0:1, :], mod_ref[0, 1:2, :]
        h_scr[...] = (_layer_norm(x_ref[...]) * (1.0 + scale) + shift).astype(BF16)

    o_ref[...] = jnp.dot(h_scr[...], w_ref[...], preferred_element_type=F32)


def _inproj_call(x, mod, w, layer, mod_idx, tm):
    n, d = x.shape
    wp = w.shape[2]
    tn = 1024
    return pl.pallas_call(
        _inproj_kernel,
        grid=(n // tm, wp // tn),
        in_specs=[pl.BlockSpec((tm, d), lambda i, j: (i, 0)),
                  pl.BlockSpec((1, N_MOD, d), lambda i, j: (mod_idx(i), 0, 0)),
                  pl.BlockSpec((None, d, tn), lambda i, j: (layer, 0, j))],
        out_specs=pl.BlockSpec((tm, tn), lambda i, j: (i, j)),
        out_shape=jax.ShapeDtypeStruct((n, wp), F32),
        scratch_shapes=[pltpu.VMEM((tm, d), BF16)],
        compiler_params=_params(("parallel", "arbitrary")),
        name="inproj",
    )(x, mod, w)


def _hgrn_chunk_pre(direction, q, z, v, lbp, tri):
    c = HG_CHUNK
    loglb, log1mlb, omlb = lbp[0:1, :], lbp[1:2, :], lbp[2:3, :]
    log_f = jnp.logaddexp(loglb, log1mlb + jax.nn.log_sigmoid(z))
    k = omlb * jax.nn.sigmoid(-z)
    q = _silu(q)

    hi = log_f.astype(BF16)
    r1 = log_f - hi.astype(F32)
    mid = r1.astype(BF16)
    lo = (r1 - mid.astype(F32)).astype(BF16)
    cs = jnp.dot(tri, jnp.concatenate([hi, mid, lo], axis=1), preferred_element_type=F32)
    b = cs[:, 0:LANES] + cs[:, LANES:2 * LANES] + cs[:, 2 * LANES:3 * LANES]
    total = b[c - 1:c, :] if direction == 0 else b[0:1, :]

    qe = (q * jnp.exp(b)).astype(BF16)
    kd = (k * jnp.exp(total - b)).astype(BF16)
    vb = v.astype(BF16)

    row = lax.broadcasted_iota(jnp.int32, (8, LANES), 0)
    cols = []
    for s in range(c):
        pieces = []
        for j in range(c // 8):
            lo_t, hi_t = 8 * j, 8 * j + 7
            if direction == 0:
                dead, full = hi_t < s, lo_t >= s
            else:
                dead, full = lo_t > s, hi_t <= s
            if dead:
                pieces.append(jnp.zeros((8, LANES), F32))
                continue
            val = q[lo_t:lo_t + 8, :] * jnp.exp(b[lo_t:lo_t + 8, :] - b[s:s + 1, :]) * k[s:s + 1, :]
            if not full:
                keep = (row + lo_t >= s) if direction == 0 else (row + lo_t <= s)
                val = jnp.where(keep, val, 0.0)
            pieces.append(val)
        cols.append(jnp.concatenate(pieces, axis=0))
    pcat = jnp.concatenate(cols, axis=1).astype(BF16)
    return qe, kd, vb, total, pcat


def _hgrn_chunk_post(qe, kd, vb, total, att, st):
    o = lax.dot_general(qe, st.astype(BF16), NT_DIMS, preferred_element_type=F32)
    o = o + jnp.dot(att.astype(BF16), vb, preferred_element_type=F32)
    st = st * jnp.exp(total) + lax.dot_general(vb, kd, TN_DIMS, preferred_element_type=F32)
    return o, st


def _hgrn_kernel(qf_ref, ff_ref, vf_ref, qb_ref, fb_ref, vb_ref, lbp_ref, s0_ref, tri_ref, e_ref, *rest,
                 tile, heads, aliased):
    of_ref, ob_ref, sout_ref, st_scr = rest[2:] if aliased else rest
    i = pl.program_id(2)
    c = HG_CHUNK
    n = tile // c

    @pl.when(i == 0)
    def _():
        for hh in range(heads):
            for direction in range(2):
                st_scr[hh, direction] = s0_ref[0, direction, hh].T

    def body(j, carry):
        chains = []
        for hh in range(heads):
            cs = slice(hh * HEAD_W, (hh + 1) * HEAD_W)
            for direction, q_ref, f_ref, v_ref, o_ref in ((0, qf_ref, ff_ref, vf_ref, of_ref),
                                                          (1, qb_ref, fb_ref, vb_ref, ob_ref)):
                r0 = pl.multiple_of((j if direction == 0 else n - 1 - j) * c, c)
                rows = pl.ds(r0, c)
                pre = _hgrn_chunk_pre(direction, q_ref[rows, cs], f_ref[rows, cs], v_ref[rows, cs],
                                      lbp_ref[direction, :, cs], tri_ref[direction])
                chains.append((hh, direction, o_ref, rows, cs, pre))
        att = jnp.dot(jnp.concatenate([ch[5][4] for ch in chains], axis=0), e_ref[...],
                      preferred_element_type=F32)
        for idx, (hh, direction, o_ref, rows, cs, pre) in enumerate(chains):
            o, st = _hgrn_chunk_post(*pre[:4], att[idx * c:(idx + 1) * c, :], st_scr[hh, direction])
            o_ref[rows, cs] = o
            st_scr[hh, direction] = st
        return carry

    lax.fori_loop(0, n, body, 0)

    @pl.when(i == pl.num_programs(2) - 1)
    def _():
        for hh in range(heads):
            for direction in range(2):
                sout_ref[0, direction, hh] = st_scr[hh, direction].T


def _hgrn_call(y, lbp, s0, tri, emat, *, seq, nseq, row0, tile, heads, prev=None):
    n = y.shape[0]
    nt = seq // tile
    w = heads * HEAD_W
    blk0 = row0 // tile

    def col(c0, rev):
        def index(b, g, i):
            t = nt - 1 - i if rev else i
            return (blk0 + b * nt + t, c0 // w + g)
        return pl.BlockSpec((tile, w), index)

    state_spec = pl.BlockSpec((1, 2, heads, HEAD_W, HEAD_W), lambda b, g, i: (b, 0, g, 0, 0))
    out_f = pl.BlockSpec((tile, w), lambda b, g, i: (blk0 + b * nt + i, g))
    out_b = pl.BlockSpec((tile, w), lambda b, g, i: (blk0 + b * nt + nt - 1 - i, g))
    in_specs = [col(COL_HQ, False), col(COL_FF, False), col(COL_HI, False),
                col(COL_HQ, True), col(COL_FB, True), col(COL_HI, True),
                pl.BlockSpec((2, 3, w), lambda b, g, i: (0, 0, g)),
                state_spec,
                pl.BlockSpec((2, HG_CHUNK, HG_CHUNK), lambda b, g, i: (0, 0, 0)),
                pl.BlockSpec((HG_CHUNK * HEAD_W, HG_CHUNK), lambda b, g, i: (0, 0))]
    args = [y, y, y, y, y, y, lbp, s0, tri, emat]
    aliases = {}
    if prev is not None:
        in_specs += [pl.BlockSpec(memory_space=pl.ANY), pl.BlockSpec(memory_space=pl.ANY)]
        aliases = {len(args): 0, len(args) + 1: 1}
        args += list(prev)
    return pl.pallas_call(
        functools.partial(_hgrn_kernel, tile=tile, heads=heads, aliased=prev is not None),
        grid=(nseq, HEADS // heads, nt),
        in_specs=in_specs,
        out_specs=[out_f, out_b, state_spec],
        out_shape=[jax.ShapeDtypeStruct((n, BRANCH_W), F32), jax.ShapeDtypeStruct((n, BRANCH_W), F32),
                   jax.ShapeDtypeStruct((nseq, 2, HEADS, HEAD_W, HEAD_W), F32)],
        scratch_shapes=[pltpu.VMEM((heads, 2, HEAD_W, HEAD_W), F32)],
        input_output_aliases=aliases,
        compiler_params=_params(("parallel", "parallel", "arbitrary")),
        name="hgrn",
    )(*args)


def _hfin_kernel(of_ref, ob_ref, g_ref, gn_ref, o_ref):
    for h in range(HEADS):
        cs = slice(h * HEAD_W, (h + 1) * HEAD_W)
        o = _rms_norm(of_ref[:, cs] + ob_ref[:, cs], gn_ref[:, cs])
        o_ref[:, cs] = (o * _silu(g_ref[:, cs])).astype(BF16)


def _hfin_call(of, ob, y, gnorm, tm):
    n = of.shape[0]
    row = pl.BlockSpec((tm, BRANCH_W), lambda i: (i, 0))
    return pl.pallas_call(
        _hfin_kernel,
        grid=(n // tm,),
        in_specs=[row, row, pl.BlockSpec((tm, BRANCH_W), lambda i: (i, COL_HG // BRANCH_W)),
                  pl.BlockSpec((1, BRANCH_W), lambda i: (0, 0))],
        out_specs=row,
        out_shape=jax.ShapeDtypeStruct((n, BRANCH_W), BF16),
        compiler_params=_params(("parallel",)),
        name="hfin",
    )(of, ob, y, gnorm)


def _rope(x, cos, sin_signed, lo_half):
    rot = jnp.where(lo_half, pltpu.roll(x, LANES - 16, 1), pltpu.roll(x, 16, 1))
    return x * cos + rot * sin_signed


def _prep_kernel(cq_ref, ckv_ref, kpe_ref, dq_ref, dk_ref, dv_ref, qn_ref, kvn_ref, wuq_ref,
                 cosm_ref, sinm_ref, cosd_ref, sind_ref, kd0_ref, vdt0_ref, ckv0_ref, kpe0_ref,
                 qcat_ref, ckvn_ref, kper_ref, qd_ref, kd_ref, vdt_ref, ckvk_ref, kpek_ref):
    del kd0_ref, vdt0_ref, ckv0_ref, kpe0_ref
    lo_half = (lax.broadcasted_iota(jnp.int32, (1, LANES), 1) % 32) < 16
    cq = _rms_norm(cq_ref[...], qn_ref[...]).astype(BF16)
    qm = jnp.dot(cq, wuq_ref[...], preferred_element_type=F32)
    cos_pe, sin_pe = cosm_ref[...], sinm_ref[...]
    for h in range(HEADS):
        c0 = h * MLA_QK_W
        qcat_ref[:, c0:c0 + LANES] = qm[:, c0:c0 + LANES].astype(BF16)
        pe = _rope(qm[:, c0 + LANES:c0 + 2 * LANES], cos_pe, sin_pe, lo_half)
        qcat_ref[:, c0 + LANES:c0 + 2 * LANES] = pe.astype(BF16)
    ckvn = _rms_norm(ckv_ref[...], kvn_ref[...])
    ckvn_ref[...] = ckvn
    ckvk_ref[...] = ckvn.astype(BF16)
    kper = _rope(kpe_ref[...], cos_pe, sin_pe, lo_half)
    kper_ref[...] = kper
    kpek_ref[...] = kper.astype(BF16)
    cos_d, sin_d = cosd_ref[...], sind_ref[...]
    for h in range(HEADS):
        cs = slice(h * LANES, (h + 1) * LANES)
        qd_ref[:, cs] = _rope(dq_ref[:, cs], cos_d, sin_d, lo_half).astype(BF16)
        kd_ref[:, cs] = _rope(dk_ref[:, cs], cos_d, sin_d, lo_half).astype(BF16)
        vdt_ref[cs, :] = dv_ref[:, cs].T.astype(BF16)


def _prep_call(y, qn, kvn, wuq, cosm, sinm, cosd, sind, key_bufs, key_blk, tm):
    n = y.shape[0]
    row = lambda w, c0: pl.BlockSpec((tm, w), lambda i, c0=c0, w=w: (i, c0 // w))
    full = lambda a: pl.BlockSpec(a.shape, lambda i: (0, 0))
    tab = pl.BlockSpec((tm, LANES), lambda i: (i, 0))
    out = lambda w: pl.BlockSpec((tm, w), lambda i: (i, 0))
    keyrow = lambda w: pl.BlockSpec((tm, w), lambda i: (key_blk(i), 0))
    any_spec = pl.BlockSpec(memory_space=pl.ANY)
    n_in = 13
    return pl.pallas_call(
        _prep_kernel,
        grid=(n // tm,),
        in_specs=[row(MLA_Q_RANK, COL_CQ), row(MLA_KV_RANK, COL_CKV), row(LANES, COL_KPE),
                  row(BRANCH_W, COL_DQ), row(BRANCH_W, COL_DK), row(BRANCH_W, COL_DV),
                  full(qn), full(kvn), full(wuq), tab, tab, tab, tab,
                  any_spec, any_spec, any_spec, any_spec],
        out_specs=[out(HEADS * MLA_QK_W), out(MLA_KV_RANK), out(LANES), out(BRANCH_W),
                   keyrow(BRANCH_W), pl.BlockSpec((BRANCH_W, tm), lambda i: (0, key_blk(i))),
                   keyrow(MLA_KV_RANK), keyrow(LANES)],
        out_shape=[jax.ShapeDtypeStruct((n, HEADS * MLA_QK_W), BF16),
                   jax.ShapeDtypeStruct((n, MLA_KV_RANK), F32),
                   jax.ShapeDtypeStruct((n, LANES), F32),
                   jax.ShapeDtypeStruct((n, BRANCH_W), BF16)]
                  + [jax.ShapeDtypeStruct(b.shape, b.dtype) for b in key_bufs],
        input_output_aliases={n_in + k: 4 + k for k in range(4)},
        compiler_params=_params(("parallel",)),
        name="prep",
    )(y, y, y, y, y, y, qn, kvn, wuq, cosm, sinm, cosd, sind, *key_bufs)


def _kvup_kernel(ckv_ref, kpe_ref, wk_ref, wvt_ref, kcat_ref, vt_ref):
    ckv = ckv_ref[...]
    kn = jnp.dot(ckv, wk_ref[...], preferred_element_type=F32)
    kpe = kpe_ref[...]
    for h in range(HEADS):
        c0 = h * MLA_QK_W
        kcat_ref[:, c0:c0 + LANES] = kn[:, h * LANES:(h + 1) * LANES].astype(BF16)
        kcat_ref[:, c0 + LANES:c0 + 2 * LANES] = kpe
    vt_ref[...] = lax.dot_general(wvt_ref[...], ckv, NT_DIMS, preferred_element_type=F32).astype(BF16)


def _kvup_call(ckv_all, kpe_all, wk, wvt, tk):
    r = ckv_all.shape[0]
    return pl.pallas_call(
        _kvup_kernel,
        grid=(r // tk,),
        in_specs=[pl.BlockSpec((tk, MLA_KV_RANK), lambda i: (i, 0)),
                  pl.BlockSpec((tk, LANES), lambda i: (i, 0)),
                  pl.BlockSpec(wk.shape, lambda i: (0, 0)),
                  pl.BlockSpec(wvt.shape, lambda i: (0, 0))],
        out_specs=[pl.BlockSpec((tk, HEADS * MLA_QK_W), lambda i: (i, 0)),
                   pl.BlockSpec((BRANCH_W, tk), lambda i: (0, i))],
        out_shape=[jax.ShapeDtypeStruct((r, HEADS * MLA_QK_W), BF16),
                   jax.ShapeDtypeStruct((BRANCH_W, r), BF16)],
        compiler_params=_params(("parallel",)),
        name="kvup",
    )(ckv_all, kpe_all, wk, wvt)


def _exp_cols(s, scale):
    e = jnp.exp2((s - jnp.max(s, axis=0, keepdims=True)) * (scale * math.log2(math.e)))
    return e, 1.0 / jnp.sum(e, axis=0, keepdims=True)


def _attn_kernel(qm_ref, km_ref, vmt_ref, qd_ref, kd_ref, vdt_ref, lam_ref, gn_ref, *rest,
                 heads, out_scale, aliased):
    om_ref, od_ref = rest[2:] if aliased else rest
    lam = lam_ref[:, 0:1]
    gn = gn_ref[...]
    lane = lax.broadcasted_iota(jnp.int32, (1, LANES), 1)
    for h in range(heads):
        qk = slice(h * MLA_QK_W, (h + 1) * MLA_QK_W)
        hw = slice(h * HEAD_W, (h + 1) * HEAD_W)
        s = lax.dot_general(km_ref[:, qk], qm_ref[:, qk], NT_DIMS, preferred_element_type=F32)
        e, r = _exp_cols(s, (MLA_NOPE + MLA_ROPE) ** -0.5)
        o_t = jnp.dot(vmt_ref[hw, :], e.astype(BF16), preferred_element_type=F32) * r
        om_ref[:, hw] = o_t.T.astype(BF16)

        q = qd_ref[:, hw]
        zero = jnp.zeros_like(q)
        k = kd_ref[:, hw]
        s1 = lax.dot_general(k, jnp.where(lane < DF_DH, q, zero), NT_DIMS, preferred_element_type=F32)
        s2 = lax.dot_general(k, jnp.where(lane >= DF_DH, q, zero), NT_DIMS, preferred_element_type=F32)
        e1, r1 = _exp_cols(s1, DF_DH ** -0.5)
        e2, r2 = _exp_cols(s2, DF_DH ** -0.5)
        o_t = (jnp.dot(vdt_ref[hw, :], e1.astype(BF16), preferred_element_type=F32) * r1
               - jnp.dot(vdt_ref[hw, :], e2.astype(BF16), preferred_element_type=F32) * (lam * r2))
        od_ref[:, hw] = (_rms_norm(o_t.T, gn) * out_scale).astype(BF16)


def _attn_call(qcat, qd, kcat, vmt, kd, vdt, lam, gn, out_scale, *, nseq, tq, q_tiles, q_row0, tk,
               k_block0, heads, prev=None):
    n = qcat.shape[0]
    qb0 = q_row0 // tq
    qmap = lambda b, g, i: (qb0 + b * q_tiles + i, g)
    kmap = lambda b, g, i: (k_block0 + b, g)
    vmap = lambda b, g, i: (g, k_block0 + b)
    in_specs = [pl.BlockSpec((tq, heads * MLA_QK_W), qmap),
                pl.BlockSpec((tk, heads * MLA_QK_W), kmap),
                pl.BlockSpec((heads * HEAD_W, tk), vmap),
                pl.BlockSpec((tq, heads * HEAD_W), qmap),
                pl.BlockSpec((tk, heads * HEAD_W), kmap),
                pl.BlockSpec((heads * HEAD_W, tk), vmap),
                pl.BlockSpec((1, LANES), lambda b, g, i: (0, 0)),
                pl.BlockSpec((1, LANES), lambda b, g, i: (0, 0))]
    args = [qcat, kcat, vmt, qd, kd, vdt, lam, gn]
    aliases = {}
    if prev is not None:
        in_specs += [pl.BlockSpec(memory_space=pl.ANY), pl.BlockSpec(memory_space=pl.ANY)]
        aliases = {len(args): 0, len(args) + 1: 1}
        args += list(prev)
    out_spec = pl.BlockSpec((tq, heads * HEAD_W), qmap)
    return pl.pallas_call(
        functools.partial(_attn_kernel, heads=heads, out_scale=out_scale, aliased=prev is not None),
        grid=(nseq, HEADS // heads, q_tiles),
        in_specs=in_specs,
        out_specs=[out_spec, out_spec],
        out_shape=[jax.ShapeDtypeStruct((n, BRANCH_W), BF16), jax.ShapeDtypeStruct((n, BRANCH_W), BF16)],
        input_output_aliases=aliases,
        compiler_params=_params(("parallel", "parallel", "arbitrary")),
        name="attn",
    )(*args)


def _merge_kernel(oh_ref, om_ref, od_ref, ga_ref, gb_ref, gc_ref, w_ref, o_ref):
    acc = jax.nn.sigmoid(ga_ref[...]) * jnp.dot(oh_ref[...], w_ref[0], preferred_element_type=F32)
    acc += jax.nn.sigmoid(gb_ref[...]) * jnp.dot(om_ref[...], w_ref[1], preferred_element_type=F32)
    acc += jax.nn.sigmoid(gc_ref[...]) * jnp.dot(od_ref[...], w_ref[2], preferred_element_type=F32)
    o_ref[...] = acc.astype(BF16)


def _merge_call(oh, om, od, y, wbr, layer, tm):
    n = oh.shape[0]
    d = wbr.shape[3]
    tn = d
    br = pl.BlockSpec((tm, BRANCH_W), lambda j, i: (i, 0))
    gate = lambda g: pl.BlockSpec((tm, tn), lambda j, i, g=g: (i, (COL_GA + g * d) // tn + j))
    return pl.pallas_call(
        _merge_kernel,
        grid=(d // tn, n // tm),
        in_specs=[br, br, br, gate(0), gate(1), gate(2),
                  pl.BlockSpec((None, 3, BRANCH_W, tn), lambda j, i: (layer, 0, 0, j))],
        out_specs=pl.BlockSpec((tm, tn), lambda j, i: (i, j)),
        out_shape=jax.ShapeDtypeStruct((n, d), BF16),
        compiler_params=_params(("parallel", "parallel")),
        name="merge",
    )(oh, om, od, y, y, y, wbr)


def _outproj_kernel(m_ref, x_ref, mod_ref, w_ref, g_ref, b_ref, x1_ref, h2_ref, h2t_ref, *, alpha):
    gate1 = mod_ref[0, 2:3, :]
    shift2, scale2 = mod_ref[0, 3:4, :], mod_ref[0, 4:5, :]
    y = jnp.dot(m_ref[...], w_ref[...], preferred_element_type=F32)
    x1 = _layer_norm(alpha * x_ref[...] + gate1 * y) * g_ref[...] + b_ref[...]
    x1_ref[...] = x1
    h2 = _layer_norm(x1) * (1.0 + scale2) + shift2
    h2_ref[...] = h2.astype(BF16)
    h2t_ref[...] = h2.T.astype(BF16)


def _outproj_call(merged, x, mod, w, layer, g, b, mod_idx, alpha, tm):
    n, d = x.shape
    row = pl.BlockSpec((tm, d), lambda i: (i, 0))
    vec = pl.BlockSpec((1, d), lambda i: (0, 0))
    return pl.pallas_call(
        functools.partial(_outproj_kernel, alpha=alpha),
        grid=(n // tm,),
        in_specs=[row, row, pl.BlockSpec((1, N_MOD, d), lambda i: (mod_idx(i), 0, 0)),
                  pl.BlockSpec((None, d, d), lambda i: (layer, 0, 0)), vec, vec],
        out_specs=[row, row, pl.BlockSpec((d, tm), lambda i: (0, i))],
        out_shape=[jax.ShapeDtypeStruct((n, d), F32), jax.ShapeDtypeStruct((n, d), BF16),
                   jax.ShapeDtypeStruct((d, n), BF16)],
        compiler_params=_params(("parallel",)),
        name="outproj",
    )(merged, x, mod, w, g, b)


def _peerq_kernel(h_ref, w_ref, sk_ref, st_ref):
    q = jnp.dot(h_ref[...], w_ref[...], preferred_element_type=F32).astype(BF16)
    for hp in range(2 * HEADS):
        st_ref[hp // 2, hp % 2] = lax.dot_general(sk_ref[hp % 2], q[:, hp * LANES:(hp + 1) * LANES], NT_DIMS,
                                                  preferred_element_type=F32)


def _peerq_call(h2, wq, layer, subkeys, tm):
    n, d = h2.shape
    return pl.pallas_call(
        _peerq_kernel,
        grid=(n // tm,),
        in_specs=[pl.BlockSpec((tm, d), lambda i: (i, 0)),
                  pl.BlockSpec((None,) + wq.shape[1:], lambda i: (layer, 0, 0)),
                  pl.BlockSpec(subkeys.shape, lambda i: (0, 0, 0))],
        out_specs=pl.BlockSpec((HEADS, 2, PEER_NKEYS, tm), lambda i: (0, 0, 0, i)),
        out_shape=jax.ShapeDtypeStruct((HEADS, 2, PEER_NKEYS, n), F32),
        compiler_params=_params(("parallel",)),
        name="peerq",
    )(h2, wq, subkeys)


def _top_values(x, count):
    rows = lax.broadcasted_iota(jnp.int32, x.shape, 0).astype(F32)
    vals = []
    for r in range(count):
        m = jnp.max(x, axis=0, keepdims=True)
        vals.append(m)
        if r + 1 < count:
            first = jnp.min(jnp.where(x == m, rows, float(x.shape[0])), axis=0, keepdims=True)
            x = jnp.where(rows == first, -jnp.inf, x)
    return vals


def _topk_kernel(st_ref, e1_ref, e2_ref, th_ref, v_scr, cand_scr):
    row8 = lax.broadcasted_iota(jnp.int32, (8, st_ref.shape[3]), 0)
    for h in range(HEADS):
        s1, s2 = st_ref[h, 0], st_ref[h, 1]
        v1 = _top_values(s1, PEER_TOPK)
        v2 = _top_values(s2, PEER_TOPK)
        for r in range(PEER_TOPK):
            v_scr[0, r:r + 1, :] = v1[r]
            v_scr[1, r:r + 1, :] = v2[r]
        cand_scr[0:16, :] = v1[0] + v_scr[1]
        for a in range(1, 8):
            pair = v1[a] + v_scr[1, 0:8, :]
            cand_scr[8 + 8 * a:16 + 8 * a, :] = jnp.where(row8 < PEER_TOPK // (a + 1), pair, -jnp.inf)
        cand_scr[72:80, :] = v_scr[0, 8:16, :] + v2[0]
        top = _top_values(cand_scr[...], PEER_TOPK)
        z = jnp.exp(top[0] - top[0])
        for r in range(1, PEER_TOPK):
            z = z + jnp.exp(top[r] - top[0])
        e1_ref[h] = jnp.exp(s1 - v1[0]) * (1.0 / z)
        e2_ref[h] = jnp.exp(s2 - v2[0])
        th = jnp.full(s1.shape, jnp.inf, F32)
        for b in range(PEER_TOPK):
            th = jnp.where(s1 + v2[b] >= top[PEER_TOPK - 1], v2[b], th)
        th_ref[h] = th


def _topk_call(st, tm):
    n = st.shape[3]
    out = pl.BlockSpec((HEADS, PEER_NKEYS, tm), lambda i: (0, 0, i))
    shape = jax.ShapeDtypeStruct((HEADS, PEER_NKEYS, n), F32)
    return pl.pallas_call(
        _topk_kernel,
        grid=(n // tm,),
        in_specs=[pl.BlockSpec((HEADS, 2, PEER_NKEYS, tm), lambda i: (0, 0, 0, i))],
        out_specs=[out, out, out],
        out_shape=[shape, shape, shape],
        scratch_shapes=[pltpu.VMEM((2, PEER_TOPK, tm), F32), pltpu.VMEM((80, tm), F32)],
        compiler_params=_params(("parallel",)),
        name="topk",
    )(st)


def _peer_kernel(ht_ref, s2_ref, e1_ref, e2_ref, th_ref, u_ref, vt_ref, ot_ref, ga_scr, *, rows_per_step):
    j = pl.program_id(1)
    nj = pl.num_programs(1) - 1
    slot = j % 2
    tm = ht_ref.shape[1]

    @pl.when(j == 0)
    def _():
        ot_ref[...] = jnp.zeros_like(ot_ref)
        ga_scr[1] = jnp.zeros(ga_scr.shape[1:], BF16)

    ot_ref[...] += jnp.dot(vt_ref[...], ga_scr[1 - slot], preferred_element_type=F32)

    a_t = jnp.dot(u_ref[...], ht_ref[...], preferred_element_type=F32)
    i0 = jnp.minimum(j, nj - 1) * rows_per_step
    for r in range(rows_per_step):
        i = i0 + r
        g = jnp.zeros((PEER_NKEYS, tm), F32)
        for h in range(HEADS):
            keep = s2_ref[h] >= th_ref[h, pl.ds(i, 1), :]
            g = g + jnp.where(keep, e2_ref[h], 0.0) * e1_ref[h, pl.ds(i, 1), :]
        rows = slice(r * PEER_NKEYS, (r + 1) * PEER_NKEYS)
        ga_scr[slot, rows, :] = (g * _gelu(a_t[rows, :])).astype(BF16)


def _peer_call(h2t, st, e1, e2, th, u, vt, layer, tm, rows_per_step):
    d, n = h2t.shape
    te = rows_per_step * PEER_NKEYS
    nj = u.shape[1] // te
    once = pl.Buffered(1)
    tok = pl.BlockSpec((HEADS, PEER_NKEYS, tm), lambda i, j: (0, 0, i), pipeline_mode=once)
    return pl.pallas_call(
        functools.partial(_peer_kernel, rows_per_step=rows_per_step),
        grid=(n // tm, nj + 1),
        in_specs=[pl.BlockSpec((d, tm), lambda i, j: (0, i), pipeline_mode=once),
                  pl.BlockSpec((HEADS, None, PEER_NKEYS, tm), lambda i, j: (0, 1, 0, i), pipeline_mode=once),
                  tok, tok, tok,
                  pl.BlockSpec((None, te, d), lambda i, j: (layer, jnp.minimum(j, nj - 1), 0)),
                  pl.BlockSpec((None, d, te), lambda i, j: (layer, 0, jnp.maximum(j - 1, 0)))],
        out_specs=pl.BlockSpec((d, tm), lambda i, j: (0, i)),
        out_shape=jax.ShapeDtypeStruct((d, n), F32),
        scratch_shapes=[pltpu.VMEM((2, te, tm), BF16)],
        compiler_params=_params(("parallel", "arbitrary")),
        name="peer",
    )(h2t, st, e1, e2, th, u, vt)


def _ln2_kernel(x_ref, yt_ref, mod_ref, g_ref, b_ref, o_ref, *, alpha):
    gate2 = mod_ref[0, 5:6, :]
    o_ref[...] = _layer_norm(alpha * x_ref[...] + gate2 * yt_ref[...].T) * g_ref[...] + b_ref[...]


def _ln2_call(x1, y2t, mod, g, b, mod_idx, alpha, tm):
    n, d = x1.shape
    row = pl.BlockSpec((tm, d), lambda i: (i, 0))
    vec = pl.BlockSpec((1, d), lambda i: (0, 0))
    return pl.pallas_call(
        functools.partial(_ln2_kernel, alpha=alpha),
        grid=(n // tm,),
        in_specs=[row, pl.BlockSpec((d, tm), lambda i: (0, i)),
                  pl.BlockSpec((1, N_MOD, d), lambda i: (mod_idx(i), 0, 0)), vec, vec],
        out_specs=row,
        out_shape=jax.ShapeDtypeStruct((n, d), F32),
        compiler_params=_params(("parallel",)),
        name="ln2",
    )(x1, y2t, mod, g, b)


def _rope_tables(n_ctx, n_lat_seq, lat_seq):
    quarter = 16
    t = jnp.arange(lat_seq)
    rowp = (t // GRID_W).astype(F32)
    colp = (t % GRID_W).astype(F32)
    inv_freq = ROPE_BASE ** (-jnp.arange(quarter, dtype=F32) / quarter)
    ang = jnp.stack([rowp[:, None] * inv_freq, colp[:, None] * inv_freq], axis=1)
    ang = jnp.concatenate([ang, ang], axis=-1).reshape(lat_seq, 64)
    sign = jnp.where((jnp.arange(64) % 32) < 16, -1.0, 1.0).astype(F32)
    cos, sin = jnp.cos(ang), jnp.sin(ang) * sign
    ones, zeros = jnp.ones((lat_seq, 64), F32), jnp.zeros((lat_seq, 64), F32)

    def full(tab, ident):
        lat = jnp.tile(tab, (n_lat_seq, 1))
        return jnp.concatenate([jnp.full((n_ctx, LANES), ident, F32), lat], axis=0)

    cosd = full(jnp.concatenate([cos, cos], axis=1), 1.0)
    sind = full(jnp.concatenate([sin, sin], axis=1), 0.0)
    cosm = full(jnp.concatenate([cos, ones], axis=1), 1.0)
    sinm = full(jnp.concatenate([sin, zeros], axis=1), 0.0)
    return cosd, sind, cosm, sinm


def kernel(x_prompt, x_sample, cache_mla_ckv, cache_mla_kpe, cache_diff_k, cache_diff_v, state_hgrn,
           c, c_ctx, w_ada, b_ada, w_in, hg_lb_logits, hg_norm, mla_q_norm, w_uq, mla_kv_norm, w_ukv,
           df_lambda, df_norm, w_branch, w_out, ln1_g, ln1_b, ln2_g, ln2_b,
           peer_wq, peer_subkeys, peer_u, peer_v):
    nb, seq, d = x_prompt.shape
    db, dseq, _ = x_sample.shape
    depth = w_in.shape[0]
    past = cache_mla_ckv.shape[2]
    n_ctx, n_lat = nb * seq, db * dseq
    n = n_ctx + n_lat
    tk_lat = past + dseq
    tm = 512
    tm_in = 2 * tm if n_ctx % (2 * tm) == 0 and dseq % (2 * tm) == 0 else tm
    tq = 256
    hg_tile = min(dseq, 512)
    assert n_ctx % tm == 0 and dseq % tm == 0 and (db * tk_lat) % seq == 0 and seq % tq == 0
    alpha = (2 * depth) ** 0.25

    def mod_idx(i, tile=tm):
        return jnp.where(i < n_ctx // tile, 0, 1 + (i - n_ctx // tile) // (dseq // tile))

    assert past % tm == 0

    def key_blk(i):
        j = i - n_ctx // tm
        lat = (j // (dseq // tm)) * (tk_lat // tm) + past // tm + j % (dseq // tm)
        return jnp.where(i < n_ctx // tm, db * tk_lat // tm + i, lat)

    lb = jnp.cumsum(jax.nn.softmax(hg_lb_logits.astype(F32), axis=0), axis=0)
    lb = lb - lb[0:1]
    lbp = jnp.stack([jnp.log(lb), jnp.log1p(-lb), 1.0 - lb], axis=2)
    lq = df_lambda.astype(F32)
    lam_init = np.array([0.8 - 0.6 * math.exp(-0.3 * l) for l in range(depth)], np.float32)
    lam = jnp.exp(jnp.sum(lq[:, 0] * lq[:, 1], axis=-1)) - jnp.exp(jnp.sum(lq[:, 2] * lq[:, 3], axis=-1)) + lam_init

    kpe0 = 5 * 1024 + MLA_Q_RANK + MLA_KV_RANK
    w_in_p = jnp.concatenate(
        [w_in[:, :, :5 * 1024], w_in[:, :, kpe0 + MLA_ROPE:], w_in[:, :, 5 * 1024:kpe0],
         w_in[:, :, kpe0:kpe0 + MLA_ROPE],
         jnp.zeros((depth, d, IN_W_PAD - w_in.shape[2]), w_in.dtype)], axis=2).astype(BF16)
    wq4 = w_uq.reshape(depth, MLA_Q_RANK, HEADS, MLA_NOPE + MLA_ROPE)
    w_uq_p = jnp.concatenate(
        [wq4, jnp.zeros((depth, MLA_Q_RANK, HEADS, MLA_QK_W - MLA_NOPE - MLA_ROPE), w_uq.dtype)],
        axis=3).reshape(depth, MLA_Q_RANK, HEADS * MLA_QK_W).astype(BF16)
    wkv4 = w_ukv.reshape(depth, MLA_KV_RANK, HEADS, 2 * HEAD_W)
    w_uk = wkv4[..., :HEAD_W].reshape(depth, MLA_KV_RANK, BRANCH_W).astype(BF16)
    w_uvt = jnp.swapaxes(wkv4[..., HEAD_W:].reshape(depth, MLA_KV_RANK, BRANCH_W), 1, 2).astype(BF16)
    w_branch_b, w_out_b, peer_wq_b = w_branch.astype(BF16), w_out.astype(BF16), peer_wq.astype(BF16)
    subkeys_b, peer_u_b = peer_subkeys.astype(BF16), peer_u.astype(BF16)
    peer_vt_b = jnp.swapaxes(peer_v.astype(BF16), 1, 2)

    tri = jnp.stack([jnp.tril(jnp.ones((HG_CHUNK, HG_CHUNK), F32)),
                     jnp.triu(jnp.ones((HG_CHUNK, HG_CHUNK), F32))]).astype(BF16)
    emat = jnp.repeat(jnp.eye(HG_CHUNK, dtype=F32), HEAD_W, axis=0).astype(BF16)
    cosd, sind, cosm, sinm = _rope_tables(n_ctx, db, dseq)

    cvec = jnp.concatenate([c_ctx[None, :], c, jnp.zeros((8 - 1 - db, d), F32)], axis=0)
    mods = _ada_call(cvec, w_ada, b_ada)[:, :1 + db].reshape(depth, 1 + db, N_MOD, d)

    x = jnp.concatenate([x_prompt.reshape(n_ctx, d), x_sample.reshape(n_lat, d)], axis=0)
    zero_state = jnp.zeros((nb, 2, HEADS, HEAD_W, HEAD_W), F32)
    new_ckv, new_kpe, new_dk, new_dv, new_st = [], [], [], [], []
    for l in range(depth):
        mod = mods[l]
        y = _inproj_call(x, mod, w_in_p, l, functools.partial(mod_idx, tile=tm_in), tm_in)

        of, ob, st_ctx = _hgrn_call(y, lbp[l], zero_state, tri, emat, seq=seq, nseq=nb, row0=0,
                                    tile=seq, heads=4)
        of, ob, _ = _hgrn_call(y, lbp[l], state_hgrn[:, l], tri, emat, seq=dseq, nseq=db, row0=n_ctx,
                               tile=hg_tile, heads=4, prev=(of, ob))
        oh = _hfin_call(of, ob, y, hg_norm[l][None, :], tm)

        seed = lambda cache: jnp.pad(jnp.pad(cache.astype(BF16), ((0, 0), (0, dseq), (0, 0)))
                                     .reshape(db * tk_lat, cache.shape[2]), ((0, n_ctx), (0, 0)))
        key_bufs = (seed(cache_diff_k[:, l].reshape(db, past, BRANCH_W)),
                    seed(cache_diff_v[:, l].reshape(db, past, BRANCH_W)).T,
                    seed(cache_mla_ckv[:, l]),
                    seed(jnp.pad(cache_mla_kpe[:, l], ((0, 0), (0, 0), (0, LANES - MLA_ROPE)))))
        qcat, ckvn, kper, qd, kd_all, vdt, ckv_all, kpe_all = _prep_call(
            y, mla_q_norm[l][None, :], mla_kv_norm[l][None, :], w_uq_p[l], cosm, sinm, cosd, sind,
            key_bufs, key_blk, tm)
        kcat, vmt = _kvup_call(ckv_all, kpe_all, w_uk[l], w_uvt[l], seq)
        lam_row = jnp.full((1, LANES), lam[l], F32)
        attn_args = (qcat, qd, kcat, vmt, kd_all, vdt, lam_row, df_norm[l][None, :], float(1.0 - lam_init[l]))
        om, od = _attn_call(*attn_args, nseq=nb, tq=seq, q_tiles=1, q_row0=0, tk=seq,
                            k_block0=db * tk_lat // seq, heads=HEADS)
        om, od = _attn_call(*attn_args, nseq=db, tq=tq, q_tiles=dseq // tq, q_row0=n_ctx,
                            tk=tk_lat, k_block0=0, heads=1, prev=(om, od))

        merged = _merge_call(oh, om, od, y, w_branch_b, l, tm // 2)
        x1, h2, h2t = _outproj_call(merged, x, mod, w_out_b, l, ln1_g[l][None, :], ln1_b[l][None, :],
                               functools.partial(mod_idx, tile=256), alpha, 256)

        st = _peerq_call(h2, peer_wq_b, l, subkeys_b[l], tm)
        e1, e2, th = _topk_call(st, tm)
        y2t = _peer_call(h2t, st, e1, e2, th, peer_u_b, peer_vt_b, l, tm, 8)
        x = _ln2_call(x1, y2t, mod, ln2_g[l][None, :], ln2_b[l][None, :], mod_idx, alpha, tm)

        new_ckv.append(ckvn[:n_ctx].reshape(nb, seq, MLA_KV_RANK))
        new_kpe.append(kper[:n_ctx, :MLA_ROPE].reshape(nb, seq, MLA_ROPE))
        new_dk.append(y[:n_ctx, COL_DK:COL_DK + BRANCH_W].reshape(nb, seq, HEADS, HEAD_W))
        new_dv.append(y[:n_ctx, COL_DV:COL_DV + BRANCH_W].reshape(nb, seq, HEADS, HEAD_W))
        new_st.append(st_ctx)

    return (x[:n_ctx].reshape(nb, seq, d), x[n_ctx:].reshape(db, dseq, d),
            jnp.stack(new_ckv, axis=1), jnp.stack(new_kpe, axis=1), jnp.stack(new_dk, axis=1),
            jnp.stack(new_dv, axis=1), jnp.stack(new_st, axis=1))
```
